```python
import math
import jax, jax.numpy as jnp
from jax import lax
import numpy as np


D_MODEL = 1024
BATCH = 16
SEQ = 256
DEPTH = 2
DEC_BATCH = 2
DEC_SEQ = 2048
PAST_LEN = 256

GRID_W = 64
HEAD_DIM = 64
DIFF_HEADS = 4
GQA_HEADS = 8
GQA_KV_HEADS = 2
GQA_GROUP = GQA_HEADS // GQA_KV_HEADS
DIFF_WIDTH = DIFF_HEADS * 2 * HEAD_DIM
GQA_WIDTH = GQA_HEADS * HEAD_DIM
KV_WIDTH = GQA_KV_HEADS * HEAD_DIM
MIX_WIDTH = DIFF_WIDTH + GQA_WIDTH
IN_COLS = 3 * DIFF_WIDTH + GQA_WIDTH + 2 * KV_WIDTH
IN_SPLITS = (DIFF_WIDTH, 2 * DIFF_WIDTH, 3 * DIFF_WIDTH,
             3 * DIFF_WIDTH + GQA_WIDTH, 3 * DIFF_WIDTH + GQA_WIDTH + KV_WIDTH)
D_FF = 2816
N_EXPERTS = 8
TOP_K = 2
N_DENSE = (DEPTH + 1) // 2
N_MOE = DEPTH // 2
Q_BLOCK = 128
ROPE_THETA = 10000.0
EPS = 1e-6
DEEPNORM_ALPHA = (2 * DEPTH) ** 0.25
DEEPNORM_BETA = (8 * DEPTH) ** -0.25

kernel_name = "hybrid_diffusion_diffattn_gqa_moe_step"


def layer_norm(x, g, b):
    xf = x.astype(jnp.float32)
    mu = jnp.mean(xf, axis=-1, keepdims=True)
    var = jnp.mean(jnp.square(xf - mu), axis=-1, keepdims=True)
    return ((xf - mu) * lax.rsqrt(var + EPS)).astype(x.dtype) * g + b


def rms_norm(x, g):
    xf = x.astype(jnp.float32)
    return (xf * lax.rsqrt(jnp.mean(xf * xf, axis=-1, keepdims=True) + EPS)).astype(x.dtype) * g


def axial_rope_tables(n_tokens, dtype):
    rows = n_tokens // GRID_W
    row = jnp.repeat(jnp.arange(rows, dtype=jnp.float32), GRID_W)
    col = jnp.tile(jnp.arange(GRID_W, dtype=jnp.float32), rows)
    n_freq = HEAD_DIM // 4
    inv = ROPE_THETA ** (-jnp.arange(n_freq, dtype=jnp.float32) / n_freq)
    ar = row[:, None] * inv
    ac = col[:, None] * inv
    ang = jnp.concatenate([ar, ar, ac, ac], axis=-1)
    return jnp.cos(ang).astype(dtype), jnp.sin(ang).astype(dtype)


def apply_axial_rope(x, cos, sin):
    x1, x2, x3, x4 = jnp.split(x, 4, axis=-1)
    rot = jnp.concatenate([-x2, x1, -x4, x3], axis=-1)
    shape = (1, cos.shape[0]) + (1,) * (x.ndim - 3) + (HEAD_DIM,)
    return x * cos.reshape(shape) + rot * sin.reshape(shape)


def sweep_query_blocks(fn, q):
    b, n = q.shape[:2]
    nb = n // Q_BLOCK
    blocks = jnp.moveaxis(q.reshape((b, nb, Q_BLOCK) + q.shape[2:]), 1, 0)
    out = lax.map(fn, blocks)
    return jnp.moveaxis(out, 0, 1).reshape((b, n) + out.shape[3:])


def diff_lambda_value(lp, lam_init):
    lpf = lp.astype(jnp.float32)
    return jnp.exp(jnp.sum(lpf[0] * lpf[1])) - jnp.exp(jnp.sum(lpf[2] * lpf[3])) + lam_init


def diff_attention(q, k, v, lam):
    scale = HEAD_DIM ** -0.5
    def block(qb):
        s = jnp.einsum('bqhcd,bkhcd->bchqk', qb, k).astype(jnp.float32) * scale
        p = jax.nn.softmax(s, axis=-1)
        a = p[:, 0] - lam * p[:, 1]
        return jnp.einsum('bhqk,bkhe->bqhe', a.astype(v.dtype), v)
    return sweep_query_blocks(block, q)


def gqa_attention(q, k, v):
    scale = HEAD_DIM ** -0.5
    def block(qb):
        b, nq = qb.shape[:2]
        qg = qb.reshape(b, nq, GQA_KV_HEADS, GQA_GROUP, HEAD_DIM)
        s = jnp.einsum('bqkgd,btkd->bkgqt', qg, k).astype(jnp.float32) * scale
        p = jax.nn.softmax(s, axis=-1).astype(v.dtype)
        o = jnp.einsum('bkgqt,btkd->bqkgd', p, v)
        return o.reshape(b, nq, GQA_HEADS, HEAD_DIM)
    return sweep_query_blocks(block, q)


def project_heads(h, w_in, qk_gain):
    b, n, _ = h.shape
    a_q, a_k, a_v, b_q, b_k, b_v = jnp.split(h @ w_in, list(IN_SPLITS), axis=-1)
    a_q = a_q.reshape(b, n, DIFF_HEADS, 2, HEAD_DIM)
    a_k = a_k.reshape(b, n, DIFF_HEADS, 2, HEAD_DIM)
    a_v = a_v.reshape(b, n, DIFF_HEADS, 2 * HEAD_DIM)
    b_q = rms_norm(b_q.reshape(b, n, GQA_HEADS, HEAD_DIM), qk_gain[0])
    b_k = rms_norm(b_k.reshape(b, n, GQA_KV_HEADS, HEAD_DIM), qk_gain[1])
    b_v = b_v.reshape(b, n, GQA_KV_HEADS, HEAD_DIM)
    return a_q, a_k, a_v, b_q, b_k, b_v


def mixer_output(a_q, a_k, a_v, b_q, b_k, b_v, lam, lam_init, subln_gain, w_out):
    o_a = rms_norm(diff_attention(a_q, a_k, a_v, lam), subln_gain) * (1.0 - lam_init)
    o_b = gqa_attention(b_q, b_k, b_v)
    b, n = o_a.shape[:2]
    o = jnp.concatenate([o_a.reshape(b, n, DIFF_WIDTH), o_b.reshape(b, n, GQA_WIDTH)], axis=-1)
    return o @ w_out


def ada_modulation(cond, w_ada, b_ada):
    m = jax.nn.silu(cond) @ w_ada + b_ada
    return jnp.split(m[..., None, :], 6, axis=-1)


def swiglu(h, wg, wu, wd):
    return (jax.nn.silu(h @ wg) * (h @ wu)) @ wd


def moe_swiglu(h, w_router, wg, wu, wd):
    logits = (h @ w_router).astype(jnp.float32)
    top_v, top_i = lax.top_k(logits, TOP_K)
    top_p = jax.nn.softmax(top_v, axis=-1)
    gates = jnp.sum(jax.nn.one_hot(top_i, N_EXPERTS, dtype=jnp.float32) * top_p[..., None], axis=-2)
    act = jax.nn.silu(jnp.einsum('bnd,edf->bnef', h, wg)) * jnp.einsum('bnd,edf->bnef', h, wu)
    act = act * gates.astype(h.dtype)[..., None]
    return jnp.einsum('bnef,efd->bnd', act, wd)


def channel_mixer(l, h, w_ffn_gate, w_ffn_up, w_ffn_down, w_router, w_moe_gate, w_moe_up, w_moe_down):
    if l % 2 == 0:
        i = l // 2
        return swiglu(h, w_ffn_gate[i], w_ffn_up[i], w_ffn_down[i])
    i = l // 2
    return moe_swiglu(h, w_router[i], w_moe_gate[i], w_moe_up[i], w_moe_down[i])


def setup_inputs(seed: int = 0) -> dict:
    key = jax.random.key(seed)
    ks = jax.random.split(key, 24)
    nrm = jax.random.normal
    f32 = jnp.float32
    col_scale = jnp.concatenate([
        jnp.ones((2 * DIFF_WIDTH,), f32), jnp.full((DIFF_WIDTH,), DEEPNORM_BETA, f32),
        jnp.ones((GQA_WIDTH + KV_WIDTH,), f32), jnp.full((KV_WIDTH,), DEEPNORM_BETA, f32)])
    return {
        "x_prompt": nrm(ks[0], (BATCH, SEQ, D_MODEL), f32),
        "x_sample": nrm(ks[1], (DEC_BATCH, DEC_SEQ, D_MODEL), f32),
        "cache_a_k": nrm(ks[2], (DEC_BATCH, DEPTH, PAST_LEN, DIFF_HEADS, 2, HEAD_DIM), f32),
        "cache_a_v": 0.5 * nrm(ks[3], (DEC_BATCH, DEPTH, PAST_LEN, DIFF_HEADS, 2 * HEAD_DIM), f32),
        "cache_b_k": nrm(ks[4], (DEC_BATCH, DEPTH, PAST_LEN, GQA_KV_HEADS, HEAD_DIM), f32),
        "cache_b_v": 0.5 * nrm(ks[5], (DEC_BATCH, DEPTH, PAST_LEN, GQA_KV_HEADS, HEAD_DIM), f32),
        "c": nrm(ks[6], (DEC_BATCH, D_MODEL), f32),
        "c_ctx": nrm(ks[7], (D_MODEL,), f32),
        "w_ada": nrm(ks[8], (DEPTH, D_MODEL, 6 * D_MODEL), f32) * D_MODEL ** -0.5,
        "b_ada": 0.01 * nrm(ks[9], (DEPTH, 6 * D_MODEL), f32),
        "w_in": nrm(ks[10], (DEPTH, D_MODEL, IN_COLS), f32) * D_MODEL ** -0.5 * col_scale,
        "w_out": nrm(ks[11], (DEPTH, MIX_WIDTH, D_MODEL), f32) * MIX_WIDTH ** -0.5 * DEEPNORM_BETA,
        "diff_lambda": 0.1 * nrm(ks[12], (DEPTH, 4, HEAD_DIM), f32),
        "diff_subln": 1.0 + 0.02 * nrm(ks[13], (DEPTH, 2 * HEAD_DIM), f32),
        "qk_norm_gain": 1.0 + 0.02 * nrm(ks[14], (DEPTH, 2, HEAD_DIM), f32),
        "ln_gain": 1.0 + 0.02 * nrm(ks[15], (DEPTH, 2, D_MODEL), f32),
        "ln_bias": 0.02 * nrm(ks[16], (DEPTH, 2, D_MODEL), f32),
        "w_ffn_gate": nrm(ks[17], (N_DENSE, D_MODEL, D_FF), f32) * D_MODEL ** -0.5,
        "w_ffn_up": nrm(ks[18], (N_DENSE, D_MODEL, D_FF), f32) * D_MODEL ** -0.5,
        "w_ffn_down": nrm(ks[19], (N_DENSE, D_FF, D_MODEL), f32) * D_FF ** -0.5 * DEEPNORM_BETA,
        "w_router": nrm(ks[20], (N_MOE, D_MODEL, N_EXPERTS), f32) * D_MODEL ** -0.5,
        "w_moe_gate": nrm(ks[21], (N_MOE, N_EXPERTS, D_MODEL, D_FF), f32) * D_MODEL ** -0.5,
        "w_moe_up": nrm(ks[22], (N_MOE, N_EXPERTS, D_MODEL, D_FF), f32) * D_MODEL ** -0.5,
        "w_moe_down": nrm(ks[23], (N_MOE, N_EXPERTS, D_FF, D_MODEL), f32) * D_FF ** -0.5 * DEEPNORM_BETA,
    }


def reference(x_prompt, x_sample, cache_a_k, cache_a_v, cache_b_k, cache_b_v, c, c_ctx,
              w_ada, b_ada, w_in, w_out, diff_lambda, diff_subln, qk_norm_gain, ln_gain, ln_bias,
              w_ffn_gate, w_ffn_up, w_ffn_down, w_router, w_moe_gate, w_moe_up, w_moe_down):
    rope_cos, rope_sin = axial_rope_tables(x_sample.shape[1], x_sample.dtype)
    xp, xs = x_prompt, x_sample
    ctx_a_k, ctx_a_v, ctx_b_k, ctx_b_v = [], [], [], []
    for l in range(DEPTH):
        lam_init = 0.8 - 0.6 * math.exp(-0.3 * l)
        lam = diff_lambda_value(diff_lambda[l], lam_init)
        p_sh1, p_sc1, p_g1, p_sh2, p_sc2, p_g2 = ada_modulation(c_ctx, w_ada[l], b_ada[l])
        s_sh1, s_sc1, s_g1, s_sh2, s_sc2, s_g2 = ada_modulation(c, w_ada[l], b_ada[l])

        hp = xp * (1 + p_sc1) + p_sh1
        a_q, a_k, a_v, b_q, b_k, b_v = project_heads(hp, w_in[l], qk_norm_gain[l])
        ctx_a_k.append(a_k)
        ctx_a_v.append(a_v)
        ctx_b_k.append(b_k)
        ctx_b_v.append(b_v)
        mix_p = mixer_output(a_q, a_k, a_v, b_q, b_k, b_v, lam, lam_init, diff_subln[l], w_out[l])
        xp = layer_norm(DEEPNORM_ALPHA * xp + p_g1 * mix_p, ln_gain[l, 0], ln_bias[l, 0])

        hs = xs * (1 + s_sc1) + s_sh1
        a_q, a_k, a_v, b_q, b_k, b_v = project_heads(hs, w_in[l], qk_norm_gain[l])
        a_q = apply_axial_rope(a_q, rope_cos, rope_sin)
        a_k = apply_axial_rope(a_k, rope_cos, rope_sin)
        b_q = apply_axial_rope(b_q, rope_cos, rope_sin)
        b_k = apply_axial_rope(b_k, rope_cos, rope_sin)
        a_k = jnp.concatenate([a_k, cache_a_k[:, l]], axis=1)
        a_v = jnp.concatenate([a_v, cache_a_v[:, l]], axis=1)
        b_k = jnp.concatenate([b_k, cache_b_k[:, l]], axis=1)
        b_v = jnp.concatenate([b_v, cache_b_v[:, l]], axis=1)
        mix_s = mixer_output(a_q, a_k, a_v, b_q, b_k, b_v, lam, lam_init, diff_subln[l], w_out[l])
        xs = layer_norm(DEEPNORM_ALPHA * xs + s_g1 * mix_s, ln_gain[l, 0], ln_bias[l, 0])

        hp = xp * (1 + p_sc2) + p_sh2
        ffn_p = channel_mixer(l, hp, w_ffn_gate, w_ffn_up, w_ffn_down,
                              w_router, w_moe_gate, w_moe_up, w_moe_down)
        xp = layer_norm(DEEPNORM_ALPHA * xp + p_g2 * ffn_p, ln_gain[l, 1], ln_bias[l, 1])
        hs = xs * (1 + s_sc2) + s_sh2
        ffn_s = channel_mixer(l, hs, w_ffn_gate, w_ffn_up, w_ffn_down,
                              w_router, w_moe_gate, w_moe_up, w_moe_down)
        xs = layer_norm(DEEPNORM_ALPHA * xs + s_g2 * ffn_s, ln_gain[l, 1], ln_bias[l, 1])

    new_cache_a_k = jnp.stack(ctx_a_k, axis=1)
    new_cache_a_v = jnp.stack(ctx_a_v, axis=1)
    new_cache_b_k = jnp.stack(ctx_b_k, axis=1)
    new_cache_b_v = jnp.stack(ctx_b_v, axis=1)
    return (xp, xs, new_cache_a_k, new_cache_a_v, new_cache_b_k, new_cache_b_v)
```

```python
import functools
import math

import jax
import jax.numpy as jnp
from jax import lax
from jax.experimental import pallas as pl
from jax.experimental.pallas import tpu as pltpu

D_MODEL = 1024
BATCH = 16
SEQ = 256
DEPTH = 2
DEC_BATCH = 2
DEC_SEQ = 2048
PAST_LEN = 256
GRID_W = 64
HEAD_DIM = 64
DIFF_HEADS = 4
GQA_HEADS = 8
GQA_KV_HEADS = 2
DIFF_WIDTH = DIFF_HEADS * 2 * HEAD_DIM
GQA_WIDTH = GQA_HEADS * HEAD_DIM
KV_WIDTH = GQA_KV_HEADS * HEAD_DIM
IN_COLS = 3 * DIFF_WIDTH + GQA_WIDTH + 2 * KV_WIDTH
D_FF = 2816
N_EXPERTS = 8
ROPE_THETA = 10000.0
EPS = 1e-6
DEEPNORM_ALPHA = (2 * DEPTH) ** 0.25

N_PROMPT = BATCH * SEQ
N_SAMPLE = DEC_BATCH * DEC_SEQ
N_TOK = N_PROMPT + N_SAMPLE
N_MOD = 6 * D_MODEL
MOD_ROWS = 8

LANES = 128
VMEM_LIMIT = 56 * 1024 * 1024

ADA_TN = 1536
PROJ_TM = 512
ATT_TQ = 256
FFN_TM = 1024
FFN_TF = 256

KV_DIFF_STRIDE = 3 * LANES
KV_GQA_OFF = DIFF_HEADS * KV_DIFF_STRIDE
KV_PREP_COLS = KV_GQA_OFF + 4 * LANES
KV_NEW_COLS = 2 * DIFF_WIDTH + 4 * KV_WIDTH

F32 = jnp.float32
BF16 = jnp.bfloat16


def _params(n_axes):
    return pltpu.CompilerParams(dimension_semantics=("arbitrary",) * n_axes,
                                vmem_limit_bytes=VMEM_LIMIT)


def _layer_norm(y, g, b):
    mu = jnp.mean(y, axis=-1, keepdims=True)
    yc = y - mu
    var = jnp.mean(yc * yc, axis=-1, keepdims=True)
    return yc * lax.rsqrt(var + EPS) * g + b


def _silu(x):
    return x * jax.nn.sigmoid(x)


def _half_masks():
    lane = lax.broadcasted_iota(jnp.int32, (1, LANES), 1)
    lo = (lane < HEAD_DIM).astype(F32)
    return lo, 1.0 - lo


def _ada_kernel(cond_ref, w_ref, b_ref, o_ref):
    s = _silu(cond_ref[...])
    o_ref[...] = jnp.dot(s, w_ref[...], precision=lax.Precision.HIGHEST,
                         preferred_element_type=F32) + b_ref[...]


def _ada_call(cond, w_ada, b_ada):
    return pl.pallas_call(
        _ada_kernel,
        grid=(DEPTH, N_MOD // ADA_TN),
        in_specs=[
            pl.BlockSpec((MOD_ROWS, D_MODEL), lambda l, n: (0, 0)),
            pl.BlockSpec((None, D_MODEL, ADA_TN), lambda l, n: (l, 0, n)),
            pl.BlockSpec((None, 1, ADA_TN), lambda l, n: (l, 0, n)),
        ],
        out_specs=pl.BlockSpec((None, MOD_ROWS, ADA_TN), lambda l, n: (l, 0, n)),
        out_shape=jax.ShapeDtypeStruct((DEPTH, MOD_ROWS, N_MOD), F32),
        compiler_params=_params(2),
        name="ada_modulation",
    )(cond, w_ada, b_ada.reshape(DEPTH, 1, N_MOD))


def _head_sumsq(x):
    r = lax.broadcasted_iota(jnp.int32, (LANES, LANES), 0) // HEAD_DIM
    c = lax.broadcasted_iota(jnp.int32, (LANES, LANES), 1) // HEAD_DIM
    ones = (r == c).astype(BF16)
    sq = x * x
    hi = sq.astype(BF16)
    lo = (sq - hi.astype(F32)).astype(BF16)
    return (jnp.dot(hi, ones, preferred_element_type=F32)
            + jnp.dot(lo, ones, preferred_element_type=F32))


def _inproj_kernel(*refs, rope, caches):
    x_ref, mod_ref, w_ref, gq_ref, gk_ref = refs[:5]
    pos = 5
    if rope:
        cos_ref, sa_ref, sb_ref = refs[pos:pos + 3]
        pos += 3
    q_out, kv_out = refs[pos:pos + 2]
    pos += 2
    if caches:
        ak_out, av_out, bk_out, bv_out = refs[pos:pos + 4]
        pos += 4
    wbf = refs[pos]

    @pl.when(pl.program_id(0) == 0)
    def _():
        wbf[...] = w_ref[...].astype(BF16)

    shift = mod_ref[:, 0:D_MODEL]
    scale = mod_ref[:, D_MODEL:2 * D_MODEL]
    h = (x_ref[...] * (1.0 + scale) + shift).astype(BF16)
    proj = jnp.dot(h, wbf[...], preferred_element_type=F32)

    def group(base, g):
        return proj[:, base + g * LANES: base + (g + 1) * LANES]

    def rot(v):
        if not rope:
            return v
        return (v * cos_ref[...] + pltpu.roll(v, LANES - HEAD_DIM // 4, axis=1) * sa_ref[...]
                + pltpu.roll(v, HEAD_DIM // 4, axis=1) * sb_ref[...])

    def normed(v, gain):
        return v * lax.rsqrt(_head_sumsq(v) * (1.0 / HEAD_DIM) + EPS) * gain

    qk_scale = HEAD_DIM ** -0.5
    off_ak, off_av, off_bq = DIFF_WIDTH, 2 * DIFF_WIDTH, 3 * DIFF_WIDTH
    off_bk, off_bv = off_bq + GQA_WIDTH, off_bq + GQA_WIDTH + KV_WIDTH

    for g in range(DIFF_WIDTH // LANES):
        q_out[:, g * LANES:(g + 1) * LANES] = (rot(group(0, g)) * qk_scale).astype(BF16)
        a_k = group(off_ak, g)
        a_v = group(off_av, g)
        if caches:
            ak_out[:, g * LANES:(g + 1) * LANES] = a_k
            av_out[:, g * LANES:(g + 1) * LANES] = a_v
        kv_out[:, g * LANES:(g + 1) * LANES] = rot(a_k).astype(BF16)
        kv_out[:, DIFF_WIDTH + g * LANES: DIFF_WIDTH + (g + 1) * LANES] = a_v.astype(BF16)
    for g in range(GQA_WIDTH // LANES):
        b_q = rot(normed(group(off_bq, g), gq_ref[...]))
        q_out[:, DIFF_WIDTH + g * LANES: DIFF_WIDTH + (g + 1) * LANES] = (b_q * qk_scale).astype(BF16)
    b_k = normed(group(off_bk, 0), gk_ref[...])
    b_v = group(off_bv, 0)
    if caches:
        bk_out[...] = b_k
        bv_out[...] = b_v
    b_k = rot(b_k)
    base = 2 * DIFF_WIDTH
    kv_out[:, base:base + LANES] = b_k.astype(BF16)
    kv_out[:, base + LANES:base + 2 * LANES] = b_v.astype(BF16)
    kv_out[:, base + 2 * LANES:base + 3 * LANES] = pltpu.roll(b_k, HEAD_DIM, axis=1).astype(BF16)
    kv_out[:, base + 3 * LANES:base + 4 * LANES] = pltpu.roll(b_v, HEAD_DIM, axis=1).astype(BF16)


def _inproj_call(x, row_off, n_rows, mod3, mod_row_fn, w_in, gq, gk, layer, rope_tabs, caches):
    tm = PROJ_TM
    n_tiles = n_rows // tm
    blk_off = row_off // tm
    rope = rope_tabs is not None
    in_specs = [
        pl.BlockSpec((tm, D_MODEL), lambda i: (i + blk_off, 0)),
        pl.BlockSpec((None, 1, N_MOD), lambda i: (mod_row_fn(i), 0, 0)),
        pl.BlockSpec((None, D_MODEL, IN_COLS), lambda i: (layer, 0, 0)),
        pl.BlockSpec((1, LANES), lambda i: (0, 0)),
        pl.BlockSpec((1, LANES), lambda i: (0, 0)),
    ]
    args = [x, mod3, w_in, gq, gk]
    if rope:
        pos_tiles = DEC_SEQ // tm
        for t in rope_tabs:
            in_specs.append(pl.BlockSpec((tm, LANES), lambda i: (i % pos_tiles, 0)))
            args.append(t)
    out_shape = [jax.ShapeDtypeStruct((n_rows, 2 * DIFF_WIDTH), BF16),
                 jax.ShapeDtypeStruct((n_rows, KV_NEW_COLS), BF16)]
    out_specs = [pl.BlockSpec((tm, 2 * DIFF_WIDTH), lambda i: (i, 0)),
                 pl.BlockSpec((tm, KV_NEW_COLS), lambda i: (i, 0))]
    if caches:
        for width in (DIFF_WIDTH, DIFF_WIDTH, KV_WIDTH, KV_WIDTH):
            out_shape.append(jax.ShapeDtypeStruct((n_rows, width), F32))
            out_specs.append(pl.BlockSpec((tm, width), lambda i: (i, 0)))
    return pl.pallas_call(
        functools.partial(_inproj_kernel, rope=rope, caches=caches),
        grid=(n_tiles,),
        in_specs=in_specs,
        out_specs=out_specs,
        out_shape=out_shape,
        scratch_shapes=[pltpu.VMEM((D_MODEL, IN_COLS), BF16)],
        compiler_params=_params(1),
        name="in_projection_rope" if rope else "in_projection_ctx",
    )(*args)


def _attn_kernel(*refs, n_new, n_cache, lam_init):
    q_ref, kv_ref = refs[:2]
    pos = 2
    if n_cache:
        cak_ref, cav_ref, cbk_ref, cbv_ref = refs[pos:pos + 4]
        pos += 4
    x_ref, mod_ref, wout_ref, lam_ref, subln_ref, lng_ref, lnb_ref, o_ref = refs[pos:pos + 8]
    kvs, wbf, oscr = refs[pos + 8:pos + 11]

    b = pl.program_id(0)
    qi = pl.program_id(1)
    lo_f, hi_f = _half_masks()
    lo_b, hi_b = lo_f.astype(BF16), hi_f.astype(BF16)

    @pl.when((b == 0) & (qi == 0))
    def _():
        wbf[...] = wout_ref[...].astype(BF16)

    @pl.when(qi == 0)
    def _():
        for h in range(DIFF_HEADS):
            k = kv_ref[:, h * LANES:(h + 1) * LANES]
            c0 = h * KV_DIFF_STRIDE
            kvs[0:n_new, c0:c0 + LANES] = k * lo_b
            kvs[0:n_new, c0 + LANES:c0 + 2 * LANES] = k * hi_b
            kvs[0:n_new, c0 + 2 * LANES:c0 + 3 * LANES] = kv_ref[:, DIFF_WIDTH + h * LANES:
                                                                 DIFF_WIDTH + (h + 1) * LANES]
            if n_cache:
                kc = cak_ref[:, h * LANES:(h + 1) * LANES]
                kvs[n_new:n_new + n_cache, c0:c0 + LANES] = (kc * lo_f).astype(BF16)
                kvs[n_new:n_new + n_cache, c0 + LANES:c0 + 2 * LANES] = (kc * hi_f).astype(BF16)
                kvs[n_new:n_new + n_cache, c0 + 2 * LANES:c0 + 3 * LANES] = (
                    cav_ref[:, h * LANES:(h + 1) * LANES].astype(BF16))
        kvs[0:n_new, KV_GQA_OFF:KV_GQA_OFF + 4 * LANES] = kv_ref[:, 2 * DIFF_WIDTH:2 * DIFF_WIDTH + 4 * LANES]
        if n_cache:
            ck = cbk_ref[...]
            cv = cbv_ref[...]
            rows = slice(n_new, n_new + n_cache)
            kvs[rows, KV_GQA_OFF:KV_GQA_OFF + LANES] = ck.astype(BF16)
            kvs[rows, KV_GQA_OFF + LANES:KV_GQA_OFF + 2 * LANES] = cv.astype(BF16)
            kvs[rows, KV_GQA_OFF + 2 * LANES:KV_GQA_OFF + 3 * LANES] = pltpu.roll(ck, HEAD_DIM, axis=1).astype(BF16)
            kvs[rows, KV_GQA_OFF + 3 * LANES:KV_GQA_OFF + 4 * LANES] = pltpu.roll(cv, HEAD_DIM, axis=1).astype(BF16)

    lp = lam_ref[...]
    lam = (jnp.exp(jnp.sum(lp[0:1] * lp[1:2], axis=-1, keepdims=True))
           - jnp.exp(jnp.sum(lp[2:3] * lp[3:4], axis=-1, keepdims=True)) + lam_init)

    def scores(q, k):
        return lax.dot_general(q, k, (((1,), (1,)), ((), ())), preferred_element_type=F32)

    def softmax_parts(s):
        e = jnp.exp(s - jnp.max(s, axis=-1, keepdims=True))
        return e, 1.0 / jnp.sum(e, axis=-1, keepdims=True)

    for h in range(DIFF_HEADS):
        c0 = h * KV_DIFF_STRIDE
        q = q_ref[:, h * LANES:(h + 1) * LANES]
        e1, r1 = softmax_parts(scores(q, kvs[:, c0:c0 + LANES]))
        e2, r2 = softmax_parts(scores(q, kvs[:, c0 + LANES:c0 + 2 * LANES]))
        a = (e1 * r1 - e2 * (lam * r2)).astype(BF16)
        o = jnp.dot(a, kvs[:, c0 + 2 * LANES:c0 + 3 * LANES], preferred_element_type=F32)
        o = o * lax.rsqrt(jnp.mean(o * o, axis=-1, keepdims=True) + EPS) * subln_ref[...]
        oscr[:, h * LANES:(h + 1) * LANES] = (o * (1.0 - lam_init)).astype(BF16)

    for pair in range(GQA_HEADS // 2):
        q_pair = q_ref[:, DIFF_WIDTH + pair * LANES: DIFF_WIDTH + (pair + 1) * LANES]
        halves = []
        for c in range(2):
            kv_head = (2 * pair + c) // (GQA_HEADS // GQA_KV_HEADS)
            koff = KV_GQA_OFF if kv_head == c else KV_GQA_OFF + 2 * LANES
            e, r = softmax_parts(scores(q_pair * (lo_b if c == 0 else hi_b), kvs[:, koff:koff + LANES]))
            o = jnp.dot(e.astype(BF16), kvs[:, koff + LANES:koff + 2 * LANES], preferred_element_type=F32)
            halves.append(o * r)
        o_pair = halves[0] * lo_f + halves[1] * hi_f
        oscr[:, DIFF_WIDTH + pair * LANES: DIFF_WIDTH + (pair + 1) * LANES] = o_pair.astype(BF16)

    mix = jnp.dot(oscr[...], wbf[...], preferred_element_type=F32)
    gate = mod_ref[:, 2 * D_MODEL:3 * D_MODEL]
    y = DEEPNORM_ALPHA * x_ref[...] + gate * mix
    o_ref[...] = _layer_norm(y, lng_ref[...], lnb_ref[...])


def _attn_call(q, kv, cache, x, x_row_off, mod3, mod_row_fn, w_out, diff_lambda, subln, ln_g, ln_b,
               layer, n_batch, n_new):
    tq = ATT_TQ
    nq = n_new // tq
    x_blk_off = x_row_off // tq
    n_cache = PAST_LEN if cache is not None else 0
    lam_init = 0.8 - 0.6 * math.exp(-0.3 * layer)
    in_specs = [
        pl.BlockSpec((tq, 2 * DIFF_WIDTH), lambda b, i: (b * nq + i, 0)),
        pl.BlockSpec((n_new, KV_NEW_COLS), lambda b, i: (b, 0)),
    ]
    args = [q, kv]
    if cache is not None:
        for arr in cache:
            width = arr.shape[-1]
            in_specs.append(pl.BlockSpec((None, None, PAST_LEN, width), lambda b, i: (b, layer, 0, 0)))
            args.append(arr)
    in_specs += [
        pl.BlockSpec((tq, D_MODEL), lambda b, i: (b * nq + i + x_blk_off, 0)),
        pl.BlockSpec((None, 1, N_MOD), lambda b, i: (mod_row_fn(b), 0, 0)),
        pl.BlockSpec((None, D_MODEL, D_MODEL), lambda b, i: (layer, 0, 0)),
        pl.BlockSpec((None, 4, HEAD_DIM), lambda b, i: (layer, 0, 0)),
        pl.BlockSpec((1, LANES), lambda b, i: (0, 0)),
        pl.BlockSpec((1, D_MODEL), lambda b, i: (0, 0)),
        pl.BlockSpec((1, D_MODEL), lambda b, i: (0, 0)),
    ]
    args += [x, mod3, w_out, diff_lambda, subln, ln_g, ln_b]
    return pl.pallas_call(
        functools.partial(_attn_kernel, n_new=n_new, n_cache=n_cache, lam_init=lam_init),
        grid=(n_batch, nq),
        in_specs=in_specs,
        out_specs=pl.BlockSpec((tq, D_MODEL), lambda b, i: (b * nq + i, 0)),
        out_shape=jax.ShapeDtypeStruct((n_batch * n_new, D_MODEL), F32),
        scratch_shapes=[pltpu.VMEM((n_new + n_cache, KV_PREP_COLS), BF16),
                        pltpu.VMEM((D_MODEL, D_MODEL), BF16),
                        pltpu.VMEM((tq, D_MODEL), BF16)],
        compiler_params=_params(2),
        name="token_mixer_latent" if cache is not None else "token_mixer_ctx",
    )(*args)


def _router_kernel(xp_ref, xs_ref, mod_ref, wr_ref, gates_ref, *, n_prompt_tiles):
    i = pl.program_id(0)

    def run(x_ref):
        shift = mod_ref[:, 3 * D_MODEL:4 * D_MODEL]
        scale = mod_ref[:, 4 * D_MODEL:5 * D_MODEL]
        h = x_ref[...] * (1.0 + scale) + shift
        logits = jnp.dot(h, wr_ref[...], precision=lax.Precision.HIGHEST, preferred_element_type=F32)
        lane = lax.broadcasted_iota(jnp.int32, logits.shape, 1).astype(F32)
        neg = jnp.float32(-jnp.inf)
        logits = jnp.where(lane < N_EXPERTS, logits, neg)
        m1 = jnp.max(logits, axis=-1, keepdims=True)
        i1 = jnp.min(jnp.where(logits == m1, lane, float(LANES)), axis=-1, keepdims=True)
        rest = jnp.where(lane == i1, neg, logits)
        m2 = jnp.max(rest, axis=-1, keepdims=True)
        i2 = jnp.min(jnp.where(rest == m2, lane, float(LANES)), axis=-1, keepdims=True)
        e2 = jnp.exp(m2 - m1)
        p1 = 1.0 / (1.0 + e2)
        p2 = e2 / (1.0 + e2)
        gates_ref[...] = jnp.where(lane == i1, p1, 0.0) + jnp.where(lane == i2, p2, 0.0)

    @pl.when(i < n_prompt_tiles)
    def _():
        run(xp_ref)

    @pl.when(i >= n_prompt_tiles)
    def _():
        run(xs_ref)


def _router_call(xp, xs, mod3, mod_row_fn, w_router_pad):
    tm = FFN_TM
    npt = N_PROMPT // tm
    return pl.pallas_call(
        functools.partial(_router_kernel, n_prompt_tiles=npt),
        grid=(N_TOK // tm,),
        in_specs=[
            pl.BlockSpec((tm, D_MODEL), lambda i: (jnp.minimum(i, npt - 1), 0)),
            pl.BlockSpec((tm, D_MODEL), lambda i: (jnp.maximum(i - npt, 0), 0)),
            pl.BlockSpec((None, 1, N_MOD), lambda i: (mod_row_fn(i), 0, 0)),
            pl.BlockSpec((D_MODEL, LANES), lambda i: (0, 0)),
        ],
        out_specs=pl.BlockSpec((tm, LANES), lambda i: (i, 0)),
        out_shape=jax.ShapeDtypeStruct((N_TOK, LANES), F32),
        compiler_params=_params(1),
        name="router",
    )(xp, xs, mod3, w_router_pad)


def _ffn_kernel(*refs, n_prompt_tiles, n_experts, gated):
    xp_ref, xs_ref, mod_ref = refs[:3]
    pos = 3
    if gated:
        gates_ref = refs[pos]
        pos += 1
    wg_ref, wu_ref, wd_ref, lng_ref, lnb_ref, o_ref, hscr, acc = refs[pos:pos + 8]
    i = pl.program_id(0)
    e = pl.program_id(1)
    j = pl.program_id(2)
    first = (e == 0) & (j == 0)
    last = (e == n_experts - 1) & (j == pl.num_programs(2) - 1)

    def modulate(x_ref):
        shift = mod_ref[:, 3 * D_MODEL:4 * D_MODEL]
        scale = mod_ref[:, 4 * D_MODEL:5 * D_MODEL]
        hscr[...] = (x_ref[...] * (1.0 + scale) + shift).astype(BF16)
        acc[...] = jnp.zeros_like(acc)

    @pl.when(first & (i < n_prompt_tiles))
    def _():
        modulate(xp_ref)

    @pl.when(first & (i >= n_prompt_tiles))
    def _():
        modulate(xs_ref)

    h = hscr[...]
    g = jnp.dot(h, wg_ref[...].astype(BF16), preferred_element_type=F32)
    u = jnp.dot(h, wu_ref[...].astype(BF16), preferred_element_type=F32)
    a = _silu(g) * u
    if gated:
        lane = lax.broadcasted_iota(jnp.int32, gates_ref.shape, 1)
        a = a * jnp.sum(jnp.where(lane == e, gates_ref[...], 0.0), axis=-1, keepdims=True)
    acc[...] += jnp.dot(a.astype(BF16), wd_ref[...].astype(BF16), preferred_element_type=F32)

    def finish(x_ref):
        gate = mod_ref[:, 5 * D_MODEL:6 * D_MODEL]
        y = DEEPNORM_ALPHA * x_ref[...] + gate * acc[...]
        o_ref[...] = _layer_norm(y, lng_ref[...], lnb_ref[...])

    @pl.when(last & (i < n_prompt_tiles))
    def _():
        finish(xp_ref)

    @pl.when(last & (i >= n_prompt_tiles))
    def _():
        finish(xs_ref)


def _ffn_call(xp, xs, mod3, mod_row_fn, gates, wg, wu, wd, ln_g, ln_b):
    tm, tf = FFN_TM, FFN_TF
    npt = N_PROMPT // tm
    n_experts = wg.shape[0]
    gated = gates is not None
    in_specs = [
        pl.BlockSpec((tm, D_MODEL), lambda i, e, j: (jnp.minimum(i, npt - 1), 0)),
        pl.BlockSpec((tm, D_MODEL), lambda i, e, j: (jnp.maximum(i - npt, 0), 0)),
        pl.BlockSpec((None, 1, N_MOD), lambda i, e, j: (mod_row_fn(i), 0, 0)),
    ]
    args = [xp, xs, mod3]
    if gated:
        in_specs.append(pl.BlockSpec((tm, LANES), lambda i, e, j: (i, 0)))
        args.append(gates)
    in_specs += [
        pl.BlockSpec((None, D_MODEL, tf), lambda i, e, j: (e, 0, j)),
        pl.BlockSpec((None, D_MODEL, tf), lambda i, e, j: (e, 0, j)),
        pl.BlockSpec((None, tf, D_MODEL), lambda i, e, j: (e, j, 0)),
        pl.BlockSpec((1, D_MODEL), lambda i, e, j: (0, 0)),
        pl.BlockSpec((1, D_MODEL), lambda i, e, j: (0, 0)),
    ]
    args += [wg, wu, wd, ln_g, ln_b]
    return pl.pallas_call(
        functools.partial(_ffn_kernel, n_prompt_tiles=npt, n_experts=n_experts, gated=gated),
        grid=(N_TOK // tm, n_experts, D_FF // tf),
        in_specs=in_specs,
        out_specs=pl.BlockSpec((tm, D_MODEL), lambda i, e, j: (i, 0)),
        out_shape=jax.ShapeDtypeStruct((N_TOK, D_MODEL), F32),
        scratch_shapes=[pltpu.VMEM((tm, D_MODEL), BF16), pltpu.VMEM((tm, D_MODEL), F32)],
        compiler_params=_params(3),
        name="channel_mixer_moe" if gated else "channel_mixer_dense",
    )(*args)


def _rope_tables():
    rows = DEC_SEQ // GRID_W
    row = jnp.repeat(jnp.arange(rows, dtype=F32), GRID_W)
    col = jnp.tile(jnp.arange(GRID_W, dtype=F32), rows)
    n_freq = HEAD_DIM // 4
    inv = ROPE_THETA ** (-jnp.arange(n_freq, dtype=F32) / n_freq)
    ar = row[:, None] * inv
    ac = col[:, None] * inv
    ang = jnp.concatenate([ar, ar, ac, ac], axis=-1)
    cos = jnp.tile(jnp.cos(ang), (1, LANES // HEAD_DIM))
    sin = jnp.tile(jnp.sin(ang), (1, LANES // HEAD_DIM))
    first_half = (jnp.arange(LANES) % (2 * n_freq)) < n_freq
    sin_next = jnp.where(first_half, -sin, 0.0)
    sin_prev = jnp.where(first_half, 0.0, sin)
    return cos, sin_next, sin_prev


def kernel(x_prompt, x_sample, cache_a_k, cache_a_v, cache_b_k, cache_b_v, c, c_ctx, w_ada, b_ada, w_in,
           w_out, diff_lambda, diff_subln, qk_norm_gain, ln_gain, ln_bias, w_ffn_gate, w_ffn_up,
           w_ffn_down, w_router, w_moe_gate, w_moe_up, w_moe_down):
    cond = jnp.zeros((MOD_ROWS, D_MODEL), F32).at[0].set(c_ctx).at[1:1 + DEC_BATCH].set(c)
    mod3 = _ada_call(cond, w_ada, b_ada).reshape(DEPTH * MOD_ROWS, 1, N_MOD)
    rope_tabs = _rope_tables()
    cache = (cache_a_k.reshape(DEC_BATCH, DEPTH, PAST_LEN, DIFF_WIDTH),
             cache_a_v.reshape(DEC_BATCH, DEPTH, PAST_LEN, DIFF_WIDTH),
             cache_b_k.reshape(DEC_BATCH, DEPTH, PAST_LEN, KV_WIDTH),
             cache_b_v.reshape(DEC_BATCH, DEPTH, PAST_LEN, KV_WIDTH))

    xp = x_prompt.reshape(N_PROMPT, D_MODEL)
    xs = x_sample.reshape(N_SAMPLE, D_MODEL)
    xs_off = 0
    new_caches = []
    for l in range(DEPTH):
        base = l * MOD_ROWS
        gq = jnp.tile(qk_norm_gain[l, 0], LANES // HEAD_DIM).reshape(1, LANES)
        gk = jnp.tile(qk_norm_gain[l, 1], LANES // HEAD_DIM).reshape(1, LANES)
        subln = diff_subln[l].reshape(1, LANES)
        ln1_g, ln1_b = ln_gain[l, 0].reshape(1, D_MODEL), ln_bias[l, 0].reshape(1, D_MODEL)
        ln2_g, ln2_b = ln_gain[l, 1].reshape(1, D_MODEL), ln_bias[l, 1].reshape(1, D_MODEL)

        q_p, kv_p, ak, av, bk, bv = _inproj_call(
            xp, 0, N_PROMPT, mod3, lambda i: base, w_in, gq, gk, l, None, True)
        new_caches.append((ak, av, bk, bv))
        x1_p = _attn_call(q_p, kv_p, None, xp, 0, mod3, lambda b: base, w_out, diff_lambda, subln,
                          ln1_g, ln1_b, l, BATCH, SEQ)
        s_tiles = DEC_SEQ // PROJ_TM
        q_s, kv_s = _inproj_call(
            xs, xs_off, N_SAMPLE, mod3, lambda i: base + 1 + i // s_tiles, w_in, gq, gk, l, rope_tabs, False)
        x1_s = _attn_call(q_s, kv_s, cache, xs, xs_off, mod3, lambda b: base + 1 + b, w_out, diff_lambda,
                          subln, ln1_g, ln1_b, l, DEC_BATCH, DEC_SEQ)

        npt = N_PROMPT // FFN_TM
        s_ffn_tiles = DEC_SEQ // FFN_TM
        mod_row = lambda i: base + jnp.where(i < npt, 0, 1 + (i - npt) // s_ffn_tiles)
        if l % 2 == 0:
            k = l // 2
            x2 = _ffn_call(x1_p, x1_s, mod3, mod_row, None, w_ffn_gate[k:k + 1], w_ffn_up[k:k + 1],
                           w_ffn_down[k:k + 1], ln2_g, ln2_b)
        else:
            k = l // 2
            wr = jnp.zeros((D_MODEL, LANES), F32).at[:, :N_EXPERTS].set(w_router[k])
            gates = _router_call(x1_p, x1_s, mod3, mod_row, wr)
            x2 = _ffn_call(x1_p, x1_s, mod3, mod_row, gates, w_moe_gate[k], w_moe_up[k], w_moe_down[k],
                           ln2_g, ln2_b)
        xp, xs, xs_off = x2, x2, N_PROMPT

    y_prompt = x2[:N_PROMPT].reshape(BATCH, SEQ, D_MODEL)
    y_sample = x2[N_PROMPT:].reshape(DEC_BATCH, DEC_SEQ, D_MODEL)
    stack = lambda idx, shape: jnp.stack([nc[idx] for nc in new_caches], axis=0).reshape(
        (DEPTH, BATCH, SEQ) + shape).transpose((1, 0, 2) + tuple(range(3, 3 + len(shape))))
    return (y_prompt, y_sample,
            stack(0, (DIFF_HEADS, 2, HEAD_DIM)), stack(1, (DIFF_HEADS, 2 * HEAD_DIM)),
            stack(2, (GQA_KV_HEADS, HEAD_DIM)), stack(3, (GQA_KV_HEADS, HEAD_DIM)))
```

```python
import functools
import math

import jax
import jax.numpy as jnp
from jax import lax
from jax.experimental import pallas as pl
from jax.experimental.pallas import tpu as pltpu

D_MODEL = 1024
BATCH = 16
SEQ = 256
DEPTH = 2
DEC_BATCH = 2
DEC_SEQ = 2048
PAST_LEN = 256
GRID_W = 64
HEAD_DIM = 64
DIFF_HEADS = 4
GQA_HEADS = 8
GQA_KV_HEADS = 2
DIFF_WIDTH = DIFF_HEADS * 2 * HEAD_DIM
GQA_WIDTH = GQA_HEADS * HEAD_DIM
KV_WIDTH = GQA_KV_HEADS * HEAD_DIM
IN_COLS = 3 * DIFF_WIDTH + GQA_WIDTH + 2 * KV_WIDTH
D_FF = 2816
N_EXPERTS = 8
ROPE_THETA = 10000.0
EPS = 1e-6
DEEPNORM_ALPHA = (2 * DEPTH) ** 0.25

N_PROMPT = BATCH * SEQ
N_SAMPLE = DEC_BATCH * DEC_SEQ
N_TOK = N_PROMPT + N_SAMPLE
N_MOD = 6 * D_MODEL
MOD_ROWS = 8

LANES = 128
VMEM_LIMIT = 56 * 1024 * 1024

ADA_TN = 1536
PROJ_TM = 512
ATT_TQ = 256
FFN_TM = 1024
FFN_TF = 256
MOE_TM = 1024
MOE_CHUNK = 256
MOE_WINDOW = 2 * MOE_CHUNK
MOE_ROWS = 2 * N_TOK
MOE_NCHUNK = N_TOK // MOE_CHUNK
MOE_MAX_VISITS = MOE_ROWS // MOE_TM + N_EXPERTS - 1

KV_DIFF_STRIDE = 3 * LANES
KV_GQA_OFF = DIFF_HEADS * KV_DIFF_STRIDE
KV_PREP_COLS = KV_GQA_OFF + 4 * LANES
KV_NEW_COLS = 2 * DIFF_WIDTH + 4 * KV_WIDTH

F32 = jnp.float32
BF16 = jnp.bfloat16


def _params(n_axes):
    return pltpu.CompilerParams(dimension_semantics=("arbitrary",) * n_axes,
                                vmem_limit_bytes=VMEM_LIMIT)


def _layer_norm(y, g, b):
    mu = jnp.mean(y, axis=-1, keepdims=True)
    yc = y - mu
    var = jnp.mean(yc * yc, axis=-1, keepdims=True)
    return yc * lax.rsqrt(var + EPS) * g + b


def _silu(x):
    return x * jax.nn.sigmoid(x)


def _half_masks():
    lane = lax.broadcasted_iota(jnp.int32, (1, LANES), 1)
    lo = (lane < HEAD_DIM).astype(F32)
    return lo, 1.0 - lo


def _ada_kernel(cond_ref, w_ref, b_ref, o_ref):
    s = _silu(cond_ref[...])
    o_ref[...] = jnp.dot(s, w_ref[...], precision=lax.Precision.HIGHEST,
                         preferred_element_type=F32) + b_ref[...]


def _ada_call(cond, w_ada, b_ada):
    return pl.pallas_call(
        _ada_kernel,
        grid=(DEPTH, N_MOD // ADA_TN),
        in_specs=[
            pl.BlockSpec((MOD_ROWS, D_MODEL), lambda l, n: (0, 0)),
            pl.BlockSpec((None, D_MODEL, ADA_TN), lambda l, n: (l, 0, n)),
            pl.BlockSpec((None, 1, ADA_TN), lambda l, n: (l, 0, n)),
        ],
        out_specs=pl.BlockSpec((None, MOD_ROWS, ADA_TN), lambda l, n: (l, 0, n)),
        out_shape=jax.ShapeDtypeStruct((DEPTH, MOD_ROWS, N_MOD), F32),
        compiler_params=_params(2),
        name="ada_modulation",
    )(cond, w_ada, b_ada.reshape(DEPTH, 1, N_MOD))


def _head_sumsq(x):
    r = lax.broadcasted_iota(jnp.int32, (LANES, LANES), 0) // HEAD_DIM
    c = lax.broadcasted_iota(jnp.int32, (LANES, LANES), 1) // HEAD_DIM
    ones = (r == c).astype(BF16)
    sq = x * x
    hi = sq.astype(BF16)
    lo = (sq - hi.astype(F32)).astype(BF16)
    return (jnp.dot(hi, ones, preferred_element_type=F32)
            + jnp.dot(lo, ones, preferred_element_type=F32))


def _inproj_kernel(*refs, rope, caches):
    x_ref, mod_ref, w_ref, gq_ref, gk_ref = refs[:5]
    pos = 5
    if rope:
        cos_ref, sa_ref, sb_ref = refs[pos:pos + 3]
        pos += 3
    q_out, kv_out = refs[pos:pos + 2]
    pos += 2
    if caches:
        ak_out, av_out, bk_out, bv_out = refs[pos:pos + 4]
        pos += 4
    wbf = refs[pos]

    @pl.when(pl.program_id(0) == 0)
    def _():
        wbf[...] = w_ref[...].astype(BF16)

    shift = mod_ref[:, 0:D_MODEL]
    scale = mod_ref[:, D_MODEL:2 * D_MODEL]
    h = (x_ref[...] * (1.0 + scale) + shift).astype(BF16)
    proj = jnp.dot(h, wbf[...], preferred_element_type=F32)

    def group(base, g):
        return proj[:, base + g * LANES: base + (g + 1) * LANES]

    def rot(v):
        if not rope:
            return v
        return (v * cos_ref[...] + pltpu.roll(v, LANES - HEAD_DIM // 4, axis=1) * sa_ref[...]
                + pltpu.roll(v, HEAD_DIM // 4, axis=1) * sb_ref[...])

    def normed(v, gain):
        return v * lax.rsqrt(_head_sumsq(v) * (1.0 / HEAD_DIM) + EPS) * gain

    qk_scale = HEAD_DIM ** -0.5
    off_ak, off_av, off_bq = DIFF_WIDTH, 2 * DIFF_WIDTH, 3 * DIFF_WIDTH
    off_bk, off_bv = off_bq + GQA_WIDTH, off_bq + GQA_WIDTH + KV_WIDTH

    for g in range(DIFF_WIDTH // LANES):
        q_out[:, g * LANES:(g + 1) * LANES] = (rot(group(0, g)) * qk_scale).astype(BF16)
        a_k = group(off_ak, g)
        a_v = group(off_av, g)
        if caches:
            ak_out[:, g * LANES:(g + 1) * LANES] = a_k
            av_out[:, g * LANES:(g + 1) * LANES] = a_v
        kv_out[:, g * LANES:(g + 1) * LANES] = rot(a_k).astype(BF16)
        kv_out[:, DIFF_WIDTH + g * LANES: DIFF_WIDTH + (g + 1) * LANES] = a_v.astype(BF16)
    for g in range(GQA_WIDTH // LANES):
        b_q = rot(normed(group(off_bq, g), gq_ref[...]))
        q_out[:, DIFF_WIDTH + g * LANES: DIFF_WIDTH + (g + 1) * LANES] = (b_q * qk_scale).astype(BF16)
    b_k = normed(group(off_bk, 0), gk_ref[...])
    b_v = group(off_bv, 0)
    if caches:
        bk_out[...] = b_k
        bv_out[...] = b_v
    b_k = rot(b_k)
    base = 2 * DIFF_WIDTH
    kv_out[:, base:base + LANES] = b_k.astype(BF16)
    kv_out[:, base + LANES:base + 2 * LANES] = b_v.astype(BF16)
    kv_out[:, base + 2 * LANES:base + 3 * LANES] = pltpu.roll(b_k, HEAD_DIM, axis=1).astype(BF16)
    kv_out[:, base + 3 * LANES:base + 4 * LANES] = pltpu.roll(b_v, HEAD_DIM, axis=1).astype(BF16)


def _inproj_call(x, row_off, n_rows, mod3, mod_row_fn, w_in, gq, gk, layer, rope_tabs, caches):
    tm = PROJ_TM
    n_tiles = n_rows // tm
    blk_off = row_off // tm
    rope = rope_tabs is not None
    in_specs = [
        pl.BlockSpec((tm, D_MODEL), lambda i: (i + blk_off, 0)),
        pl.BlockSpec((None, 1, N_MOD), lambda i: (mod_row_fn(i), 0, 0)),
        pl.BlockSpec((None, D_MODEL, IN_COLS), lambda i: (layer, 0, 0)),
        pl.BlockSpec((1, LANES), lambda i: (0, 0)),
        pl.BlockSpec((1, LANES), lambda i: (0, 0)),
    ]
    args = [x, mod3, w_in, gq, gk]
    if rope:
        pos_tiles = DEC_SEQ // tm
        for t in rope_tabs:
            in_specs.append(pl.BlockSpec((tm, LANES), lambda i: (i % pos_tiles, 0)))
            args.append(t)
    out_shape = [jax.ShapeDtypeStruct((n_rows, 2 * DIFF_WIDTH), BF16),
                 jax.ShapeDtypeStruct((n_rows, KV_NEW_COLS), BF16)]
    out_specs = [pl.BlockSpec((tm, 2 * DIFF_WIDTH), lambda i: (i, 0)),
                 pl.BlockSpec((tm, KV_NEW_COLS), lambda i: (i, 0))]
    if caches:
        for width in (DIFF_WIDTH, DIFF_WIDTH, KV_WIDTH, KV_WIDTH):
            out_shape.append(jax.ShapeDtypeStruct((n_rows, width), F32))
            out_specs.append(pl.BlockSpec((tm, width), lambda i: (i, 0)))
    return pl.pallas_call(
        functools.partial(_inproj_kernel, rope=rope, caches=caches),
        grid=(n_tiles,),
        in_specs=in_specs,
        out_specs=out_specs,
        out_shape=out_shape,
        scratch_shapes=[pltpu.VMEM((D_MODEL, IN_COLS), BF16)],
        compiler_params=_params(1),
        name="in_projection_rope" if rope else "in_projection_ctx",
    )(*args)


def _attn_kernel(*refs, n_new, n_cache, lam_init):
    q_ref, kv_ref = refs[:2]
    pos = 2
    if n_cache:
        cak_ref, cav_ref, cbk_ref, cbv_ref = refs[pos:pos + 4]
        pos += 4
    x_ref, mod_ref, wout_ref, lam_ref, subln_ref, lng_ref, lnb_ref, o_ref = refs[pos:pos + 8]
    kvs, wbf, oscr = refs[pos + 8:pos + 11]

    b = pl.program_id(0)
    qi = pl.program_id(1)
    lo_f, hi_f = _half_masks()
    lo_b, hi_b = lo_f.astype(BF16), hi_f.astype(BF16)

    @pl.when((b == 0) & (qi == 0))
    def _():
        wbf[...] = wout_ref[...].astype(BF16)

    @pl.when(qi == 0)
    def _():
        for h in range(DIFF_HEADS):
            k = kv_ref[:, h * LANES:(h + 1) * LANES]
            c0 = h * KV_DIFF_STRIDE
            kvs[0:n_new, c0:c0 + LANES] = k * lo_b
            kvs[0:n_new, c0 + LANES:c0 + 2 * LANES] = k * hi_b
            kvs[0:n_new, c0 + 2 * LANES:c0 + 3 * LANES] = kv_ref[:, DIFF_WIDTH + h * LANES:
                                                                 DIFF_WIDTH + (h + 1) * LANES]
            if n_cache:
                kc = cak_ref[:, h * LANES:(h + 1) * LANES]
                kvs[n_new:n_new + n_cache, c0:c0 + LANES] = (kc * lo_f).astype(BF16)
                kvs[n_new:n_new + n_cache, c0 + LANES:c0 + 2 * LANES] = (kc * hi_f).astype(BF16)
                kvs[n_new:n_new + n_cache, c0 + 2 * LANES:c0 + 3 * LANES] = (
                    cav_ref[:, h * LANES:(h + 1) * LANES].astype(BF16))
        kvs[0:n_new, KV_GQA_OFF:KV_GQA_OFF + 4 * LANES] = kv_ref[:, 2 * DIFF_WIDTH:2 * DIFF_WIDTH + 4 * LANES]
        if n_cache:
            ck = cbk_ref[...]
            cv = cbv_ref[...]
            rows = slice(n_new, n_new + n_cache)
            kvs[rows, KV_GQA_OFF:KV_GQA_OFF + LANES] = ck.astype(BF16)
            kvs[rows, KV_GQA_OFF + LANES:KV_GQA_OFF + 2 * LANES] = cv.astype(BF16)
            kvs[rows, KV_GQA_OFF + 2 * LANES:KV_GQA_OFF + 3 * LANES] = pltpu.roll(ck, HEAD_DIM, axis=1).astype(BF16)
            kvs[rows, KV_GQA_OFF + 3 * LANES:KV_GQA_OFF + 4 * LANES] = pltpu.roll(cv, HEAD_DIM, axis=1).astype(BF16)

    lp = lam_ref[...]
    lam = (jnp.exp(jnp.sum(lp[0:1] * lp[1:2], axis=-1, keepdims=True))
           - jnp.exp(jnp.sum(lp[2:3] * lp[3:4], axis=-1, keepdims=True)) + lam_init)

    def scores(q, k):
        return lax.dot_general(q, k, (((1,), (1,)), ((), ())), preferred_element_type=F32)

    def softmax_parts(s):
        e = jnp.exp(s - jnp.max(s, axis=-1, keepdims=True))
        return e, 1.0 / jnp.sum(e, axis=-1, keepdims=True)

    for h in range(DIFF_HEADS):
        c0 = h * KV_DIFF_STRIDE
        q = q_ref[:, h * LANES:(h + 1) * LANES]
        e1, r1 = softmax_parts(scores(q, kvs[:, c0:c0 + LANES]))
        e2, r2 = softmax_parts(scores(q, kvs[:, c0 + LANES:c0 + 2 * LANES]))
        a = (e1 * r1 - e2 * (lam * r2)).astype(BF16)
        o = jnp.dot(a, kvs[:, c0 + 2 * LANES:c0 + 3 * LANES], preferred_element_type=F32)
        o = o * lax.rsqrt(jnp.mean(o * o, axis=-1, keepdims=True) + EPS) * subln_ref[...]
        oscr[:, h * LANES:(h + 1) * LANES] = (o * (1.0 - lam_init)).astype(BF16)

    for pair in range(GQA_HEADS // 2):
        q_pair = q_ref[:, DIFF_WIDTH + pair * LANES: DIFF_WIDTH + (pair + 1) * LANES]
        halves = []
        for c in range(2):
            kv_head = (2 * pair + c) // (GQA_HEADS // GQA_KV_HEADS)
            koff = KV_GQA_OFF if kv_head == c else KV_GQA_OFF + 2 * LANES
            e, r = softmax_parts(scores(q_pair * (lo_b if c == 0 else hi_b), kvs[:, koff:koff + LANES]))
            o = jnp.dot(e.astype(BF16), kvs[:, koff + LANES:koff + 2 * LANES], preferred_element_type=F32)
            halves.append(o * r)
        o_pair = halves[0] * lo_f + halves[1] * hi_f
        oscr[:, DIFF_WIDTH + pair * LANES: DIFF_WIDTH + (pair + 1) * LANES] = o_pair.astype(BF16)

    mix = jnp.dot(oscr[...], wbf[...], preferred_element_type=F32)
    gate = mod_ref[:, 2 * D_MODEL:3 * D_MODEL]
    y = DEEPNORM_ALPHA * x_ref[...] + gate * mix
    o_ref[...] = _layer_norm(y, lng_ref[...], lnb_ref[...])


def _attn_call(q, kv, cache, x, x_row_off, mod3, mod_row_fn, w_out, diff_lambda, subln, ln_g, ln_b,
               layer, n_batch, n_new):
    tq = ATT_TQ
    nq = n_new // tq
    x_blk_off = x_row_off // tq
    n_cache = PAST_LEN if cache is not None else 0
    lam_init = 0.8 - 0.6 * math.exp(-0.3 * layer)
    in_specs = [
        pl.BlockSpec((tq, 2 * DIFF_WIDTH), lambda b, i: (b * nq + i, 0)),
        pl.BlockSpec((n_new, KV_NEW_COLS), lambda b, i: (b, 0)),
    ]
    args = [q, kv]
    if cache is not None:
        for arr in cache:
            width = arr.shape[-1]
            in_specs.append(pl.BlockSpec((None, None, PAST_LEN, width), lambda b, i: (b, layer, 0, 0)))
            args.append(arr)
    in_specs += [
        pl.BlockSpec((tq, D_MODEL), lambda b, i: (b * nq + i + x_blk_off, 0)),
        pl.BlockSpec((None, 1, N_MOD), lambda b, i: (mod_row_fn(b), 0, 0)),
        pl.BlockSpec((None, D_MODEL, D_MODEL), lambda b, i: (layer, 0, 0)),
        pl.BlockSpec((None, 4, HEAD_DIM), lambda b, i: (layer, 0, 0)),
        pl.BlockSpec((1, LANES), lambda b, i: (0, 0)),
        pl.BlockSpec((1, D_MODEL), lambda b, i: (0, 0)),
        pl.BlockSpec((1, D_MODEL), lambda b, i: (0, 0)),
    ]
    args += [x, mod3, w_out, diff_lambda, subln, ln_g, ln_b]
    return pl.pallas_call(
        functools.partial(_attn_kernel, n_new=n_new, n_cache=n_cache, lam_init=lam_init),
        grid=(n_batch, nq),
        in_specs=in_specs,
        out_specs=pl.BlockSpec((tq, D_MODEL), lambda b, i: (b * nq + i, 0)),
        out_shape=jax.ShapeDtypeStruct((n_batch * n_new, D_MODEL), F32),
        scratch_shapes=[pltpu.VMEM((n_new + n_cache, KV_PREP_COLS), BF16),
                        pltpu.VMEM((D_MODEL, D_MODEL), BF16),
                        pltpu.VMEM((tq, D_MODEL), BF16)],
        compiler_params=_params(2),
        name="token_mixer_latent" if cache is not None else "token_mixer_ctx",
    )(*args)


def _router_kernel(xp_ref, xs_ref, mod_ref, wr_ref, h_ref, gsplit_ref, meta_ref, cum_ref, total_ref,
                   carry, *, n_prompt_tiles):
    i = pl.program_id(0)

    @pl.when(i == 0)
    def _():
        carry[...] = jnp.zeros_like(carry)

    def run(x_ref):
        tm = x_ref.shape[0]
        shift = mod_ref[:, 3 * D_MODEL:4 * D_MODEL]
        scale = mod_ref[:, 4 * D_MODEL:5 * D_MODEL]
        h = x_ref[...] * (1.0 + scale) + shift
        h_ref[...] = h.astype(BF16)
        logits = jnp.dot(h, wr_ref[...], precision=lax.Precision.HIGHEST, preferred_element_type=F32)
        lane = lax.broadcasted_iota(jnp.int32, logits.shape, 1).astype(F32)
        neg = jnp.float32(-jnp.inf)
        logits = jnp.where(lane < N_EXPERTS, logits, neg)
        m1 = jnp.max(logits, axis=-1, keepdims=True)
        i1 = jnp.min(jnp.where(logits == m1, lane, float(LANES)), axis=-1, keepdims=True)
        rest = jnp.where(lane == i1, neg, logits)
        m2 = jnp.max(rest, axis=-1, keepdims=True)
        i2 = jnp.min(jnp.where(rest == m2, lane, float(LANES)), axis=-1, keepdims=True)
        e2 = jnp.exp(m2 - m1)
        p1 = 1.0 / (1.0 + e2)
        p2 = e2 / (1.0 + e2)
        hit1 = lane == i1
        hit2 = lane == i2
        gates = jnp.where(hit1, p1, 0.0) + jnp.where(hit2, p2, 0.0)
        g_hi = gates.astype(BF16)
        gsplit_ref[:, 0:LANES] = g_hi
        gsplit_ref[:, LANES:2 * LANES] = (gates - g_hi.astype(F32)).astype(BF16)

        sel = jnp.where(hit1, 1.0, 0.0) + jnp.where(hit2, 1.0, 0.0)
        r = lax.broadcasted_iota(jnp.int32, (tm, tm), 0)
        c = lax.broadcasted_iota(jnp.int32, (tm, tm), 1)
        before = jnp.where(c < r, 1.0, 0.0).astype(BF16)
        cumx = jnp.dot(before, sel.astype(BF16), preferred_element_type=F32) + carry[0:1, :]
        rank1 = jnp.sum(jnp.where(hit1, cumx, 0.0), axis=-1, keepdims=True)
        rank2 = jnp.sum(jnp.where(hit2, cumx, 0.0), axis=-1, keepdims=True)
        meta = jnp.zeros_like(logits)
        for k, val in enumerate((i1, i2, p1, p2, rank1, rank2)):
            meta = jnp.where(lane == float(k), val, meta)
        meta_ref[...] = meta
        for k in range(tm // MOE_CHUNK):
            cum_ref[k] = jnp.broadcast_to(cumx[k * MOE_CHUNK:k * MOE_CHUNK + 1, :], (8, LANES))
        new_carry = carry[0:1, :] + jnp.sum(sel, axis=0, keepdims=True)
        carry[...] = jnp.broadcast_to(new_carry, carry.shape)
        total_ref[...] = jnp.broadcast_to(new_carry, total_ref.shape)

    @pl.when(i < n_prompt_tiles)
    def _():
        run(xp_ref)

    @pl.when(i >= n_prompt_tiles)
    def _():
        run(xs_ref)


def _router_call(xp, xs, mod3, mod_row_fn, w_router_pad):
    tm = FFN_TM
    npt = N_PROMPT // tm
    chunks = tm // MOE_CHUNK
    return pl.pallas_call(
        functools.partial(_router_kernel, n_prompt_tiles=npt),
        grid=(N_TOK // tm,),
        in_specs=[
            pl.BlockSpec((tm, D_MODEL), lambda i: (jnp.minimum(i, npt - 1), 0)),
            pl.BlockSpec((tm, D_MODEL), lambda i: (jnp.maximum(i - npt, 0), 0)),
            pl.BlockSpec((None, 1, N_MOD), lambda i: (mod_row_fn(i), 0, 0)),
            pl.BlockSpec((D_MODEL, LANES), lambda i: (0, 0)),
        ],
        out_specs=[
            pl.BlockSpec((tm, D_MODEL), lambda i: (i, 0)),
            pl.BlockSpec((tm, 2 * LANES), lambda i: (i, 0)),
            pl.BlockSpec((tm, LANES), lambda i: (i, 0)),
            pl.BlockSpec((chunks, 8, LANES), lambda i: (i, 0, 0)),
            pl.BlockSpec((8, LANES), lambda i: (0, 0)),
        ],
        out_shape=[
            jax.ShapeDtypeStruct((N_TOK, D_MODEL), BF16),
            jax.ShapeDtypeStruct((N_TOK, 2 * LANES), BF16),
            jax.ShapeDtypeStruct((N_TOK, LANES), F32),
            jax.ShapeDtypeStruct((MOE_NCHUNK, 8, LANES), F32),
            jax.ShapeDtypeStruct((8, LANES), F32),
        ],
        scratch_shapes=[pltpu.VMEM((8, LANES), F32)],
        compiler_params=_params(1),
        name="router",
    )(xp, xs, mod3, w_router_pad)


def _ffn_kernel(xp_ref, xs_ref, mod_ref, wg_ref, wu_ref, wd_ref, lng_ref, lnb_ref, o_ref, hscr, acc, *,
                n_prompt_tiles):
    i = pl.program_id(0)
    j = pl.program_id(1)
    first = j == 0
    last = j == pl.num_programs(1) - 1

    def modulate(x_ref):
        shift = mod_ref[:, 3 * D_MODEL:4 * D_MODEL]
        scale = mod_ref[:, 4 * D_MODEL:5 * D_MODEL]
        hscr[...] = (x_ref[...] * (1.0 + scale) + shift).astype(BF16)
        acc[...] = jnp.zeros_like(acc)

    @pl.when(first & (i < n_prompt_tiles))
    def _():
        modulate(xp_ref)

    @pl.when(first & (i >= n_prompt_tiles))
    def _():
        modulate(xs_ref)

    h = hscr[...]
    g = jnp.dot(h, wg_ref[...].astype(BF16), preferred_element_type=F32)
    u = jnp.dot(h, wu_ref[...].astype(BF16), preferred_element_type=F32)
    a = _silu(g) * u
    acc[...] += jnp.dot(a.astype(BF16), wd_ref[...].astype(BF16), preferred_element_type=F32)

    def finish(x_ref):
        gate = mod_ref[:, 5 * D_MODEL:6 * D_MODEL]
        y = DEEPNORM_ALPHA * x_ref[...] + gate * acc[...]
        o_ref[...] = _layer_norm(y, lng_ref[...], lnb_ref[...])

    @pl.when(last & (i < n_prompt_tiles))
    def _():
        finish(xp_ref)

    @pl.when(last & (i >= n_prompt_tiles))
    def _():
        finish(xs_ref)


def _ffn_call(xp, xs, mod3, mod_row_fn, wg, wu, wd, layer_idx, ln_g, ln_b):
    tm, tf = FFN_TM, FFN_TF
    npt = N_PROMPT // tm
    return pl.pallas_call(
        functools.partial(_ffn_kernel, n_prompt_tiles=npt),
        grid=(N_TOK // tm, D_FF // tf),
        in_specs=[
            pl.BlockSpec((tm, D_MODEL), lambda i, j: (jnp.minimum(i, npt - 1), 0)),
            pl.BlockSpec((tm, D_MODEL), lambda i, j: (jnp.maximum(i - npt, 0), 0)),
            pl.BlockSpec((None, 1, N_MOD), lambda i, j: (mod_row_fn(i), 0, 0)),
            pl.BlockSpec((None, D_MODEL, tf), lambda i, j: (layer_idx, 0, j)),
            pl.BlockSpec((None, D_MODEL, tf), lambda i, j: (layer_idx, 0, j)),
            pl.BlockSpec((None, tf, D_MODEL), lambda i, j: (layer_idx, j, 0)),
            pl.BlockSpec((1, D_MODEL), lambda i, j: (0, 0)),
            pl.BlockSpec((1, D_MODEL), lambda i, j: (0, 0)),
        ],
        out_specs=pl.BlockSpec((tm, D_MODEL), lambda i, j: (i, 0)),
        out_shape=jax.ShapeDtypeStruct((N_TOK, D_MODEL), F32),
        scratch_shapes=[pltpu.VMEM((tm, D_MODEL), BF16), pltpu.VMEM((tm, D_MODEL), F32)],
        compiler_params=_params(2),
        name="channel_mixer_dense",
    )(xp, xs, mod3, wg, wu, wd, ln_g, ln_b)


VISIT_FIELDS = 8


def _moe_ffn_kernel(vinfo, gch, h_ref, gsplit_ref, dest_ref, wg_ref, wu_ref, wd_ref, y_ref,
                    hs, gate_s, acc):
    v = pl.program_id(0)
    j = pl.program_id(1)
    tile = vinfo[v * VISIT_FIELDS + 0]
    expert = vinfo[v * VISIT_FIELDS + 1]
    valid = vinfo[v * VISIT_FIELDS + 2] == 1
    first = vinfo[v * VISIT_FIELDS + 3] == 1
    row_lo = vinfo[v * VISIT_FIELDS + 4]
    row_hi = vinfo[v * VISIT_FIELDS + 5]
    subs = MOE_TM // MOE_CHUNK

    @pl.when(valid & (j == 0))
    def _():
        row_iota = lax.broadcasted_iota(jnp.int32, (MOE_CHUNK, MOE_CHUNK), 0)
        gate_lane = lax.broadcasted_iota(jnp.int32, (MOE_CHUNK, 2 * LANES), 1) % LANES
        for s in range(subs):
            rows = slice(s * MOE_CHUNK, (s + 1) * MOE_CHUNK)

            @pl.when(first)
            def _():
                hs[rows, :] = jnp.zeros((MOE_CHUNK, D_MODEL), BF16)
                gate_s[rows, :] = jnp.zeros((MOE_CHUNK, 1), F32)

            base = tile * MOE_TM + s * MOE_CHUNK

            def body(c, carry):
                onehot = jnp.where(row_iota + base == dest_ref[pl.ds(c, 1), :], 1.0, 0.0).astype(BF16)
                start = pl.multiple_of(c * MOE_CHUNK, MOE_CHUNK)
                part = jnp.dot(onehot, h_ref[pl.ds(start, MOE_CHUNK), :], preferred_element_type=F32)
                hs[rows, :] = hs[rows, :] + part.astype(BF16)
                gpart = jnp.dot(onehot, gsplit_ref[pl.ds(start, MOE_CHUNK), :], preferred_element_type=F32)
                gate_s[rows, :] = gate_s[rows, :] + jnp.sum(
                    jnp.where(gate_lane == expert, gpart, 0.0), axis=-1, keepdims=True)
                return carry

            lax.fori_loop(gch[(v * subs + s) * 2], gch[(v * subs + s) * 2 + 1] + 1, body, 0)
        acc[...] = jnp.zeros_like(acc)

    @pl.when(valid)
    def _():
        h = hs[...]
        g = jnp.dot(h, wg_ref[...].astype(BF16), preferred_element_type=F32)
        u = jnp.dot(h, wu_ref[...].astype(BF16), preferred_element_type=F32)
        a = _silu(g) * u * gate_s[...]
        acc[...] += jnp.dot(a.astype(BF16), wd_ref[...].astype(BF16), preferred_element_type=F32)

    last = valid & (j == pl.num_programs(1) - 1)
    row = lax.broadcasted_iota(jnp.int32, (MOE_TM, 1), 0)
    mine = jnp.where(row >= row_lo, 1.0, 0.0) * jnp.where(row < row_hi, 1.0, 0.0)

    @pl.when(last & first)
    def _():
        y_ref[...] = (acc[...] * mine).astype(BF16)

    @pl.when(last & jnp.logical_not(first))
    def _():
        y_ref[...] = jnp.where(mine > 0.0, acc[...], y_ref[...].astype(F32)).astype(BF16)


def _moe_ffn_call(vinfo, gch, h, gsplit, dest_e, wg, wu, wd):
    tm, tf = MOE_TM, FFN_TF
    n_j = D_FF // tf

    def expert_of(v, vinfo):
        return vinfo[v * VISIT_FIELDS + 1]

    def w_col(v, j, vinfo):
        return jnp.where(vinfo[v * VISIT_FIELDS + 2] == 1, j, n_j - 1)

    grid_spec = pltpu.PrefetchScalarGridSpec(
        num_scalar_prefetch=2,
        grid=(MOE_MAX_VISITS, n_j),
        in_specs=[
            pl.BlockSpec((N_TOK, D_MODEL), lambda v, j, vinfo, gch: (0, 0), pipeline_mode=pl.Buffered(1)),
            pl.BlockSpec((N_TOK, 2 * LANES), lambda v, j, vinfo, gch: (0, 0), pipeline_mode=pl.Buffered(1)),
            pl.BlockSpec((None, MOE_NCHUNK, MOE_CHUNK), lambda v, j, vinfo, gch: (expert_of(v, vinfo), 0, 0)),
            pl.BlockSpec((None, D_MODEL, tf),
                         lambda v, j, vinfo, gch: (expert_of(v, vinfo), 0, w_col(v, j, vinfo))),
            pl.BlockSpec((None, D_MODEL, tf),
                         lambda v, j, vinfo, gch: (expert_of(v, vinfo), 0, w_col(v, j, vinfo))),
            pl.BlockSpec((None, tf, D_MODEL),
                         lambda v, j, vinfo, gch: (expert_of(v, vinfo), w_col(v, j, vinfo), 0)),
        ],
        out_specs=pl.BlockSpec((tm, D_MODEL), lambda v, j, vinfo, gch: (vinfo[v * VISIT_FIELDS], 0)),
        scratch_shapes=[pltpu.VMEM((tm, D_MODEL), BF16), pltpu.VMEM((tm, 1), F32),
                        pltpu.VMEM((tm, D_MODEL), F32)],
    )
    return pl.pallas_call(
        _moe_ffn_kernel,
        grid_spec=grid_spec,
        out_shape=jax.ShapeDtypeStruct((MOE_ROWS, D_MODEL), BF16),
        compiler_params=_params(2),
        name="channel_mixer_experts",
    )(vinfo, gch, h, gsplit, dest_e, wg, wu, wd)


def _combine_kernel(cstart, ccount, xp_ref, xs_ref, meta_ref, y_ref, mod_ref, lng_ref, lnb_ref, o_ref, acc, *,
                    n_prompt_tiles):
    c = pl.program_id(0)
    acc[...] = jnp.zeros_like(acc)
    meta = meta_ref[...]
    e1, e2, d1, d2 = meta[:, 0:1], meta[:, 1:2], meta[:, 2:3], meta[:, 3:4]
    col = lax.broadcasted_iota(jnp.int32, (MOE_CHUNK, MOE_WINDOW), 1).astype(F32)
    for e in range(N_EXPERTS):
        @pl.when(ccount[c * N_EXPERTS + e] > 0)
        def _():
            start = pl.multiple_of(cstart[c * N_EXPERTS + e], MOE_CHUNK)
            dest = jnp.where(e1 == float(e), d1, jnp.where(e2 == float(e), d2, -1.0)) - start.astype(F32)
            onehot = jnp.where(col == dest, 1.0, 0.0).astype(BF16)
            acc[...] += jnp.dot(onehot, y_ref[pl.ds(start, MOE_WINDOW), :], preferred_element_type=F32)

    def finish(x_ref):
        gate = mod_ref[:, 5 * D_MODEL:6 * D_MODEL]
        y = DEEPNORM_ALPHA * x_ref[...] + gate * acc[...]
        o_ref[...] = _layer_norm(y, lng_ref[...], lnb_ref[...])

    @pl.when(c < n_prompt_tiles)
    def _():
        finish(xp_ref)

    @pl.when(c >= n_prompt_tiles)
    def _():
        finish(xs_ref)


def _combine_call(cstart, ccount, xp, xs, meta2, y, mod3, mod_row_fn, ln_g, ln_b):
    tm = MOE_CHUNK
    npt = N_PROMPT // tm
    grid_spec = pltpu.PrefetchScalarGridSpec(
        num_scalar_prefetch=2,
        grid=(MOE_NCHUNK,),
        in_specs=[
            pl.BlockSpec((tm, D_MODEL), lambda c, a, b: (jnp.minimum(c, npt - 1), 0)),
            pl.BlockSpec((tm, D_MODEL), lambda c, a, b: (jnp.maximum(c - npt, 0), 0)),
            pl.BlockSpec((tm, LANES), lambda c, a, b: (c, 0)),
            pl.BlockSpec((MOE_ROWS, D_MODEL), lambda c, a, b: (0, 0), pipeline_mode=pl.Buffered(1)),
            pl.BlockSpec((None, 1, N_MOD), lambda c, a, b: (mod_row_fn(c), 0, 0)),
            pl.BlockSpec((1, D_MODEL), lambda c, a, b: (0, 0)),
            pl.BlockSpec((1, D_MODEL), lambda c, a, b: (0, 0)),
        ],
        out_specs=pl.BlockSpec((tm, D_MODEL), lambda c, a, b: (c, 0)),
        scratch_shapes=[pltpu.VMEM((tm, D_MODEL), F32)],
    )
    return pl.pallas_call(
        functools.partial(_combine_kernel, n_prompt_tiles=npt),
        grid_spec=grid_spec,
        out_shape=jax.ShapeDtypeStruct((N_TOK, D_MODEL), F32),
        compiler_params=_params(1),
        name="expert_combine",
    )(cstart, ccount, xp, xs, meta2, y, mod3, ln_g, ln_b)


def _moe_plan(meta, cum, total):
    i32 = jnp.int32
    cnt = total[0, :N_EXPERTS].astype(i32)
    off = jnp.cumsum(cnt) - cnt
    e1, e2 = meta[:, 0].astype(i32), meta[:, 1].astype(i32)
    d1 = off[e1] + meta[:, 4].astype(i32)
    d2 = off[e2] + meta[:, 5].astype(i32)
    eid = jnp.arange(N_EXPERTS, dtype=i32)[:, None]
    dest_e = jnp.where(e1[None, :] == eid, d1[None, :], jnp.where(e2[None, :] == eid, d2[None, :], -1))
    dest_e = dest_e.reshape(N_EXPERTS, MOE_NCHUNK, MOE_CHUNK)
    meta2 = jnp.zeros((N_TOK, LANES), F32).at[:, 0:2].set(meta[:, 0:2]).at[:, 2].set(
        d1.astype(F32)).at[:, 3].set(d2.astype(F32))
    cumc = jnp.concatenate([cum[:, 0, :N_EXPERTS], total[0:1, :N_EXPERTS]], axis=0).astype(i32)

    n_tiles = MOE_ROWS // MOE_TM
    t0 = jnp.arange(n_tiles, dtype=i32)[:, None] * MOE_TM
    lo = jnp.maximum(t0, off[None, :]).reshape(-1)
    hi = jnp.minimum(t0 + MOE_TM, (off + cnt)[None, :]).reshape(-1)
    ok = hi > lo
    n_visits = jnp.sum(ok.astype(i32))
    order = jnp.argsort(jnp.logical_not(ok), stable=True)[:MOE_MAX_VISITS].astype(i32)
    slot = jnp.arange(MOE_MAX_VISITS, dtype=i32)
    valid = slot < n_visits
    order = order[jnp.minimum(slot, n_visits - 1)]
    vt, ve = order // N_EXPERTS, order % N_EXPERTS
    vlo = jnp.where(valid, lo[order] - vt * MOE_TM, 0)
    vhi = jnp.where(valid, hi[order] - vt * MOE_TM, 0)
    zero = jnp.zeros_like(vt)
    vinfo = jnp.stack([vt, ve, valid.astype(i32), (valid & (vlo == 0)).astype(i32), vlo, vhi, zero, zero],
                      axis=1).reshape(-1).astype(i32)

    subs = MOE_TM // MOE_CHUNK
    s0 = jnp.arange(subs, dtype=i32)[None, :] * MOE_CHUNK
    rlo = jnp.maximum(vlo[:, None], s0)
    rhi = jnp.minimum(vhi[:, None], s0 + MOE_CHUNK)
    to_rank = (vt * MOE_TM - off[ve])[:, None]
    cum_v = cumc[:, ve]
    c_lo = jnp.sum((cum_v[:, :, None] <= (rlo + to_rank)[None]).astype(i32), axis=0) - 1
    c_hi = jnp.sum((cum_v[:, :, None] < (rhi + to_rank)[None]).astype(i32), axis=0) - 1
    empty = rhi <= rlo
    c_lo = jnp.where(empty, 1, jnp.clip(c_lo, 0, MOE_NCHUNK - 1))
    c_hi = jnp.where(empty, 0, jnp.clip(c_hi, 0, MOE_NCHUNK - 1))
    gch = jnp.stack([c_lo, c_hi], axis=-1).reshape(-1).astype(i32)

    seg_lo = off[None, :] + cumc[:-1]
    seg_n = cumc[1:] - cumc[:-1]
    cstart = jnp.clip((seg_lo // MOE_CHUNK) * MOE_CHUNK, 0, MOE_ROWS - MOE_WINDOW)
    return vinfo, gch, dest_e, meta2, cstart.reshape(-1).astype(i32), seg_n.reshape(-1).astype(i32)


def _rope_tables():
    rows = DEC_SEQ // GRID_W
    row = jnp.repeat(jnp.arange(rows, dtype=F32), GRID_W)
    col = jnp.tile(jnp.arange(GRID_W, dtype=F32), rows)
    n_freq = HEAD_DIM // 4
    inv = ROPE_THETA ** (-jnp.arange(n_freq, dtype=F32) / n_freq)
    ar = row[:, None] * inv
    ac = col[:, None] * inv
    ang = jnp.concatenate([ar, ar, ac, ac], axis=-1)
    cos = jnp.tile(jnp.cos(ang), (1, LANES // HEAD_DIM))
    sin = jnp.tile(jnp.sin(ang), (1, LANES // HEAD_DIM))
    first_half = (jnp.arange(LANES) % (2 * n_freq)) < n_freq
    sin_next = jnp.where(first_half, -sin, 0.0)
    sin_prev = jnp.where(first_half, 0.0, sin)
    return cos, sin_next, sin_prev


def kernel(x_prompt, x_sample, cache_a_k, cache_a_v, cache_b_k, cache_b_v, c, c_ctx, w_ada, b_ada, w_in,
           w_out, diff_lambda, diff_subln, qk_norm_gain, ln_gain, ln_bias, w_ffn_gate, w_ffn_up,
           w_ffn_down, w_router, w_moe_gate, w_moe_up, w_moe_down):
    cond = jnp.zeros((MOD_ROWS, D_MODEL), F32).at[0].set(c_ctx).at[1:1 + DEC_BATCH].set(c)
    mod3 = _ada_call(cond, w_ada, b_ada).reshape(DEPTH * MOD_ROWS, 1, N_MOD)
    rope_tabs = _rope_tables()
    cache = (cache_a_k.reshape(DEC_BATCH, DEPTH, PAST_LEN, DIFF_WIDTH),
             cache_a_v.reshape(DEC_BATCH, DEPTH, PAST_LEN, DIFF_WIDTH),
             cache_b_k.reshape(DEC_BATCH, DEPTH, PAST_LEN, KV_WIDTH),
             cache_b_v.reshape(DEC_BATCH, DEPTH, PAST_LEN, KV_WIDTH))

    xp = x_prompt.reshape(N_PROMPT, D_MODEL)
    xs = x_sample.reshape(N_SAMPLE, D_MODEL)
    xs_off = 0
    new_caches = []
    for l in range(DEPTH):
        base = l * MOD_ROWS
        gq = jnp.tile(qk_norm_gain[l, 0], LANES // HEAD_DIM).reshape(1, LANES)
        gk = jnp.tile(qk_norm_gain[l, 1], LANES // HEAD_DIM).reshape(1, LANES)
        subln = diff_subln[l].reshape(1, LANES)
        ln1_g, ln1_b = ln_gain[l, 0].reshape(1, D_MODEL), ln_bias[l, 0].reshape(1, D_MODEL)
        ln2_g, ln2_b = ln_gain[l, 1].reshape(1, D_MODEL), ln_bias[l, 1].reshape(1, D_MODEL)

        q_p, kv_p, ak, av, bk, bv = _inproj_call(
            xp, 0, N_PROMPT, mod3, lambda i: base, w_in, gq, gk, l, None, True)
        new_caches.append((ak, av, bk, bv))
        x1_p = _attn_call(q_p, kv_p, None, xp, 0, mod3, lambda b: base, w_out, diff_lambda, subln,
                          ln1_g, ln1_b, l, BATCH, SEQ)
        s_tiles = DEC_SEQ // PROJ_TM
        q_s, kv_s = _inproj_call(
            xs, xs_off, N_SAMPLE, mod3, lambda i: base + 1 + i // s_tiles, w_in, gq, gk, l, rope_tabs, False)
        x1_s = _attn_call(q_s, kv_s, cache, xs, xs_off, mod3, lambda b: base + 1 + b, w_out, diff_lambda,
                          subln, ln1_g, ln1_b, l, DEC_BATCH, DEC_SEQ)

        npt = N_PROMPT // FFN_TM
        s_ffn_tiles = DEC_SEQ // FFN_TM
        mod_row = lambda i: base + jnp.where(i < npt, 0, 1 + (i - npt) // s_ffn_tiles)
        k = l // 2
        if l % 2 == 0:
            x2 = _ffn_call(x1_p, x1_s, mod3, mod_row, w_ffn_gate, w_ffn_up, w_ffn_down, k, ln2_g, ln2_b)
        else:
            wr = jnp.zeros((D_MODEL, LANES), F32).at[:, :N_EXPERTS].set(w_router[k])
            h2, gsplit, meta, cum, total = _router_call(x1_p, x1_s, mod3, mod_row, wr)
            vinfo, gch, dest_e, meta2, cstart, ccount = _moe_plan(meta, cum, total)
            y = _moe_ffn_call(vinfo, gch, h2, gsplit, dest_e, w_moe_gate[k], w_moe_up[k], w_moe_down[k])
            npc = N_PROMPT // MOE_CHUNK
            s_chunks = DEC_SEQ // MOE_CHUNK
            mod_row_c = lambda c: base + jnp.where(c < npc, 0, 1 + (c - npc) // s_chunks)
            x2 = _combine_call(cstart, ccount, x1_p, x1_s, meta2, y, mod3, mod_row_c, ln2_g, ln2_b)
        xp, xs, xs_off = x2, x2, N_PROMPT

    y_prompt = x2[:N_PROMPT].reshape(BATCH, SEQ, D_MODEL)
    y_sample = x2[N_PROMPT:].reshape(DEC_BATCH, DEC_SEQ, D_MODEL)
    stack = lambda idx, shape: jnp.stack([nc[idx] for nc in new_caches], axis=0).reshape(
        (DEPTH, BATCH, SEQ) + shape).transpose((1, 0, 2) + tuple(range(3, 3 + len(shape))))
    return (y_prompt, y_sample,
            stack(0, (DIFF_HEADS, 2, HEAD_DIM)), stack(1, (DIFF_HEADS, 2 * HEAD_DIM)),
            stack(2, (GQA_KV_HEADS, HEAD_DIM)), stack(3, (GQA_KV_HEADS, HEAD_DIM)))
```

```python
import functools
import math

import jax
import jax.numpy as jnp
from jax import lax
from jax.experimental import pallas as pl
from jax.experimental.pallas import tpu as pltpu

D_MODEL = 1024
BATCH = 16
SEQ = 256
DEPTH = 2
DEC_BATCH = 2
DEC_SEQ = 2048
PAST_LEN = 256
GRID_W = 64
HEAD_DIM = 64
DIFF_HEADS = 4
GQA_HEADS = 8
GQA_KV_HEADS = 2
DIFF_WIDTH = DIFF_HEADS * 2 * HEAD_DIM
GQA_WIDTH = GQA_HEADS * HEAD_DIM
KV_WIDTH = GQA_KV_HEADS * HEAD_DIM
IN_COLS = 3 * DIFF_WIDTH + GQA_WIDTH + 2 * KV_WIDTH
D_FF = 2816
N_EXPERTS = 8
ROPE_THETA = 10000.0
EPS = 1e-6
DEEPNORM_ALPHA = (2 * DEPTH) ** 0.25

N_PROMPT = BATCH * SEQ
N_SAMPLE = DEC_BATCH * DEC_SEQ
N_TOK = N_PROMPT + N_SAMPLE
N_MOD = 6 * D_MODEL
MOD_ROWS = 8

LANES = 128
VMEM_LIMIT = 56 * 1024 * 1024

ADA_TN = 1536
PROJ_TM = 512
ATT_TQ = 256
FFN_TM = 1024
FFN_TF = 256
MOE_TM = 1024
MOE_CHUNK = 256
MOE_HALF = MOE_CHUNK // 2
MOE_WINDOW = 2 * MOE_HALF
MOE_GATHER_CHUNKS = 6
MOE_ROWS = 2 * N_TOK
MOE_NCHUNK = N_TOK // MOE_CHUNK
MOE_SUBS = MOE_TM // MOE_CHUNK
MOE_MAX_VISITS = MOE_ROWS // MOE_TM + N_EXPERTS - 1

KV_DIFF_STRIDE = 3 * LANES
KV_GQA_OFF = DIFF_HEADS * KV_DIFF_STRIDE
KV_PREP_COLS = KV_GQA_OFF + 4 * LANES
KV_NEW_COLS = 2 * DIFF_WIDTH + 4 * KV_WIDTH

F32 = jnp.float32
BF16 = jnp.bfloat16


def _params(n_axes):
    return pltpu.CompilerParams(dimension_semantics=("arbitrary",) * n_axes,
                                vmem_limit_bytes=VMEM_LIMIT)


def _layer_norm(y, g, b):
    mu = jnp.mean(y, axis=-1, keepdims=True)
    yc = y - mu
    var = jnp.mean(yc * yc, axis=-1, keepdims=True)
    return yc * lax.rsqrt(var + EPS) * g + b


def _silu(x):
    return x * jax.nn.sigmoid(x)


def _half_masks():
    lane = lax.broadcasted_iota(jnp.int32, (1, LANES), 1)
    lo = (lane < HEAD_DIM).astype(F32)
    return lo, 1.0 - lo


def _ada_kernel(cond_ref, w_ref, b_ref, o_ref):
    s = _silu(cond_ref[...])
    o_ref[...] = jnp.dot(s, w_ref[...], precision=lax.Precision.HIGHEST,
                         preferred_element_type=F32) + b_ref[...]


def _ada_call(cond, w_ada, b_ada):
    return pl.pallas_call(
        _ada_kernel,
        grid=(DEPTH, N_MOD // ADA_TN),
        in_specs=[
            pl.BlockSpec((MOD_ROWS, D_MODEL), lambda l, n: (0, 0)),
            pl.BlockSpec((None, D_MODEL, ADA_TN), lambda l, n: (l, 0, n)),
            pl.BlockSpec((None, 1, ADA_TN), lambda l, n: (l, 0, n)),
        ],
        out_specs=pl.BlockSpec((None, MOD_ROWS, ADA_TN), lambda l, n: (l, 0, n)),
        out_shape=jax.ShapeDtypeStruct((DEPTH, MOD_ROWS, N_MOD), F32),
        compiler_params=_params(2),
        name="ada_modulation",
    )(cond, w_ada, b_ada.reshape(DEPTH, 1, N_MOD))


def _head_sumsq(x):
    r = lax.broadcasted_iota(jnp.int32, (LANES, LANES), 0) // HEAD_DIM
    c = lax.broadcasted_iota(jnp.int32, (LANES, LANES), 1) // HEAD_DIM
    ones = (r == c).astype(BF16)
    sq = x * x
    hi = sq.astype(BF16)
    lo = (sq - hi.astype(F32)).astype(BF16)
    return (jnp.dot(hi, ones, preferred_element_type=F32)
            + jnp.dot(lo, ones, preferred_element_type=F32))


def _inproj_kernel(*refs, rope, caches):
    x_ref, mod_ref, w_ref, gq_ref, gk_ref = refs[:5]
    pos = 5
    if rope:
        cos_ref, sa_ref, sb_ref = refs[pos:pos + 3]
        pos += 3
    q_out, kv_out = refs[pos:pos + 2]
    pos += 2
    if caches:
        ak_out, av_out, bk_out, bv_out = refs[pos:pos + 4]
        pos += 4
    wbf = refs[pos]

    @pl.when(pl.program_id(0) == 0)
    def _():
        wbf[...] = w_ref[...].astype(BF16)

    shift = mod_ref[:, 0:D_MODEL]
    scale = mod_ref[:, D_MODEL:2 * D_MODEL]
    h = (x_ref[...] * (1.0 + scale) + shift).astype(BF16)
    proj = jnp.dot(h, wbf[...], preferred_element_type=F32)

    def group(base, g):
        return proj[:, base + g * LANES: base + (g + 1) * LANES]

    def rot(v):
        if not rope:
            return v
        return (v * cos_ref[...] + pltpu.roll(v, LANES - HEAD_DIM // 4, axis=1) * sa_ref[...]
                + pltpu.roll(v, HEAD_DIM // 4, axis=1) * sb_ref[...])

    def normed(v, gain):
        return v * lax.rsqrt(_head_sumsq(v) * (1.0 / HEAD_DIM) + EPS) * gain

    qk_scale = HEAD_DIM ** -0.5
    off_ak, off_av, off_bq = DIFF_WIDTH, 2 * DIFF_WIDTH, 3 * DIFF_WIDTH
    off_bk, off_bv = off_bq + GQA_WIDTH, off_bq + GQA_WIDTH + KV_WIDTH

    for g in range(DIFF_WIDTH // LANES):
        q_out[:, g * LANES:(g + 1) * LANES] = (rot(group(0, g)) * qk_scale).astype(BF16)
        a_k = group(off_ak, g)
        a_v = group(off_av, g)
        if caches:
            ak_out[:, g * LANES:(g + 1) * LANES] = a_k
            av_out[:, g * LANES:(g + 1) * LANES] = a_v
        kv_out[:, g * LANES:(g + 1) * LANES] = rot(a_k).astype(BF16)
        kv_out[:, DIFF_WIDTH + g * LANES: DIFF_WIDTH + (g + 1) * LANES] = a_v.astype(BF16)
    for g in range(GQA_WIDTH // LANES):
        b_q = rot(normed(group(off_bq, g), gq_ref[...]))
        q_out[:, DIFF_WIDTH + g * LANES: DIFF_WIDTH + (g + 1) * LANES] = (b_q * qk_scale).astype(BF16)
    b_k = normed(group(off_bk, 0), gk_ref[...])
    b_v = group(off_bv, 0)
    if caches:
        bk_out[...] = b_k
        bv_out[...] = b_v
    b_k = rot(b_k)
    base = 2 * DIFF_WIDTH
    kv_out[:, base:base + LANES] = b_k.astype(BF16)
    kv_out[:, base + LANES:base + 2 * LANES] = b_v.astype(BF16)
    kv_out[:, base + 2 * LANES:base + 3 * LANES] = pltpu.roll(b_k, HEAD_DIM, axis=1).astype(BF16)
    kv_out[:, base + 3 * LANES:base + 4 * LANES] = pltpu.roll(b_v, HEAD_DIM, axis=1).astype(BF16)


def _inproj_call(x, row_off, n_rows, mod3, mod_row_fn, w_in, gq, gk, layer, rope_tabs, caches):
    tm = PROJ_TM
    n_tiles = n_rows // tm
    blk_off = row_off // tm
    rope = rope_tabs is not None
    in_specs = [
        pl.BlockSpec((tm, D_MODEL), lambda i: (i + blk_off, 0)),
        pl.BlockSpec((None, 1, N_MOD), lambda i: (mod_row_fn(i), 0, 0)),
        pl.BlockSpec((None, D_MODEL, IN_COLS), lambda i: (layer, 0, 0)),
        pl.BlockSpec((1, LANES), lambda i: (0, 0)),
        pl.BlockSpec((1, LANES), lambda i: (0, 0)),
    ]
    args = [x, mod3, w_in, gq, gk]
    if rope:
        pos_tiles = DEC_SEQ // tm
        for t in rope_tabs:
            in_specs.append(pl.BlockSpec((tm, LANES), lambda i: (i % pos_tiles, 0)))
            args.append(t)
    out_shape = [jax.ShapeDtypeStruct((n_rows, 2 * DIFF_WIDTH), BF16),
                 jax.ShapeDtypeStruct((n_rows, KV_NEW_COLS), BF16)]
    out_specs = [pl.BlockSpec((tm, 2 * DIFF_WIDTH), lambda i: (i, 0)),
                 pl.BlockSpec((tm, KV_NEW_COLS), lambda i: (i, 0))]
    if caches:
        for width in (DIFF_WIDTH, DIFF_WIDTH, KV_WIDTH, KV_WIDTH):
            out_shape.append(jax.ShapeDtypeStruct((n_rows, width), F32))
            out_specs.append(pl.BlockSpec((tm, width), lambda i: (i, 0)))
    return pl.pallas_call(
        functools.partial(_inproj_kernel, rope=rope, caches=caches),
        grid=(n_tiles,),
        in_specs=in_specs,
        out_specs=out_specs,
        out_shape=out_shape,
        scratch_shapes=[pltpu.VMEM((D_MODEL, IN_COLS), BF16)],
        compiler_params=_params(1),
        name="in_projection_rope" if rope else "in_projection_ctx",
    )(*args)


def _attn_kernel(*refs, n_new, n_cache, lam_init):
    q_ref, kv_ref = refs[:2]
    pos = 2
    if n_cache:
        cak_ref, cav_ref, cbk_ref, cbv_ref = refs[pos:pos + 4]
        pos += 4
    x_ref, mod_ref, wout_ref, lam_ref, subln_ref, lng_ref, lnb_ref, o_ref = refs[pos:pos + 8]
    kvs, wbf, oscr = refs[pos + 8:pos + 11]

    b = pl.program_id(0)
    qi = pl.program_id(1)
    lo_f, hi_f = _half_masks()
    lo_b, hi_b = lo_f.astype(BF16), hi_f.astype(BF16)

    @pl.when((b == 0) & (qi == 0))
    def _():
        wbf[...] = wout_ref[...].astype(BF16)

    @pl.when(qi == 0)
    def _():
        for h in range(DIFF_HEADS):
            k = kv_ref[:, h * LANES:(h + 1) * LANES]
            c0 = h * KV_DIFF_STRIDE
            kvs[0:n_new, c0:c0 + LANES] = k * lo_b
            kvs[0:n_new, c0 + LANES:c0 + 2 * LANES] = k * hi_b
            kvs[0:n_new, c0 + 2 * LANES:c0 + 3 * LANES] = kv_ref[:, DIFF_WIDTH + h * LANES:
                                                                 DIFF_WIDTH + (h + 1) * LANES]
            if n_cache:
                kc = cak_ref[:, h * LANES:(h + 1) * LANES]
                kvs[n_new:n_new + n_cache, c0:c0 + LANES] = (kc * lo_f).astype(BF16)
                kvs[n_new:n_new + n_cache, c0 + LANES:c0 + 2 * LANES] = (kc * hi_f).astype(BF16)
                kvs[n_new:n_new + n_cache, c0 + 2 * LANES:c0 + 3 * LANES] = (
                    cav_ref[:, h * LANES:(h + 1) * LANES].astype(BF16))
        kvs[0:n_new, KV_GQA_OFF:KV_GQA_OFF + 4 * LANES] = kv_ref[:, 2 * DIFF_WIDTH:2 * DIFF_WIDTH + 4 * LANES]
        if n_cache:
            ck = cbk_ref[...]
            cv = cbv_ref[...]
            rows = slice(n_new, n_new + n_cache)
            kvs[rows, KV_GQA_OFF:KV_GQA_OFF + LANES] = ck.astype(BF16)
            kvs[rows, KV_GQA_OFF + LANES:KV_GQA_OFF + 2 * LANES] = cv.astype(BF16)
            kvs[rows, KV_GQA_OFF + 2 * LANES:KV_GQA_OFF + 3 * LANES] = pltpu.roll(ck, HEAD_DIM, axis=1).astype(BF16)
            kvs[rows, KV_GQA_OFF + 3 * LANES:KV_GQA_OFF + 4 * LANES] = pltpu.roll(cv, HEAD_DIM, axis=1).astype(BF16)

    lp = lam_ref[...]
    lam = (jnp.exp(jnp.sum(lp[0:1] * lp[1:2], axis=-1, keepdims=True))
           - jnp.exp(jnp.sum(lp[2:3] * lp[3:4], axis=-1, keepdims=True)) + lam_init)

    def scores(q, k):
        return lax.dot_general(q, k, (((1,), (1,)), ((), ())), preferred_element_type=F32)

    def softmax_parts(s):
        e = jnp.exp(s - jnp.max(s, axis=-1, keepdims=True))
        return e, 1.0 / jnp.sum(e, axis=-1, keepdims=True)

    for h in range(DIFF_HEADS):
        c0 = h * KV_DIFF_STRIDE
        q = q_ref[:, h * LANES:(h + 1) * LANES]
        e1, r1 = softmax_parts(scores(q, kvs[:, c0:c0 + LANES]))
        e2, r2 = softmax_parts(scores(q, kvs[:, c0 + LANES:c0 + 2 * LANES]))
        a = (e1 * r1 - e2 * (lam * r2)).astype(BF16)
        o = jnp.dot(a, kvs[:, c0 + 2 * LANES:c0 + 3 * LANES], preferred_element_type=F32)
        o = o * lax.rsqrt(jnp.mean(o * o, axis=-1, keepdims=True) + EPS) * subln_ref[...]
        oscr[:, h * LANES:(h + 1) * LANES] = (o * (1.0 - lam_init)).astype(BF16)

    for pair in range(GQA_HEADS // 2):
        q_pair = q_ref[:, DIFF_WIDTH + pair * LANES: DIFF_WIDTH + (pair + 1) * LANES]
        halves = []
        for c in range(2):
            kv_head = (2 * pair + c) // (GQA_HEADS // GQA_KV_HEADS)
            koff = KV_GQA_OFF if kv_head == c else KV_GQA_OFF + 2 * LANES
            e, r = softmax_parts(scores(q_pair * (lo_b if c == 0 else hi_b), kvs[:, koff:koff + LANES]))
            o = jnp.dot(e.astype(BF16), kvs[:, koff + LANES:koff + 2 * LANES], preferred_element_type=F32)
            halves.append(o * r)
        o_pair = halves[0] * lo_f + halves[1] * hi_f
        oscr[:, DIFF_WIDTH + pair * LANES: DIFF_WIDTH + (pair + 1) * LANES] = o_pair.astype(BF16)

    mix = jnp.dot(oscr[...], wbf[...], preferred_element_type=F32)
    gate = mod_ref[:, 2 * D_MODEL:3 * D_MODEL]
    y = DEEPNORM_ALPHA * x_ref[...] + gate * mix
    o_ref[...] = _layer_norm(y, lng_ref[...], lnb_ref[...])


def _attn_call(q, kv, cache, x, x_row_off, mod3, mod_row_fn, w_out, diff_lambda, subln, ln_g, ln_b,
               layer, n_batch, n_new):
    tq = ATT_TQ
    nq = n_new // tq
    x_blk_off = x_row_off // tq
    n_cache = PAST_LEN if cache is not None else 0
    lam_init = 0.8 - 0.6 * math.exp(-0.3 * layer)
    in_specs = [
        pl.BlockSpec((tq, 2 * DIFF_WIDTH), lambda b, i: (b * nq + i, 0)),
        pl.BlockSpec((n_new, KV_NEW_COLS), lambda b, i: (b, 0)),
    ]
    args = [q, kv]
    if cache is not None:
        for arr in cache:
            width = arr.shape[-1]
            in_specs.append(pl.BlockSpec((None, None, PAST_LEN, width), lambda b, i: (b, layer, 0, 0)))
            args.append(arr)
    in_specs += [
        pl.BlockSpec((tq, D_MODEL), lambda b, i: (b * nq + i + x_blk_off, 0)),
        pl.BlockSpec((None, 1, N_MOD), lambda b, i: (mod_row_fn(b), 0, 0)),
        pl.BlockSpec((None, D_MODEL, D_MODEL), lambda b, i: (layer, 0, 0)),
        pl.BlockSpec((None, 4, HEAD_DIM), lambda b, i: (layer, 0, 0)),
        pl.BlockSpec((1, LANES), lambda b, i: (0, 0)),
        pl.BlockSpec((1, D_MODEL), lambda b, i: (0, 0)),
        pl.BlockSpec((1, D_MODEL), lambda b, i: (0, 0)),
    ]
    args += [x, mod3, w_out, diff_lambda, subln, ln_g, ln_b]
    return pl.pallas_call(
        functools.partial(_attn_kernel, n_new=n_new, n_cache=n_cache, lam_init=lam_init),
        grid=(n_batch, nq),
        in_specs=in_specs,
        out_specs=pl.BlockSpec((tq, D_MODEL), lambda b, i: (b * nq + i, 0)),
        out_shape=jax.ShapeDtypeStruct((n_batch * n_new, D_MODEL), F32),
        scratch_shapes=[pltpu.VMEM((n_new + n_cache, KV_PREP_COLS), BF16),
                        pltpu.VMEM((D_MODEL, D_MODEL), BF16),
                        pltpu.VMEM((tq, D_MODEL), BF16)],
        compiler_params=_params(2),
        name="token_mixer_latent" if cache is not None else "token_mixer_ctx",
    )(*args)


def _router_kernel(xp_ref, xs_ref, mod_ref, wr_ref, h_ref, gsplit_ref, meta_ref, meta_t_ref, cum_ref,
                   total_ref, carry, *, n_prompt_tiles):
    i = pl.program_id(0)

    @pl.when(i == 0)
    def _():
        carry[...] = jnp.zeros_like(carry)

    def run(x_ref):
        tm = x_ref.shape[0]
        shift = mod_ref[:, 3 * D_MODEL:4 * D_MODEL]
        scale = mod_ref[:, 4 * D_MODEL:5 * D_MODEL]
        h = x_ref[...] * (1.0 + scale) + shift
        h_ref[...] = h.astype(BF16)
        logits = jnp.dot(h, wr_ref[...], precision=lax.Precision.HIGHEST, preferred_element_type=F32)
        lane = lax.broadcasted_iota(jnp.int32, logits.shape, 1).astype(F32)
        neg = jnp.float32(-jnp.inf)
        logits = jnp.where(lane < N_EXPERTS, logits, neg)
        m1 = jnp.max(logits, axis=-1, keepdims=True)
        i1 = jnp.min(jnp.where(logits == m1, lane, float(LANES)), axis=-1, keepdims=True)
        rest = jnp.where(lane == i1, neg, logits)
        m2 = jnp.max(rest, axis=-1, keepdims=True)
        i2 = jnp.min(jnp.where(rest == m2, lane, float(LANES)), axis=-1, keepdims=True)
        e2 = jnp.exp(m2 - m1)
        p1 = 1.0 / (1.0 + e2)
        p2 = e2 / (1.0 + e2)
        hit1 = lane == i1
        hit2 = lane == i2
        gates = jnp.where(hit1, p1, 0.0) + jnp.where(hit2, p2, 0.0)
        g_hi = gates.astype(BF16)
        gsplit_ref[:, 0:LANES] = g_hi
        gsplit_ref[:, LANES:2 * LANES] = (gates - g_hi.astype(F32)).astype(BF16)

        sel = jnp.where(hit1, 1.0, 0.0) + jnp.where(hit2, 1.0, 0.0)
        r = lax.broadcasted_iota(jnp.int32, (tm, tm), 0)
        c = lax.broadcasted_iota(jnp.int32, (tm, tm), 1)
        before = jnp.where(c < r, 1.0, 0.0).astype(BF16)
        cumx = jnp.dot(before, sel.astype(BF16), preferred_element_type=F32) + carry[0:1, :]
        rank1 = jnp.sum(jnp.where(hit1, cumx, 0.0), axis=-1, keepdims=True)
        rank2 = jnp.sum(jnp.where(hit2, cumx, 0.0), axis=-1, keepdims=True)
        meta = jnp.zeros_like(logits)
        for k, val in enumerate((i1, i2, rank1, rank2)):
            meta = jnp.where(lane == float(k), val, meta)
        meta_ref[...] = meta
        meta_t_ref[...] = jnp.transpose(meta)[0:8, :]
        for k in range(tm // MOE_HALF):
            cum_ref[k] = jnp.broadcast_to(cumx[k * MOE_HALF:k * MOE_HALF + 1, :], (8, LANES))
        new_carry = carry[0:1, :] + jnp.sum(sel, axis=0, keepdims=True)
        carry[...] = jnp.broadcast_to(new_carry, carry.shape)
        total_ref[...] = jnp.broadcast_to(new_carry, total_ref.shape)

    @pl.when(i < n_prompt_tiles)
    def _():
        run(xp_ref)

    @pl.when(i >= n_prompt_tiles)
    def _():
        run(xs_ref)


def _router_call(xp, xs, mod3, mod_row_fn, w_router_pad):
    tm = FFN_TM
    npt = N_PROMPT // tm
    halves = tm // MOE_HALF
    return pl.pallas_call(
        functools.partial(_router_kernel, n_prompt_tiles=npt),
        grid=(N_TOK // tm,),
        in_specs=[
            pl.BlockSpec((tm, D_MODEL), lambda i: (jnp.minimum(i, npt - 1), 0)),
            pl.BlockSpec((tm, D_MODEL), lambda i: (jnp.maximum(i - npt, 0), 0)),
            pl.BlockSpec((None, 1, N_MOD), lambda i: (mod_row_fn(i), 0, 0)),
            pl.BlockSpec((D_MODEL, LANES), lambda i: (0, 0)),
        ],
        out_specs=[
            pl.BlockSpec((tm, D_MODEL), lambda i: (i, 0)),
            pl.BlockSpec((tm, 2 * LANES), lambda i: (i, 0)),
            pl.BlockSpec((tm, LANES), lambda i: (i, 0)),
            pl.BlockSpec((8, tm), lambda i: (0, i)),
            pl.BlockSpec((halves, 8, LANES), lambda i: (i, 0, 0)),
            pl.BlockSpec((8, LANES), lambda i: (0, 0)),
        ],
        out_shape=[
            jax.ShapeDtypeStruct((N_TOK, D_MODEL), BF16),
            jax.ShapeDtypeStruct((N_TOK, 2 * LANES), BF16),
            jax.ShapeDtypeStruct((N_TOK, LANES), F32),
            jax.ShapeDtypeStruct((8, N_TOK), F32),
            jax.ShapeDtypeStruct((N_TOK // MOE_HALF, 8, LANES), F32),
            jax.ShapeDtypeStruct((8, LANES), F32),
        ],
        scratch_shapes=[pltpu.VMEM((8, LANES), F32)],
        compiler_params=_params(1),
        name="router",
    )(xp, xs, mod3, w_router_pad)


def _ffn_kernel(xp_ref, xs_ref, mod_ref, wg_ref, wu_ref, wd_ref, lng_ref, lnb_ref, o_ref, hscr, acc, *,
                n_prompt_tiles):
    i = pl.program_id(0)
    j = pl.program_id(1)
    first = j == 0
    last = j == pl.num_programs(1) - 1

    def modulate(x_ref):
        shift = mod_ref[:, 3 * D_MODEL:4 * D_MODEL]
        scale = mod_ref[:, 4 * D_MODEL:5 * D_MODEL]
        hscr[...] = (x_ref[...] * (1.0 + scale) + shift).astype(BF16)
        acc[...] = jnp.zeros_like(acc)

    @pl.when(first & (i < n_prompt_tiles))
    def _():
        modulate(xp_ref)

    @pl.when(first & (i >= n_prompt_tiles))
    def _():
        modulate(xs_ref)

    h = hscr[...]
    g = jnp.dot(h, wg_ref[...].astype(BF16), preferred_element_type=F32)
    u = jnp.dot(h, wu_ref[...].astype(BF16), preferred_element_type=F32)
    a = _silu(g) * u
    acc[...] += jnp.dot(a.astype(BF16), wd_ref[...].astype(BF16), preferred_element_type=F32)

    def finish(x_ref):
        gate = mod_ref[:, 5 * D_MODEL:6 * D_MODEL]
        y = DEEPNORM_ALPHA * x_ref[...] + gate * acc[...]
        o_ref[...] = _layer_norm(y, lng_ref[...], lnb_ref[...])

    @pl.when(last & (i < n_prompt_tiles))
    def _():
        finish(xp_ref)

    @pl.when(last & (i >= n_prompt_tiles))
    def _():
        finish(xs_ref)


def _ffn_call(xp, xs, mod3, mod_row_fn, wg, wu, wd, layer_idx, ln_g, ln_b):
    tm, tf = FFN_TM, FFN_TF
    npt = N_PROMPT // tm
    return pl.pallas_call(
        functools.partial(_ffn_kernel, n_prompt_tiles=npt),
        grid=(N_TOK // tm, D_FF // tf),
        in_specs=[
            pl.BlockSpec((tm, D_MODEL), lambda i, j: (jnp.minimum(i, npt - 1), 0)),
            pl.BlockSpec((tm, D_MODEL), lambda i, j: (jnp.maximum(i - npt, 0), 0)),
            pl.BlockSpec((None, 1, N_MOD), lambda i, j: (mod_row_fn(i), 0, 0)),
            pl.BlockSpec((None, D_MODEL, tf), lambda i, j: (layer_idx, 0, j)),
            pl.BlockSpec((None, D_MODEL, tf), lambda i, j: (layer_idx, 0, j)),
            pl.BlockSpec((None, tf, D_MODEL), lambda i, j: (layer_idx, j, 0)),
            pl.BlockSpec((1, D_MODEL), lambda i, j: (0, 0)),
            pl.BlockSpec((1, D_MODEL), lambda i, j: (0, 0)),
        ],
        out_specs=pl.BlockSpec((tm, D_MODEL), lambda i, j: (i, 0)),
        out_shape=jax.ShapeDtypeStruct((N_TOK, D_MODEL), F32),
        scratch_shapes=[pltpu.VMEM((tm, D_MODEL), BF16), pltpu.VMEM((tm, D_MODEL), F32)],
        compiler_params=_params(2),
        name="channel_mixer_dense",
    )(xp, xs, mod3, wg, wu, wd, ln_g, ln_b)


VISIT_FIELDS = 8
NO_ROW = -1.0e9


def _moe_ffn_kernel(vinfo, gwin, h_ref, gsplit_ref, mt_ref, wg_ref, wu_ref, wd_ref, y_ref,
                    hs, gate_s, acc, wgb, wub, wdb):
    v = pl.program_id(0)
    j = pl.program_id(1)
    tile = vinfo[v * VISIT_FIELDS + 0]
    expert = vinfo[v * VISIT_FIELDS + 1]
    valid = vinfo[v * VISIT_FIELDS + 2] == 1
    row_lo = vinfo[v * VISIT_FIELDS + 3]
    row_hi = vinfo[v * VISIT_FIELDS + 4]
    expert_row0 = vinfo[v * VISIT_FIELDS + 5]
    n_g = MOE_GATHER_CHUNKS

    def sub_rows(s):
        return slice(s * MOE_CHUNK, (s + 1) * MOE_CHUNK)

    def active(s):
        return (row_lo < (s + 1) * MOE_CHUNK) & (row_hi > s * MOE_CHUNK)

    def owns_start(s):
        return row_lo <= s * MOE_CHUNK

    @pl.when(valid & (j == 0))
    def _():
        row_iota = lax.broadcasted_iota(jnp.int32, (MOE_CHUNK, MOE_CHUNK), 0).astype(F32)
        gate_lane = lax.broadcasted_iota(jnp.int32, (MOE_CHUNK, 2 * LANES), 1) % LANES
        expert_f = expert.astype(F32)
        for s in range(MOE_SUBS):
            rows = sub_rows(s)

            @pl.when(active(s) & owns_start(s))
            def _():
                hs[rows, :] = jnp.zeros((MOE_CHUNK, D_MODEL), BF16)
                gate_s[rows, :] = jnp.zeros((MOE_CHUNK, 1), F32)

            @pl.when(active(s))
            def _():
                acc[rows, :] = jnp.zeros((MOE_CHUNK, D_MODEL), F32)
                first_chunk = gwin[(v * MOE_SUBS + s) * 2]
                rank0 = (tile * MOE_TM + s * MOE_CHUNK - expert_row0).astype(F32)

                def body(w, carry):
                    want = first_chunk + w * n_g
                    cs = jnp.minimum(want, MOE_NCHUNK - n_g)
                    pieces = []
                    for k in range(n_g):
                        mt = mt_ref[cs + k]
                        rank = jnp.where(mt[0:1] == expert_f, mt[2:3],
                                         jnp.where(mt[1:2] == expert_f, mt[3:4], NO_ROW))
                        rank = rank + jnp.where(cs + k >= want, 0.0, NO_ROW)
                        pieces.append(jnp.where(row_iota + rank0 == rank, 1.0, 0.0).astype(BF16))
                    onehot = jnp.concatenate(pieces, axis=1)
                    start = pl.multiple_of(cs * MOE_CHUNK, MOE_CHUNK)
                    part = jnp.dot(onehot, h_ref[pl.ds(start, n_g * MOE_CHUNK), :], preferred_element_type=F32)
                    hs[rows, :] = hs[rows, :] + part.astype(BF16)
                    gpart = jnp.dot(onehot, gsplit_ref[pl.ds(start, n_g * MOE_CHUNK), :],
                                    preferred_element_type=F32)
                    gate_s[rows, :] = gate_s[rows, :] + jnp.sum(
                        jnp.where(gate_lane == expert, gpart, 0.0), axis=-1, keepdims=True)
                    return carry

                lax.fori_loop(0, gwin[(v * MOE_SUBS + s) * 2 + 1], body, 0)

    @pl.when(valid)
    def _():
        wgb[...] = wg_ref[...].astype(BF16)
        wub[...] = wu_ref[...].astype(BF16)
        wdb[...] = wd_ref[...].astype(BF16)
        for s in range(MOE_SUBS):
            rows = sub_rows(s)

            @pl.when(active(s))
            def _():
                h = hs[rows, :]
                g = jnp.dot(h, wgb[...], preferred_element_type=F32)
                u = jnp.dot(h, wub[...], preferred_element_type=F32)
                a = _silu(g) * u * gate_s[rows, :]
                acc[rows, :] += jnp.dot(a.astype(BF16), wdb[...], preferred_element_type=F32)

    @pl.when(valid & (j == pl.num_programs(1) - 1))
    def _():
        for s in range(MOE_SUBS):
            rows = sub_rows(s)
            row = lax.broadcasted_iota(jnp.int32, (MOE_CHUNK, 1), 0) + s * MOE_CHUNK
            mine = jnp.where(row >= row_lo, 1.0, 0.0) * jnp.where(row < row_hi, 1.0, 0.0)

            @pl.when(active(s) & owns_start(s))
            def _():
                y_ref[rows, :] = (acc[rows, :] * mine).astype(BF16)

            @pl.when(active(s) & jnp.logical_not(owns_start(s)))
            def _():
                y_ref[rows, :] = jnp.where(mine > 0.0, acc[rows, :], y_ref[rows, :].astype(F32)).astype(BF16)


def _moe_ffn_call(vinfo, gwin, h, gsplit, meta_chunks, wg, wu, wd):
    tm, tf = MOE_TM, FFN_TF
    n_j = D_FF // tf

    def expert_of(v, vinfo):
        return vinfo[v * VISIT_FIELDS + 1]

    def w_col(v, j, vinfo):
        return jnp.where(vinfo[v * VISIT_FIELDS + 2] == 1, j, n_j - 1)

    grid_spec = pltpu.PrefetchScalarGridSpec(
        num_scalar_prefetch=2,
        grid=(MOE_MAX_VISITS, n_j),
        in_specs=[
            pl.BlockSpec((N_TOK, D_MODEL), lambda v, j, vinfo, gwin: (0, 0), pipeline_mode=pl.Buffered(1)),
            pl.BlockSpec((N_TOK, 2 * LANES), lambda v, j, vinfo, gwin: (0, 0), pipeline_mode=pl.Buffered(1)),
            pl.BlockSpec((MOE_NCHUNK, 8, MOE_CHUNK), lambda v, j, vinfo, gwin: (0, 0, 0),
                         pipeline_mode=pl.Buffered(1)),
            pl.BlockSpec((None, D_MODEL, tf),
                         lambda v, j, vinfo, gwin: (expert_of(v, vinfo), 0, w_col(v, j, vinfo))),
            pl.BlockSpec((None, D_MODEL, tf),
                         lambda v, j, vinfo, gwin: (expert_of(v, vinfo), 0, w_col(v, j, vinfo))),
            pl.BlockSpec((None, tf, D_MODEL),
                         lambda v, j, vinfo, gwin: (expert_of(v, vinfo), w_col(v, j, vinfo), 0)),
        ],
        out_specs=pl.BlockSpec((tm, D_MODEL), lambda v, j, vinfo, gwin: (vinfo[v * VISIT_FIELDS], 0)),
        scratch_shapes=[pltpu.VMEM((tm, D_MODEL), BF16), pltpu.VMEM((tm, 1), F32),
                        pltpu.VMEM((tm, D_MODEL), F32),
                        pltpu.VMEM((D_MODEL, tf), BF16), pltpu.VMEM((D_MODEL, tf), BF16),
                        pltpu.VMEM((tf, D_MODEL), BF16)],
    )
    return pl.pallas_call(
        _moe_ffn_kernel,
        grid_spec=grid_spec,
        out_shape=jax.ShapeDtypeStruct((MOE_ROWS, D_MODEL), BF16),
        compiler_params=_params(2),
        name="channel_mixer_experts",
    )(vinfo, gwin, h, gsplit, meta_chunks, wg, wu, wd)


def _combine_kernel(cstart, row0, xp_ref, xs_ref, meta_ref, y_ref, mod_ref, lng_ref, lnb_ref, op_ref, os_ref, *,
                    n_prompt_tiles):
    c = pl.program_id(0)

    def run(x_ref, o_ref):
        gate = mod_ref[:, 5 * D_MODEL:6 * D_MODEL]
        col = lax.broadcasted_iota(jnp.int32, (MOE_HALF, MOE_WINDOW), 1).astype(F32)
        for half in range(MOE_CHUNK // MOE_HALF):
            rows = slice(half * MOE_HALF, (half + 1) * MOE_HALF)
            meta = meta_ref[rows, :]
            e1, e2, r1, r2 = meta[:, 0:1], meta[:, 1:2], meta[:, 2:3], meta[:, 3:4]
            total = None
            for e in range(N_EXPERTS):
                start = pl.multiple_of(cstart[(c * 2 + half) * N_EXPERTS + e], MOE_HALF)
                rank = jnp.where(e1 == float(e), r1, jnp.where(e2 == float(e), r2, NO_ROW))
                onehot = jnp.where(col == rank + (row0[e] - start).astype(F32), 1.0, 0.0).astype(BF16)
                part = jnp.dot(onehot, y_ref[pl.ds(start, MOE_WINDOW), :], preferred_element_type=F32)
                total = part if total is None else total + part
            y = DEEPNORM_ALPHA * x_ref[rows, :] + gate * total
            o_ref[rows, :] = _layer_norm(y, lng_ref[...], lnb_ref[...])

    @pl.when(c < n_prompt_tiles)
    def _():
        run(xp_ref, op_ref)

    @pl.when(c >= n_prompt_tiles)
    def _():
        run(xs_ref, os_ref)


def _combine_call(cstart, row0, xp, xs, meta, y, mod3, mod_row_fn, ln_g, ln_b):
    tm = MOE_CHUNK
    npt = N_PROMPT // tm
    grid_spec = pltpu.PrefetchScalarGridSpec(
        num_scalar_prefetch=2,
        grid=(MOE_NCHUNK,),
        in_specs=[
            pl.BlockSpec((tm, D_MODEL), lambda c, a, b: (jnp.minimum(c, npt - 1), 0)),
            pl.BlockSpec((tm, D_MODEL), lambda c, a, b: (jnp.maximum(c - npt, 0), 0)),
            pl.BlockSpec((tm, LANES), lambda c, a, b: (c, 0)),
            pl.BlockSpec((MOE_ROWS, D_MODEL), lambda c, a, b: (0, 0), pipeline_mode=pl.Buffered(1)),
            pl.BlockSpec((None, 1, N_MOD), lambda c, a, b: (mod_row_fn(c), 0, 0)),
            pl.BlockSpec((1, D_MODEL), lambda c, a, b: (0, 0)),
            pl.BlockSpec((1, D_MODEL), lambda c, a, b: (0, 0)),
        ],
        out_specs=[
            pl.BlockSpec((tm, D_MODEL), lambda c, a, b: (jnp.minimum(c, npt - 1), 0)),
            pl.BlockSpec((tm, D_MODEL), lambda c, a, b: (jnp.maximum(c - npt, 0), 0)),
        ],
    )
    return pl.pallas_call(
        functools.partial(_combine_kernel, n_prompt_tiles=npt),
        grid_spec=grid_spec,
        out_shape=[jax.ShapeDtypeStruct((N_PROMPT, D_MODEL), F32),
                   jax.ShapeDtypeStruct((N_SAMPLE, D_MODEL), F32)],
        compiler_params=_params(1),
        name="expert_combine",
    )(cstart, row0, xp, xs, meta, y, mod3, ln_g, ln_b)


def _moe_plan(cum, total):
    i32 = jnp.int32
    cnt = total[0, :N_EXPERTS].astype(i32)
    off = jnp.cumsum(cnt) - cnt
    cumh = jnp.concatenate([cum[:, 0, :N_EXPERTS], total[0:1, :N_EXPERTS]], axis=0).astype(i32)
    cumc = cumh[::MOE_CHUNK // MOE_HALF]

    n_tiles = MOE_ROWS // MOE_TM
    t0 = jnp.arange(n_tiles, dtype=i32)[:, None] * MOE_TM
    lo = jnp.maximum(t0, off[None, :]).reshape(-1)
    hi = jnp.minimum(t0 + MOE_TM, (off + cnt)[None, :]).reshape(-1)
    ok = hi > lo
    n_visits = jnp.sum(ok.astype(i32))
    order = jnp.argsort(jnp.logical_not(ok), stable=True)[:MOE_MAX_VISITS].astype(i32)
    slot = jnp.arange(MOE_MAX_VISITS, dtype=i32)
    valid = slot < n_visits
    order = order[jnp.minimum(slot, n_visits - 1)]
    vt, ve = order // N_EXPERTS, order % N_EXPERTS
    vlo = jnp.where(valid, lo[order] - vt * MOE_TM, 0)
    vhi = jnp.where(valid, hi[order] - vt * MOE_TM, 0)
    zero = jnp.zeros_like(vt)
    vinfo = jnp.stack([vt, ve, valid.astype(i32), vlo, vhi, off[ve], zero, zero], axis=1).reshape(-1).astype(i32)

    s0 = jnp.arange(MOE_SUBS, dtype=i32)[None, :] * MOE_CHUNK
    rlo = jnp.maximum(vlo[:, None], s0)
    rhi = jnp.minimum(vhi[:, None], s0 + MOE_CHUNK)
    to_rank = (vt * MOE_TM - off[ve])[:, None]
    cum_v = cumc[:, ve]
    c_lo = jnp.sum((cum_v[:, :, None] <= (rlo + to_rank)[None]).astype(i32), axis=0) - 1
    c_hi = jnp.sum((cum_v[:, :, None] < (rhi + to_rank)[None]).astype(i32), axis=0) - 1
    c_lo = jnp.clip(c_lo, 0, MOE_NCHUNK - 1)
    c_hi = jnp.clip(c_hi, 0, MOE_NCHUNK - 1)
    n_win = jnp.where(rhi <= rlo, 0, (c_hi - c_lo) // MOE_GATHER_CHUNKS + 1)
    gwin = jnp.stack([c_lo, n_win], axis=-1).reshape(-1).astype(i32)

    seg_lo = off[None, :] + cumh[:-1]
    cstart = jnp.clip((seg_lo // MOE_HALF) * MOE_HALF, 0, MOE_ROWS - MOE_WINDOW)
    return vinfo, gwin, cstart.reshape(-1).astype(i32), off.astype(i32)


def _rope_tables():
    rows = DEC_SEQ // GRID_W
    row = jnp.repeat(jnp.arange(rows, dtype=F32), GRID_W)
    col = jnp.tile(jnp.arange(GRID_W, dtype=F32), rows)
    n_freq = HEAD_DIM // 4
    inv = ROPE_THETA ** (-jnp.arange(n_freq, dtype=F32) / n_freq)
    ar = row[:, None] * inv
    ac = col[:, None] * inv
    ang = jnp.concatenate([ar, ar, ac, ac], axis=-1)
    cos = jnp.tile(jnp.cos(ang), (1, LANES // HEAD_DIM))
    sin = jnp.tile(jnp.sin(ang), (1, LANES // HEAD_DIM))
    first_half = (jnp.arange(LANES) % (2 * n_freq)) < n_freq
    sin_next = jnp.where(first_half, -sin, 0.0)
    sin_prev = jnp.where(first_half, 0.0, sin)
    return cos, sin_next, sin_prev


def kernel(x_prompt, x_sample, cache_a_k, cache_a_v, cache_b_k, cache_b_v, c, c_ctx, w_ada, b_ada, w_in,
           w_out, diff_lambda, diff_subln, qk_norm_gain, ln_gain, ln_bias, w_ffn_gate, w_ffn_up,
           w_ffn_down, w_router, w_moe_gate, w_moe_up, w_moe_down):
    cond = jnp.zeros((MOD_ROWS, D_MODEL), F32).at[0].set(c_ctx).at[1:1 + DEC_BATCH].set(c)
    mod3 = _ada_call(cond, w_ada, b_ada).reshape(DEPTH * MOD_ROWS, 1, N_MOD)
    rope_tabs = _rope_tables()
    cache = (cache_a_k.reshape(DEC_BATCH, DEPTH, PAST_LEN, DIFF_WIDTH),
             cache_a_v.reshape(DEC_BATCH, DEPTH, PAST_LEN, DIFF_WIDTH),
             cache_b_k.reshape(DEC_BATCH, DEPTH, PAST_LEN, KV_WIDTH),
             cache_b_v.reshape(DEC_BATCH, DEPTH, PAST_LEN, KV_WIDTH))

    xp = x_prompt.reshape(N_PROMPT, D_MODEL)
    xs = x_sample.reshape(N_SAMPLE, D_MODEL)
    xs_off = 0
    new_caches = []
    for l in range(DEPTH):
        base = l * MOD_ROWS
        gq = jnp.tile(qk_norm_gain[l, 0], LANES // HEAD_DIM).reshape(1, LANES)
        gk = jnp.tile(qk_norm_gain[l, 1], LANES // HEAD_DIM).reshape(1, LANES)
        subln = diff_subln[l].reshape(1, LANES)
        ln1_g, ln1_b = ln_gain[l, 0].reshape(1, D_MODEL), ln_bias[l, 0].reshape(1, D_MODEL)
        ln2_g, ln2_b = ln_gain[l, 1].reshape(1, D_MODEL), ln_bias[l, 1].reshape(1, D_MODEL)

        q_p, kv_p, ak, av, bk, bv = _inproj_call(
            xp, 0, N_PROMPT, mod3, lambda i: base, w_in, gq, gk, l, None, True)
        new_caches.append((ak, av, bk, bv))
        x1_p = _attn_call(q_p, kv_p, None, xp, 0, mod3, lambda b: base, w_out, diff_lambda, subln,
                          ln1_g, ln1_b, l, BATCH, SEQ)
        s_tiles = DEC_SEQ // PROJ_TM
        q_s, kv_s = _inproj_call(
            xs, xs_off, N_SAMPLE, mod3, lambda i: base + 1 + i // s_tiles, w_in, gq, gk, l, rope_tabs, False)
        x1_s = _attn_call(q_s, kv_s, cache, xs, xs_off, mod3, lambda b: base + 1 + b, w_out, diff_lambda,
                          subln, ln1_g, ln1_b, l, DEC_BATCH, DEC_SEQ)

        npt = N_PROMPT // FFN_TM
        s_ffn_tiles = DEC_SEQ // FFN_TM
        mod_row = lambda i: base + jnp.where(i < npt, 0, 1 + (i - npt) // s_ffn_tiles)
        k = l // 2
        if l % 2 == 0:
            x2 = _ffn_call(x1_p, x1_s, mod3, mod_row, w_ffn_gate, w_ffn_up, w_ffn_down, k, ln2_g, ln2_b)
        else:
            wr = jnp.zeros((D_MODEL, LANES), F32).at[:, :N_EXPERTS].set(w_router[k])
            h2, gsplit, meta, meta_t, cum, total = _router_call(x1_p, x1_s, mod3, mod_row, wr)
            vinfo, gwin, cstart, row0 = _moe_plan(cum, total)
            meta_chunks = meta_t.reshape(8, MOE_NCHUNK, MOE_CHUNK).transpose(1, 0, 2)
            y = _moe_ffn_call(vinfo, gwin, h2, gsplit, meta_chunks, w_moe_gate[k], w_moe_up[k], w_moe_down[k])
            npc = N_PROMPT // MOE_CHUNK
            s_chunks = DEC_SEQ // MOE_CHUNK
            mod_row_c = lambda c: base + jnp.where(c < npc, 0, 1 + (c - npc) // s_chunks)
            xp, xs = _combine_call(cstart, row0, x1_p, x1_s, meta, y, mod3, mod_row_c, ln2_g, ln2_b)
            xs_off = 0
            continue
        xp, xs, xs_off = x2, x2, N_PROMPT

    y_prompt = xp[:N_PROMPT].reshape(BATCH, SEQ, D_MODEL)
    y_sample = xs[xs_off:xs_off + N_SAMPLE].reshape(DEC_BATCH, DEC_SEQ, D_MODEL)
    stack = lambda idx, shape: jnp.stack([nc[idx] for nc in new_caches], axis=0).reshape(
        (DEPTH, BATCH, SEQ) + shape).transpose((1, 0, 2) + tuple(range(3, 3 + len(shape))))
    return (y_prompt, y_sample,
            stack(0, (DIFF_HEADS, 2, HEAD_DIM)), stack(1, (DIFF_HEADS, 2 * HEAD_DIM)),
            stack(2, (GQA_KV_HEADS, HEAD_DIM)), stack(3, (GQA_KV_HEADS, HEAD_DIM)))
```

```python
import functools
import math

import jax
import jax.numpy as jnp
from jax import lax
from jax.experimental import pallas as pl
from jax.experimental.pallas import tpu as pltpu

D_MODEL = 1024
BATCH = 16
SEQ = 256
DEPTH = 2
DEC_BATCH = 2
DEC_SEQ = 2048
PAST_LEN = 256
GRID_W = 64
HEAD_DIM = 64
DIFF_HEADS = 4
GQA_HEADS = 8
GQA_KV_HEADS = 2
DIFF_WIDTH = DIFF_HEADS * 2 * HEAD_DIM
GQA_WIDTH = GQA_HEADS * HEAD_DIM
KV_WIDTH = GQA_KV_HEADS * HEAD_DIM
IN_COLS = 3 * DIFF_WIDTH + GQA_WIDTH + 2 * KV_WIDTH
D_FF = 2816
N_EXPERTS = 8
ROPE_THETA = 10000.0
EPS = 1e-6
DEEPNORM_ALPHA = (2 * DEPTH) ** 0.25

N_PROMPT = BATCH * SEQ
N_SAMPLE = DEC_BATCH * DEC_SEQ
N_TOK = N_PROMPT + N_SAMPLE
N_MOD = 6 * D_MODEL
MOD_ROWS = 8

LANES = 128
VMEM_LIMIT = 56 * 1024 * 1024

ADA_TN = 1536
PROJ_TM = 512
ATT_TQ = 256
FFN_TM = 1024
FFN_TF = 256
MOE_TM = 1024
MOE_CHUNK = 256
MOE_HALF = MOE_CHUNK // 2
MOE_WINDOW = 2 * MOE_HALF
MOE_GATHER_CHUNKS = 6
MOE_ROWS = 2 * N_TOK
MOE_NCHUNK = N_TOK // MOE_CHUNK
MOE_SUBS = MOE_TM // MOE_CHUNK
MOE_MAX_VISITS = MOE_ROWS // MOE_TM + N_EXPERTS - 1

KV_DIFF_STRIDE = 3 * LANES
KV_GQA_OFF = DIFF_HEADS * KV_DIFF_STRIDE
KV_PREP_COLS = KV_GQA_OFF + 4 * LANES
KV_NEW_COLS = 2 * DIFF_WIDTH + 4 * KV_WIDTH

F32 = jnp.float32
BF16 = jnp.bfloat16


def _params(n_axes):
    return pltpu.CompilerParams(dimension_semantics=("arbitrary",) * n_axes,
                                vmem_limit_bytes=VMEM_LIMIT)


def _layer_norm(y, g, b):
    mu = jnp.mean(y, axis=-1, keepdims=True)
    yc = y - mu
    var = jnp.mean(yc * yc, axis=-1, keepdims=True)
    return yc * lax.rsqrt(var + EPS) * g + b


def _silu(x):
    return x * jax.nn.sigmoid(x)


def _half_masks():
    lane = lax.broadcasted_iota(jnp.int32, (1, LANES), 1)
    lo = (lane < HEAD_DIM).astype(F32)
    return lo, 1.0 - lo


def _ada_kernel(cond_ref, w_ref, b_ref, o_ref):
    s = _silu(cond_ref[...])
    o_ref[...] = jnp.dot(s, w_ref[...], precision=lax.Precision.HIGHEST,
                         preferred_element_type=F32) + b_ref[...]


def _ada_call(cond, w_ada, b_ada):
    return pl.pallas_call(
        _ada_kernel,
        grid=(DEPTH, N_MOD // ADA_TN),
        in_specs=[
            pl.BlockSpec((MOD_ROWS, D_MODEL), lambda l, n: (0, 0)),
            pl.BlockSpec((None, D_MODEL, ADA_TN), lambda l, n: (l, 0, n)),
            pl.BlockSpec((None, 1, ADA_TN), lambda l, n: (l, 0, n)),
        ],
        out_specs=pl.BlockSpec((None, MOD_ROWS, ADA_TN), lambda l, n: (l, 0, n)),
        out_shape=jax.ShapeDtypeStruct((DEPTH, MOD_ROWS, N_MOD), F32),
        compiler_params=_params(2),
        name="ada_modulation",
    )(cond, w_ada, b_ada.reshape(DEPTH, 1, N_MOD))


def _head_sumsq(x):
    r = lax.broadcasted_iota(jnp.int32, (LANES, LANES), 0) // HEAD_DIM
    c = lax.broadcasted_iota(jnp.int32, (LANES, LANES), 1) // HEAD_DIM
    ones = (r == c).astype(BF16)
    sq = x * x
    hi = sq.astype(BF16)
    lo = (sq - hi.astype(F32)).astype(BF16)
    return (jnp.dot(hi, ones, preferred_element_type=F32)
            + jnp.dot(lo, ones, preferred_element_type=F32))


def _inproj_kernel(*refs, rope, caches):
    x_ref, mod_ref, w_ref, gq_ref, gk_ref = refs[:5]
    pos = 5
    if rope:
        cos_ref, sa_ref, sb_ref = refs[pos:pos + 3]
        pos += 3
    q_out, kv_out = refs[pos:pos + 2]
    pos += 2
    if caches:
        ak_out, av_out, bk_out, bv_out = refs[pos:pos + 4]
        pos += 4
    wbf = refs[pos]

    @pl.when(pl.program_id(0) == 0)
    def _():
        wbf[...] = w_ref[...].astype(BF16)

    shift = mod_ref[:, 0:D_MODEL]
    scale = mod_ref[:, D_MODEL:2 * D_MODEL]
    h = (x_ref[...] * (1.0 + scale) + shift).astype(BF16)
    proj = jnp.dot(h, wbf[...], preferred_element_type=F32)

    def group(base, g):
        return proj[:, base + g * LANES: base + (g + 1) * LANES]

    def rot(v):
        if not rope:
            return v
        return (v * cos_ref[...] + pltpu.roll(v, LANES - HEAD_DIM // 4, axis=1) * sa_ref[...]
                + pltpu.roll(v, HEAD_DIM // 4, axis=1) * sb_ref[...])

    def normed(v, gain):
        return v * lax.rsqrt(_head_sumsq(v) * (1.0 / HEAD_DIM) + EPS) * gain

    qk_scale = HEAD_DIM ** -0.5
    off_ak, off_av, off_bq = DIFF_WIDTH, 2 * DIFF_WIDTH, 3 * DIFF_WIDTH
    off_bk, off_bv = off_bq + GQA_WIDTH, off_bq + GQA_WIDTH + KV_WIDTH

    for g in range(DIFF_WIDTH // LANES):
        q_out[:, g * LANES:(g + 1) * LANES] = (rot(group(0, g)) * qk_scale).astype(BF16)
        a_k = group(off_ak, g)
        a_v = group(off_av, g)
        if caches:
            ak_out[:, g * LANES:(g + 1) * LANES] = a_k
            av_out[:, g * LANES:(g + 1) * LANES] = a_v
        kv_out[:, g * LANES:(g + 1) * LANES] = rot(a_k).astype(BF16)
        kv_out[:, DIFF_WIDTH + g * LANES: DIFF_WIDTH + (g + 1) * LANES] = a_v.astype(BF16)
    for g in range(GQA_WIDTH // LANES):
        b_q = rot(normed(group(off_bq, g), gq_ref[...]))
        q_out[:, DIFF_WIDTH + g * LANES: DIFF_WIDTH + (g + 1) * LANES] = (b_q * qk_scale).astype(BF16)
    b_k = normed(group(off_bk, 0), gk_ref[...])
    b_v = group(off_bv, 0)
    if caches:
        bk_out[...] = b_k
        bv_out[...] = b_v
    b_k = rot(b_k)
    base = 2 * DIFF_WIDTH
    kv_out[:, base:base + LANES] = b_k.astype(BF16)
    kv_out[:, base + LANES:base + 2 * LANES] = b_v.astype(BF16)
    kv_out[:, base + 2 * LANES:base + 3 * LANES] = pltpu.roll(b_k, HEAD_DIM, axis=1).astype(BF16)
    kv_out[:, base + 3 * LANES:base + 4 * LANES] = pltpu.roll(b_v, HEAD_DIM, axis=1).astype(BF16)


def _inproj_call(x, row_off, n_rows, mod3, mod_row_fn, w_in, gq, gk, layer, rope_tabs, caches):
    tm = PROJ_TM
    n_tiles = n_rows // tm
    blk_off = row_off // tm
    rope = rope_tabs is not None
    in_specs = [
        pl.BlockSpec((tm, D_MODEL), lambda i: (i + blk_off, 0)),
        pl.BlockSpec((None, 1, N_MOD), lambda i: (mod_row_fn(i), 0, 0)),
        pl.BlockSpec((None, D_MODEL, IN_COLS), lambda i: (layer, 0, 0)),
        pl.BlockSpec((1, LANES), lambda i: (0, 0)),
        pl.BlockSpec((1, LANES), lambda i: (0, 0)),
    ]
    args = [x, mod3, w_in, gq, gk]
    if rope:
        pos_tiles = DEC_SEQ // tm
        for t in rope_tabs:
            in_specs.append(pl.BlockSpec((tm, LANES), lambda i: (i % pos_tiles, 0)))
            args.append(t)
    out_shape = [jax.ShapeDtypeStruct((n_rows, 2 * DIFF_WIDTH), BF16),
                 jax.ShapeDtypeStruct((n_rows, KV_NEW_COLS), BF16)]
    out_specs = [pl.BlockSpec((tm, 2 * DIFF_WIDTH), lambda i: (i, 0)),
                 pl.BlockSpec((tm, KV_NEW_COLS), lambda i: (i, 0))]
    if caches:
        for width in (DIFF_WIDTH, DIFF_WIDTH, KV_WIDTH, KV_WIDTH):
            out_shape.append(jax.ShapeDtypeStruct((n_rows, width), F32))
            out_specs.append(pl.BlockSpec((tm, width), lambda i: (i, 0)))
    return pl.pallas_call(
        functools.partial(_inproj_kernel, rope=rope, caches=caches),
        grid=(n_tiles,),
        in_specs=in_specs,
        out_specs=out_specs,
        out_shape=out_shape,
        scratch_shapes=[pltpu.VMEM((D_MODEL, IN_COLS), BF16)],
        compiler_params=_params(1),
        name="in_projection_rope" if rope else "in_projection_ctx",
    )(*args)


def _attn_kernel(*refs, n_new, n_cache, lam_init):
    q_ref, kv_ref = refs[:2]
    pos = 2
    if n_cache:
        cak_ref, cav_ref, cbk_ref, cbv_ref = refs[pos:pos + 4]
        pos += 4
    x_ref, mod_ref, wout_ref, lam_ref, subln_ref, lng_ref, lnb_ref, o_ref = refs[pos:pos + 8]
    kvs, wbf, oscr = refs[pos + 8:pos + 11]

    b = pl.program_id(0)
    qi = pl.program_id(1)
    lo_f, hi_f = _half_masks()
    lo_b, hi_b = lo_f.astype(BF16), hi_f.astype(BF16)

    @pl.when((b == 0) & (qi == 0))
    def _():
        wbf[...] = wout_ref[...].astype(BF16)

    @pl.when(qi == 0)
    def _():
        for h in range(DIFF_HEADS):
            k = kv_ref[:, h * LANES:(h + 1) * LANES]
            c0 = h * KV_DIFF_STRIDE
            kvs[0:n_new, c0:c0 + LANES] = k * lo_b
            kvs[0:n_new, c0 + LANES:c0 + 2 * LANES] = k * hi_b
            kvs[0:n_new, c0 + 2 * LANES:c0 + 3 * LANES] = kv_ref[:, DIFF_WIDTH + h * LANES:
                                                                 DIFF_WIDTH + (h + 1) * LANES]
            if n_cache:
                kc = cak_ref[:, h * LANES:(h + 1) * LANES]
                kvs[n_new:n_new + n_cache, c0:c0 + LANES] = (kc * lo_f).astype(BF16)
                kvs[n_new:n_new + n_cache, c0 + LANES:c0 + 2 * LANES] = (kc * hi_f).astype(BF16)
                kvs[n_new:n_new + n_cache, c0 + 2 * LANES:c0 + 3 * LANES] = (
                    cav_ref[:, h * LANES:(h + 1) * LANES].astype(BF16))
        kvs[0:n_new, KV_GQA_OFF:KV_GQA_OFF + 4 * LANES] = kv_ref[:, 2 * DIFF_WIDTH:2 * DIFF_WIDTH + 4 * LANES]
        if n_cache:
            ck = cbk_ref[...]
            cv = cbv_ref[...]
            rows = slice(n_new, n_new + n_cache)
            kvs[rows, KV_GQA_OFF:KV_GQA_OFF + LANES] = ck.astype(BF16)
            kvs[rows, KV_GQA_OFF + LANES:KV_GQA_OFF + 2 * LANES] = cv.astype(BF16)
            kvs[rows, KV_GQA_OFF + 2 * LANES:KV_GQA_OFF + 3 * LANES] = pltpu.roll(ck, HEAD_DIM, axis=1).astype(BF16)
            kvs[rows, KV_GQA_OFF + 3 * LANES:KV_GQA_OFF + 4 * LANES] = pltpu.roll(cv, HEAD_DIM, axis=1).astype(BF16)

    lp = lam_ref[...]
    lam = (jnp.exp(jnp.sum(lp[0:1] * lp[1:2], axis=-1, keepdims=True))
           - jnp.exp(jnp.sum(lp[2:3] * lp[3:4], axis=-1, keepdims=True)) + lam_init)

    def scores(q, k):
        return lax.dot_general(q, k, (((1,), (1,)), ((), ())), preferred_element_type=F32)

    def softmax_parts(s):
        e = jnp.exp(s - jnp.max(s, axis=-1, keepdims=True))
        return e, 1.0 / jnp.sum(e, axis=-1, keepdims=True)

    for h in range(DIFF_HEADS):
        c0 = h * KV_DIFF_STRIDE
        q = q_ref[:, h * LANES:(h + 1) * LANES]
        e1, r1 = softmax_parts(scores(q, kvs[:, c0:c0 + LANES]))
        e2, r2 = softmax_parts(scores(q, kvs[:, c0 + LANES:c0 + 2 * LANES]))
        a = (e1 * r1 - e2 * (lam * r2)).astype(BF16)
        o = jnp.dot(a, kvs[:, c0 + 2 * LANES:c0 + 3 * LANES], preferred_element_type=F32)
        o = o * lax.rsqrt(jnp.mean(o * o, axis=-1, keepdims=True) + EPS) * subln_ref[...]
        oscr[:, h * LANES:(h + 1) * LANES] = (o * (1.0 - lam_init)).astype(BF16)

    for pair in range(GQA_HEADS // 2):
        q_pair = q_ref[:, DIFF_WIDTH + pair * LANES: DIFF_WIDTH + (pair + 1) * LANES]
        halves = []
        for c in range(2):
            kv_head = (2 * pair + c) // (GQA_HEADS // GQA_KV_HEADS)
            koff = KV_GQA_OFF if kv_head == c else KV_GQA_OFF + 2 * LANES
            e, r = softmax_parts(scores(q_pair * (lo_b if c == 0 else hi_b), kvs[:, koff:koff + LANES]))
            o = jnp.dot(e.astype(BF16), kvs[:, koff + LANES:koff + 2 * LANES], preferred_element_type=F32)
            halves.append(o * r)
        o_pair = halves[0] * lo_f + halves[1] * hi_f
        oscr[:, DIFF_WIDTH + pair * LANES: DIFF_WIDTH + (pair + 1) * LANES] = o_pair.astype(BF16)

    mix = jnp.dot(oscr[...], wbf[...], preferred_element_type=F32)
    gate = mod_ref[:, 2 * D_MODEL:3 * D_MODEL]
    y = DEEPNORM_ALPHA * x_ref[...] + gate * mix
    o_ref[...] = _layer_norm(y, lng_ref[...], lnb_ref[...])


def _attn_call(q, kv, cache, x, x_row_off, mod3, mod_row_fn, w_out, diff_lambda, subln, ln_g, ln_b,
               layer, n_batch, n_new):
    tq = ATT_TQ
    nq = n_new // tq
    x_blk_off = x_row_off // tq
    n_cache = PAST_LEN if cache is not None else 0
    lam_init = 0.8 - 0.6 * math.exp(-0.3 * layer)
    in_specs = [
        pl.BlockSpec((tq, 2 * DIFF_WIDTH), lambda b, i: (b * nq + i, 0)),
        pl.BlockSpec((n_new, KV_NEW_COLS), lambda b, i: (b, 0), pipeline_mode=pl.Buffered(1)),
    ]
    args = [q, kv]
    if cache is not None:
        for arr in cache:
            width = arr.shape[-1]
            in_specs.append(pl.BlockSpec((None, None, PAST_LEN, width), lambda b, i: (b, layer, 0, 0)))
            args.append(arr)
    in_specs += [
        pl.BlockSpec((tq, D_MODEL), lambda b, i: (b * nq + i + x_blk_off, 0)),
        pl.BlockSpec((None, 1, N_MOD), lambda b, i: (mod_row_fn(b), 0, 0)),
        pl.BlockSpec((None, D_MODEL, D_MODEL), lambda b, i: (layer, 0, 0), pipeline_mode=pl.Buffered(1)),
        pl.BlockSpec((None, 4, HEAD_DIM), lambda b, i: (layer, 0, 0)),
        pl.BlockSpec((1, LANES), lambda b, i: (0, 0)),
        pl.BlockSpec((1, D_MODEL), lambda b, i: (0, 0)),
        pl.BlockSpec((1, D_MODEL), lambda b, i: (0, 0)),
    ]
    args += [x, mod3, w_out, diff_lambda, subln, ln_g, ln_b]
    return pl.pallas_call(
        functools.partial(_attn_kernel, n_new=n_new, n_cache=n_cache, lam_init=lam_init),
        grid=(n_batch, nq),
        in_specs=in_specs,
        out_specs=pl.BlockSpec((tq, D_MODEL), lambda b, i: (b * nq + i, 0)),
        out_shape=jax.ShapeDtypeStruct((n_batch * n_new, D_MODEL), F32),
        scratch_shapes=[pltpu.VMEM((n_new + n_cache, KV_PREP_COLS), BF16),
                        pltpu.VMEM((D_MODEL, D_MODEL), BF16),
                        pltpu.VMEM((tq, D_MODEL), BF16)],
        compiler_params=_params(2),
        name="token_mixer_latent" if cache is not None else "token_mixer_ctx",
    )(*args)


def _router_kernel(xp_ref, xs_ref, mod_ref, wr_ref, h_ref, gsplit_ref, meta_ref, meta_t_ref, cum_ref,
                   total_ref, carry, *, n_prompt_tiles):
    i = pl.program_id(0)

    @pl.when(i == 0)
    def _():
        carry[...] = jnp.zeros_like(carry)

    def run(x_ref):
        tm = x_ref.shape[0]
        shift = mod_ref[:, 3 * D_MODEL:4 * D_MODEL]
        scale = mod_ref[:, 4 * D_MODEL:5 * D_MODEL]
        h = x_ref[...] * (1.0 + scale) + shift
        h_ref[...] = h.astype(BF16)
        logits = jnp.dot(h, wr_ref[...], precision=lax.Precision.HIGHEST, preferred_element_type=F32)
        lane = lax.broadcasted_iota(jnp.int32, logits.shape, 1).astype(F32)
        neg = jnp.float32(-jnp.inf)
        logits = jnp.where(lane < N_EXPERTS, logits, neg)
        m1 = jnp.max(logits, axis=-1, keepdims=True)
        i1 = jnp.min(jnp.where(logits == m1, lane, float(LANES)), axis=-1, keepdims=True)
        rest = jnp.where(lane == i1, neg, logits)
        m2 = jnp.max(rest, axis=-1, keepdims=True)
        i2 = jnp.min(jnp.where(rest == m2, lane, float(LANES)), axis=-1, keepdims=True)
        e2 = jnp.exp(m2 - m1)
        p1 = 1.0 / (1.0 + e2)
        p2 = e2 / (1.0 + e2)
        hit1 = lane == i1
        hit2 = lane == i2
        gates = jnp.where(hit1, p1, 0.0) + jnp.where(hit2, p2, 0.0)
        g_hi = gates.astype(BF16)
        gsplit_ref[:, 0:LANES] = g_hi
        gsplit_ref[:, LANES:2 * LANES] = (gates - g_hi.astype(F32)).astype(BF16)

        sel = jnp.where(hit1, 1.0, 0.0) + jnp.where(hit2, 1.0, 0.0)
        r = lax.broadcasted_iota(jnp.int32, (tm, tm), 0)
        c = lax.broadcasted_iota(jnp.int32, (tm, tm), 1)
        before = jnp.where(c < r, 1.0, 0.0).astype(BF16)
        cumx = jnp.dot(before, sel.astype(BF16), preferred_element_type=F32) + carry[0:1, :]
        rank1 = jnp.sum(jnp.where(hit1, cumx, 0.0), axis=-1, keepdims=True)
        rank2 = jnp.sum(jnp.where(hit2, cumx, 0.0), axis=-1, keepdims=True)
        meta = jnp.zeros_like(logits)
        for k, val in enumerate((i1, i2, rank1, rank2)):
            meta = jnp.where(lane == float(k), val, meta)
        meta_ref[...] = meta
        meta_t_ref[...] = jnp.transpose(meta)[0:8, :]
        for k in range(tm // MOE_HALF):
            cum_ref[k] = jnp.broadcast_to(cumx[k * MOE_HALF:k * MOE_HALF + 1, :], (8, LANES))
        new_carry = carry[0:1, :] + jnp.sum(sel, axis=0, keepdims=True)
        carry[...] = jnp.broadcast_to(new_carry, carry.shape)
        total_ref[...] = jnp.broadcast_to(new_carry, total_ref.shape)

    @pl.when(i < n_prompt_tiles)
    def _():
        run(xp_ref)

    @pl.when(i >= n_prompt_tiles)
    def _():
        run(xs_ref)


def _router_call(xp, xs, mod3, mod_row_fn, w_router_pad):
    tm = FFN_TM
    npt = N_PROMPT // tm
    halves = tm // MOE_HALF
    return pl.pallas_call(
        functools.partial(_router_kernel, n_prompt_tiles=npt),
        grid=(N_TOK // tm,),
        in_specs=[
            pl.BlockSpec((tm, D_MODEL), lambda i: (jnp.minimum(i, npt - 1), 0)),
            pl.BlockSpec((tm, D_MODEL), lambda i: (jnp.maximum(i - npt, 0), 0)),
            pl.BlockSpec((None, 1, N_MOD), lambda i: (mod_row_fn(i), 0, 0)),
            pl.BlockSpec((D_MODEL, LANES), lambda i: (0, 0)),
        ],
        out_specs=[
            pl.BlockSpec((tm, D_MODEL), lambda i: (i, 0)),
            pl.BlockSpec((tm, 2 * LANES), lambda i: (i, 0)),
            pl.BlockSpec((tm, LANES), lambda i: (i, 0)),
            pl.BlockSpec((8, tm), lambda i: (0, i)),
            pl.BlockSpec((halves, 8, LANES), lambda i: (i, 0, 0)),
            pl.BlockSpec((8, LANES), lambda i: (0, 0)),
        ],
        out_shape=[
            jax.ShapeDtypeStruct((N_TOK, D_MODEL), BF16),
            jax.ShapeDtypeStruct((N_TOK, 2 * LANES), BF16),
            jax.ShapeDtypeStruct((N_TOK, LANES), F32),
            jax.ShapeDtypeStruct((8, N_TOK), F32),
            jax.ShapeDtypeStruct((N_TOK // MOE_HALF, 8, LANES), F32),
            jax.ShapeDtypeStruct((8, LANES), F32),
        ],
        scratch_shapes=[pltpu.VMEM((8, LANES), F32)],
        compiler_params=_params(1),
        name="router",
    )(xp, xs, mod3, w_router_pad)


def _ffn_kernel(xp_ref, xs_ref, mod_ref, wg_ref, wu_ref, wd_ref, lng_ref, lnb_ref, o_ref, hscr, acc, *,
                n_prompt_tiles):
    i = pl.program_id(0)
    j = pl.program_id(1)
    first = j == 0
    last = j == pl.num_programs(1) - 1

    def modulate(x_ref):
        shift = mod_ref[:, 3 * D_MODEL:4 * D_MODEL]
        scale = mod_ref[:, 4 * D_MODEL:5 * D_MODEL]
        hscr[...] = (x_ref[...] * (1.0 + scale) + shift).astype(BF16)
        acc[...] = jnp.zeros_like(acc)

    @pl.when(first & (i < n_prompt_tiles))
    def _():
        modulate(xp_ref)

    @pl.when(first & (i >= n_prompt_tiles))
    def _():
        modulate(xs_ref)

    h = hscr[...]
    g = jnp.dot(h, wg_ref[...].astype(BF16), preferred_element_type=F32)
    u = jnp.dot(h, wu_ref[...].astype(BF16), preferred_element_type=F32)
    a = _silu(g) * u
    acc[...] += jnp.dot(a.astype(BF16), wd_ref[...].astype(BF16), preferred_element_type=F32)

    def finish(x_ref):
        gate = mod_ref[:, 5 * D_MODEL:6 * D_MODEL]
        y = DEEPNORM_ALPHA * x_ref[...] + gate * acc[...]
        o_ref[...] = _layer_norm(y, lng_ref[...], lnb_ref[...])

    @pl.when(last & (i < n_prompt_tiles))
    def _():
        finish(xp_ref)

    @pl.when(last & (i >= n_prompt_tiles))
    def _():
        finish(xs_ref)


def _ffn_call(xp, xs, mod3, mod_row_fn, wg, wu, wd, layer_idx, ln_g, ln_b):
    tm, tf = FFN_TM, FFN_TF
    npt = N_PROMPT // tm
    return pl.pallas_call(
        functools.partial(_ffn_kernel, n_prompt_tiles=npt),
        grid=(N_TOK // tm, D_FF // tf),
        in_specs=[
            pl.BlockSpec((tm, D_MODEL), lambda i, j: (jnp.minimum(i, npt - 1), 0)),
            pl.BlockSpec((tm, D_MODEL), lambda i, j: (jnp.maximum(i - npt, 0), 0)),
            pl.BlockSpec((None, 1, N_MOD), lambda i, j: (mod_row_fn(i), 0, 0)),
            pl.BlockSpec((None, D_MODEL, tf), lambda i, j: (layer_idx, 0, j)),
            pl.BlockSpec((None, D_MODEL, tf), lambda i, j: (layer_idx, 0, j)),
            pl.BlockSpec((None, tf, D_MODEL), lambda i, j: (layer_idx, j, 0)),
            pl.BlockSpec((1, D_MODEL), lambda i, j: (0, 0)),
            pl.BlockSpec((1, D_MODEL), lambda i, j: (0, 0)),
        ],
        out_specs=pl.BlockSpec((tm, D_MODEL), lambda i, j: (i, 0)),
        out_shape=jax.ShapeDtypeStruct((N_TOK, D_MODEL), F32),
        scratch_shapes=[pltpu.VMEM((tm, D_MODEL), BF16), pltpu.VMEM((tm, D_MODEL), F32)],
        compiler_params=_params(2),
        name="channel_mixer_dense",
    )(xp, xs, mod3, wg, wu, wd, ln_g, ln_b)


VISIT_FIELDS = 8
NO_ROW = -1.0e9


def _moe_ffn_kernel(vinfo, gwin, h_ref, gsplit_ref, mt_ref, wg_ref, wu_ref, wd_ref, y_ref,
                    hs, gate_s, acc, wgb, wub, wdb):
    v = pl.program_id(0)
    j = pl.program_id(1)
    tile = vinfo[v * VISIT_FIELDS + 0]
    expert = vinfo[v * VISIT_FIELDS + 1]
    valid = vinfo[v * VISIT_FIELDS + 2] == 1
    row_lo = vinfo[v * VISIT_FIELDS + 3]
    row_hi = vinfo[v * VISIT_FIELDS + 4]
    expert_row0 = vinfo[v * VISIT_FIELDS + 5]
    n_g = MOE_GATHER_CHUNKS

    def sub_rows(s):
        return slice(s * MOE_CHUNK, (s + 1) * MOE_CHUNK)

    def active(s):
        return (row_lo < (s + 1) * MOE_CHUNK) & (row_hi > s * MOE_CHUNK)

    def owns_start(s):
        return row_lo <= s * MOE_CHUNK

    @pl.when(valid & (j == 0))
    def _():
        row_iota = lax.broadcasted_iota(jnp.int32, (MOE_CHUNK, MOE_CHUNK), 0).astype(F32)
        gate_lane = lax.broadcasted_iota(jnp.int32, (MOE_CHUNK, 2 * LANES), 1) % LANES
        expert_f = expert.astype(F32)
        for s in range(MOE_SUBS):
            rows = sub_rows(s)

            @pl.when(active(s) & owns_start(s))
            def _():
                hs[rows, :] = jnp.zeros((MOE_CHUNK, D_MODEL), BF16)
                gate_s[rows, :] = jnp.zeros((MOE_CHUNK, 1), F32)

            @pl.when(active(s))
            def _():
                acc[rows, :] = jnp.zeros((MOE_CHUNK, D_MODEL), F32)
                first_chunk = gwin[(v * MOE_SUBS + s) * 2]
                rank0 = (tile * MOE_TM + s * MOE_CHUNK - expert_row0).astype(F32)

                def body(w, carry):
                    want = first_chunk + w * n_g
                    cs = jnp.minimum(want, MOE_NCHUNK - n_g)
                    pieces = []
                    for k in range(n_g):
                        mt = mt_ref[cs + k]
                        rank = jnp.where(mt[0:1] == expert_f, mt[2:3],
                                         jnp.where(mt[1:2] == expert_f, mt[3:4], NO_ROW))
                        rank = rank + jnp.where(cs + k >= want, 0.0, NO_ROW)
                        pieces.append(jnp.where(row_iota + rank0 == rank, 1.0, 0.0).astype(BF16))
                    onehot = jnp.concatenate(pieces, axis=1)
                    start = pl.multiple_of(cs * MOE_CHUNK, MOE_CHUNK)
                    part = jnp.dot(onehot, h_ref[pl.ds(start, n_g * MOE_CHUNK), :], preferred_element_type=F32)
                    hs[rows, :] = hs[rows, :] + part.astype(BF16)
                    gpart = jnp.dot(onehot, gsplit_ref[pl.ds(start, n_g * MOE_CHUNK), :],
                                    preferred_element_type=F32)
                    gate_s[rows, :] = gate_s[rows, :] + jnp.sum(
                        jnp.where(gate_lane == expert, gpart, 0.0), axis=-1, keepdims=True)
                    return carry

                lax.fori_loop(0, gwin[(v * MOE_SUBS + s) * 2 + 1], body, 0)

    @pl.when(valid)
    def _():
        wgb[...] = wg_ref[...].astype(BF16)
        wub[...] = wu_ref[...].astype(BF16)
        wdb[...] = wd_ref[...].astype(BF16)
        first_sub = row_lo // MOE_CHUNK
        n_active = (row_hi + MOE_CHUNK - 1) // MOE_CHUNK - first_sub
        for n in range(1, MOE_SUBS + 1):
            @pl.when(n_active == n)
            def _():
                rows = pl.ds(pl.multiple_of(first_sub * MOE_CHUNK, MOE_CHUNK), n * MOE_CHUNK)
                h = hs[rows, :]
                g = jnp.dot(h, wgb[...], preferred_element_type=F32)
                u = jnp.dot(h, wub[...], preferred_element_type=F32)
                a = _silu(g) * u * gate_s[rows, :]
                acc[rows, :] += jnp.dot(a.astype(BF16), wdb[...], preferred_element_type=F32)

    @pl.when(valid & (j == pl.num_programs(1) - 1))
    def _():
        for s in range(MOE_SUBS):
            rows = sub_rows(s)
            row = lax.broadcasted_iota(jnp.int32, (MOE_CHUNK, 1), 0) + s * MOE_CHUNK
            mine = jnp.where(row >= row_lo, 1.0, 0.0) * jnp.where(row < row_hi, 1.0, 0.0)

            @pl.when(active(s) & owns_start(s))
            def _():
                y_ref[rows, :] = (acc[rows, :] * mine).astype(BF16)

            @pl.when(active(s) & jnp.logical_not(owns_start(s)))
            def _():
                y_ref[rows, :] = jnp.where(mine > 0.0, acc[rows, :], y_ref[rows, :].astype(F32)).astype(BF16)


def _moe_ffn_call(vinfo, gwin, h, gsplit, meta_chunks, wg, wu, wd):
    tm, tf = MOE_TM, FFN_TF
    n_j = D_FF // tf

    def expert_of(v, vinfo):
        return vinfo[v * VISIT_FIELDS + 1]

    def w_col(v, j, vinfo):
        return jnp.where(vinfo[v * VISIT_FIELDS + 2] == 1, j, n_j - 1)

    grid_spec = pltpu.PrefetchScalarGridSpec(
        num_scalar_prefetch=2,
        grid=(MOE_MAX_VISITS, n_j),
        in_specs=[
            pl.BlockSpec((N_TOK, D_MODEL), lambda v, j, vinfo, gwin: (0, 0), pipeline_mode=pl.Buffered(1)),
            pl.BlockSpec((N_TOK, 2 * LANES), lambda v, j, vinfo, gwin: (0, 0), pipeline_mode=pl.Buffered(1)),
            pl.BlockSpec((MOE_NCHUNK, 8, MOE_CHUNK), lambda v, j, vinfo, gwin: (0, 0, 0),
                         pipeline_mode=pl.Buffered(1)),
            pl.BlockSpec((None, D_MODEL, tf),
                         lambda v, j, vinfo, gwin: (expert_of(v, vinfo), 0, w_col(v, j, vinfo))),
            pl.BlockSpec((None, D_MODEL, tf),
                         lambda v, j, vinfo, gwin: (expert_of(v, vinfo), 0, w_col(v, j, vinfo))),
            pl.BlockSpec((None, tf, D_MODEL),
                         lambda v, j, vinfo, gwin: (expert_of(v, vinfo), w_col(v, j, vinfo), 0)),
        ],
        out_specs=pl.BlockSpec((tm, D_MODEL), lambda v, j, vinfo, gwin: (vinfo[v * VISIT_FIELDS], 0)),
        scratch_shapes=[pltpu.VMEM((tm, D_MODEL), BF16), pltpu.VMEM((tm, 1), F32),
                        pltpu.VMEM((tm, D_MODEL), F32),
                        pltpu.VMEM((D_MODEL, tf), BF16), pltpu.VMEM((D_MODEL, tf), BF16),
                        pltpu.VMEM((tf, D_MODEL), BF16)],
    )
    return pl.pallas_call(
        _moe_ffn_kernel,
        grid_spec=grid_spec,
        out_shape=jax.ShapeDtypeStruct((MOE_ROWS, D_MODEL), BF16),
        compiler_params=_params(2),
        name="channel_mixer_experts",
    )(vinfo, gwin, h, gsplit, meta_chunks, wg, wu, wd)


def _combine_kernel(cstart, row0, xp_ref, xs_ref, meta_ref, y_ref, mod_ref, lng_ref, lnb_ref, op_ref, os_ref, *,
                    n_prompt_tiles):
    c = pl.program_id(0)

    def run(x_ref, o_ref):
        gate = mod_ref[:, 5 * D_MODEL:6 * D_MODEL]
        col = lax.broadcasted_iota(jnp.int32, (MOE_HALF, MOE_WINDOW), 1).astype(F32)
        for half in range(MOE_CHUNK // MOE_HALF):
            rows = slice(half * MOE_HALF, (half + 1) * MOE_HALF)
            meta = meta_ref[rows, :]
            e1, e2, r1, r2 = meta[:, 0:1], meta[:, 1:2], meta[:, 2:3], meta[:, 3:4]
            total = None
            for e in range(N_EXPERTS):
                start = pl.multiple_of(cstart[(c * 2 + half) * N_EXPERTS + e], MOE_HALF)
                rank = jnp.where(e1 == float(e), r1, jnp.where(e2 == float(e), r2, NO_ROW))
                onehot = jnp.where(col == rank + (row0[e] - start).astype(F32), 1.0, 0.0).astype(BF16)
                part = jnp.dot(onehot, y_ref[pl.ds(start, MOE_WINDOW), :], preferred_element_type=F32)
                total = part if total is None else total + part
            y = DEEPNORM_ALPHA * x_ref[rows, :] + gate * total
            o_ref[rows, :] = _layer_norm(y, lng_ref[...], lnb_ref[...])

    @pl.when(c < n_prompt_tiles)
    def _():
        run(xp_ref, op_ref)

    @pl.when(c >= n_prompt_tiles)
    def _():
        run(xs_ref, os_ref)


def _combine_call(cstart, row0, xp, xs, meta, y, mod3, mod_row_fn, ln_g, ln_b):
    tm = MOE_CHUNK
    npt = N_PROMPT // tm
    grid_spec = pltpu.PrefetchScalarGridSpec(
        num_scalar_prefetch=2,
        grid=(MOE_NCHUNK,),
        in_specs=[
            pl.BlockSpec((tm, D_MODEL), lambda c, a, b: (jnp.minimum(c, npt - 1), 0)),
            pl.BlockSpec((tm, D_MODEL), lambda c, a, b: (jnp.maximum(c - npt, 0), 0)),
            pl.BlockSpec((tm, LANES), lambda c, a, b: (c, 0)),
            pl.BlockSpec((MOE_ROWS, D_MODEL), lambda c, a, b: (0, 0), pipeline_mode=pl.Buffered(1)),
            pl.BlockSpec((None, 1, N_MOD), lambda c, a, b: (mod_row_fn(c), 0, 0)),
            pl.BlockSpec((1, D_MODEL), lambda c, a, b: (0, 0)),
            pl.BlockSpec((1, D_MODEL), lambda c, a, b: (0, 0)),
        ],
        out_specs=[
            pl.BlockSpec((tm, D_MODEL), lambda c, a, b: (jnp.minimum(c, npt - 1), 0)),
            pl.BlockSpec((tm, D_MODEL), lambda c, a, b: (jnp.maximum(c - npt, 0), 0)),
        ],
    )
    return pl.pallas_call(
        functools.partial(_combine_kernel, n_prompt_tiles=npt),
        grid_spec=grid_spec,
        out_shape=[jax.ShapeDtypeStruct((N_PROMPT, D_MODEL), F32),
                   jax.ShapeDtypeStruct((N_SAMPLE, D_MODEL), F32)],
        compiler_params=_params(1),
        name="expert_combine",
    )(cstart, row0, xp, xs, meta, y, mod3, ln_g, ln_b)


def _moe_plan(cum, total):
    i32 = jnp.int32
    cnt = total[0, :N_EXPERTS].astype(i32)
    off = jnp.cumsum(cnt) - cnt
    cumh = jnp.concatenate([cum[:, 0, :N_EXPERTS], total[0:1, :N_EXPERTS]], axis=0).astype(i32)
    cumc = cumh[::MOE_CHUNK // MOE_HALF]

    n_tiles = MOE_ROWS // MOE_TM
    t0 = jnp.arange(n_tiles, dtype=i32)[:, None] * MOE_TM
    lo = jnp.maximum(t0, off[None, :]).reshape(-1)
    hi = jnp.minimum(t0 + MOE_TM, (off + cnt)[None, :]).reshape(-1)
    ok = hi > lo
    n_visits = jnp.sum(ok.astype(i32))
    order = jnp.argsort(jnp.logical_not(ok), stable=True)[:MOE_MAX_VISITS].astype(i32)
    slot = jnp.arange(MOE_MAX_VISITS, dtype=i32)
    valid = slot < n_visits
    order = order[jnp.minimum(slot, n_visits - 1)]
    vt, ve = order // N_EXPERTS, order % N_EXPERTS
    vlo = jnp.where(valid, lo[order] - vt * MOE_TM, 0)
    vhi = jnp.where(valid, hi[order] - vt * MOE_TM, 0)
    zero = jnp.zeros_like(vt)
    vinfo = jnp.stack([vt, ve, valid.astype(i32), vlo, vhi, off[ve], zero, zero], axis=1).reshape(-1).astype(i32)

    s0 = jnp.arange(MOE_SUBS, dtype=i32)[None, :] * MOE_CHUNK
    rlo = jnp.maximum(vlo[:, None], s0)
    rhi = jnp.minimum(vhi[:, None], s0 + MOE_CHUNK)
    to_rank = (vt * MOE_TM - off[ve])[:, None]
    cum_v = cumc[:, ve]
    c_lo = jnp.sum((cum_v[:, :, None] <= (rlo + to_rank)[None]).astype(i32), axis=0) - 1
    c_hi = jnp.sum((cum_v[:, :, None] < (rhi + to_rank)[None]).astype(i32), axis=0) - 1
    c_lo = jnp.clip(c_lo, 0, MOE_NCHUNK - 1)
    c_hi = jnp.clip(c_hi, 0, MOE_NCHUNK - 1)
    n_win = jnp.where(rhi <= rlo, 0, (c_hi - c_lo) // MOE_GATHER_CHUNKS + 1)
    gwin = jnp.stack([c_lo, n_win], axis=-1).reshape(-1).astype(i32)

    seg_lo = off[None, :] + cumh[:-1]
    cstart = jnp.clip((seg_lo // MOE_HALF) * MOE_HALF, 0, MOE_ROWS - MOE_WINDOW)
    return vinfo, gwin, cstart.reshape(-1).astype(i32), off.astype(i32)


def _rope_tables():
    rows = DEC_SEQ // GRID_W
    row = jnp.repeat(jnp.arange(rows, dtype=F32), GRID_W)
    col = jnp.tile(jnp.arange(GRID_W, dtype=F32), rows)
    n_freq = HEAD_DIM // 4
    inv = ROPE_THETA ** (-jnp.arange(n_freq, dtype=F32) / n_freq)
    ar = row[:, None] * inv
    ac = col[:, None] * inv
    ang = jnp.concatenate([ar, ar, ac, ac], axis=-1)
    cos = jnp.tile(jnp.cos(ang), (1, LANES // HEAD_DIM))
    sin = jnp.tile(jnp.sin(ang), (1, LANES // HEAD_DIM))
    first_half = (jnp.arange(LANES) % (2 * n_freq)) < n_freq
    sin_next = jnp.where(first_half, -sin, 0.0)
    sin_prev = jnp.where(first_half, 0.0, sin)
    return cos, sin_next, sin_prev


def kernel(x_prompt, x_sample, cache_a_k, cache_a_v, cache_b_k, cache_b_v, c, c_ctx, w_ada, b_ada, w_in,
           w_out, diff_lambda, diff_subln, qk_norm_gain, ln_gain, ln_bias, w_ffn_gate, w_ffn_up,
           w_ffn_down, w_router, w_moe_gate, w_moe_up, w_moe_down):
    cond = jnp.zeros((MOD_ROWS, D_MODEL), F32).at[0].set(c_ctx).at[1:1 + DEC_BATCH].set(c)
    mod3 = _ada_call(cond, w_ada, b_ada).reshape(DEPTH * MOD_ROWS, 1, N_MOD)
    rope_tabs = _rope_tables()
    cache = (cache_a_k.reshape(DEC_BATCH, DEPTH, PAST_LEN, DIFF_WIDTH),
             cache_a_v.reshape(DEC_BATCH, DEPTH, PAST_LEN, DIFF_WIDTH),
             cache_b_k.reshape(DEC_BATCH, DEPTH, PAST_LEN, KV_WIDTH),
             cache_b_v.reshape(DEC_BATCH, DEPTH, PAST_LEN, KV_WIDTH))

    xp = x_prompt.reshape(N_PROMPT, D_MODEL)
    xs = x_sample.reshape(N_SAMPLE, D_MODEL)
    xs_off = 0
    new_caches = []
    for l in range(DEPTH):
        base = l * MOD_ROWS
        gq = jnp.tile(qk_norm_gain[l, 0], LANES // HEAD_DIM).reshape(1, LANES)
        gk = jnp.tile(qk_norm_gain[l, 1], LANES // HEAD_DIM).reshape(1, LANES)
        subln = diff_subln[l].reshape(1, LANES)
        ln1_g, ln1_b = ln_gain[l, 0].reshape(1, D_MODEL), ln_bias[l, 0].reshape(1, D_MODEL)
        ln2_g, ln2_b = ln_gain[l, 1].reshape(1, D_MODEL), ln_bias[l, 1].reshape(1, D_MODEL)

        q_p, kv_p, ak, av, bk, bv = _inproj_call(
            xp, 0, N_PROMPT, mod3, lambda i: base, w_in, gq, gk, l, None, True)
        new_caches.append((ak, av, bk, bv))
        x1_p = _attn_call(q_p, kv_p, None, xp, 0, mod3, lambda b: base, w_out, diff_lambda, subln,
                          ln1_g, ln1_b, l, BATCH, SEQ)
        s_tiles = DEC_SEQ // PROJ_TM
        q_s, kv_s = _inproj_call(
            xs, xs_off, N_SAMPLE, mod3, lambda i: base + 1 + i // s_tiles, w_in, gq, gk, l, rope_tabs, False)
        x1_s = _attn_call(q_s, kv_s, cache, xs, xs_off, mod3, lambda b: base + 1 + b, w_out, diff_lambda,
                          subln, ln1_g, ln1_b, l, DEC_BATCH, DEC_SEQ)

        npt = N_PROMPT // FFN_TM
        s_ffn_tiles = DEC_SEQ // FFN_TM
        mod_row = lambda i: base + jnp.where(i < npt, 0, 1 + (i - npt) // s_ffn_tiles)
        k = l // 2
        if l % 2 == 0:
            x2 = _ffn_call(x1_p, x1_s, mod3, mod_row, w_ffn_gate, w_ffn_up, w_ffn_down, k, ln2_g, ln2_b)
        else:
            wr = jnp.zeros((D_MODEL, LANES), F32).at[:, :N_EXPERTS].set(w_router[k])
            h2, gsplit, meta, meta_t, cum, total = _router_call(x1_p, x1_s, mod3, mod_row, wr)
            vinfo, gwin, cstart, row0 = _moe_plan(cum, total)
            meta_chunks = meta_t.reshape(8, MOE_NCHUNK, MOE_CHUNK).transpose(1, 0, 2)
            y = _moe_ffn_call(vinfo, gwin, h2, gsplit, meta_chunks, w_moe_gate[k], w_moe_up[k], w_moe_down[k])
            npc = N_PROMPT // MOE_CHUNK
            s_chunks = DEC_SEQ // MOE_CHUNK
            mod_row_c = lambda c: base + jnp.where(c < npc, 0, 1 + (c - npc) // s_chunks)
            xp, xs = _combine_call(cstart, row0, x1_p, x1_s, meta, y, mod3, mod_row_c, ln2_g, ln2_b)
            xs_off = 0
            continue
        xp, xs, xs_off = x2, x2, N_PROMPT

    y_prompt = xp[:N_PROMPT].reshape(BATCH, SEQ, D_MODEL)
    y_sample = xs[xs_off:xs_off + N_SAMPLE].reshape(DEC_BATCH, DEC_SEQ, D_MODEL)
    stack = lambda idx, shape: jnp.stack([nc[idx] for nc in new_caches], axis=0).reshape(
        (DEPTH, BATCH, SEQ) + shape).transpose((1, 0, 2) + tuple(range(3, 3 + len(shape))))
    return (y_prompt, y_sample,
            stack(0, (DIFF_HEADS, 2, HEAD_DIM)), stack(1, (DIFF_HEADS, 2 * HEAD_DIM)),
            stack(2, (GQA_KV_HEADS, HEAD_DIM)), stack(3, (GQA_KV_HEADS, HEAD_DIM)))
```

```python
import functools
import math

import jax
import jax.numpy as jnp
from jax import lax
from jax.experimental import pallas as pl
from jax.experimental.pallas import tpu as pltpu

D_MODEL = 1024
BATCH = 16
SEQ = 256
DEPTH = 2
DEC_BATCH = 2
DEC_SEQ = 2048
PAST_LEN = 256
GRID_W = 64
HEAD_DIM = 64
DIFF_HEADS = 4
GQA_HEADS = 8
GQA_KV_HEADS = 2
DIFF_WIDTH = DIFF_HEADS * 2 * HEAD_DIM
GQA_WIDTH = GQA_HEADS * HEAD_DIM
KV_WIDTH = GQA_KV_HEADS * HEAD_DIM
IN_COLS = 3 * DIFF_WIDTH + GQA_WIDTH + 2 * KV_WIDTH
D_FF = 2816
N_EXPERTS = 8
ROPE_THETA = 10000.0
EPS = 1e-6
DEEPNORM_ALPHA = (2 * DEPTH) ** 0.25

N_PROMPT = BATCH * SEQ
N_SAMPLE = DEC_BATCH * DEC_SEQ
N_TOK = N_PROMPT + N_SAMPLE
N_MOD = 6 * D_MODEL
MOD_ROWS = 8

LANES = 128
VMEM_LIMIT = 56 * 1024 * 1024

ADA_TN = 1536
PROJ_TM = 512
ATT_TQ = 256
FFN_TM = 1024
FFN_TF = 256
MOE_TM = 2048
MOE_CHUNK = 256
MOE_HALF = MOE_CHUNK // 2
MOE_WINDOW = 2 * MOE_HALF
MOE_GATHER_CHUNKS = 6
MOE_ROWS = 2 * N_TOK
MOE_NCHUNK = N_TOK // MOE_CHUNK
MOE_SUBS = MOE_TM // MOE_CHUNK
MOE_MAX_VISITS = MOE_ROWS // MOE_TM + N_EXPERTS - 1

KV_DIFF_STRIDE = 3 * LANES
KV_GQA_OFF = DIFF_HEADS * KV_DIFF_STRIDE
KV_PREP_COLS = KV_GQA_OFF + 4 * LANES
KV_NEW_COLS = 2 * DIFF_WIDTH + 4 * KV_WIDTH

F32 = jnp.float32
BF16 = jnp.bfloat16


def _params(n_axes):
    return pltpu.CompilerParams(dimension_semantics=("arbitrary",) * n_axes,
                                vmem_limit_bytes=VMEM_LIMIT)


def _layer_norm(y, g, b):
    mu = jnp.mean(y, axis=-1, keepdims=True)
    yc = y - mu
    var = jnp.mean(yc * yc, axis=-1, keepdims=True)
    return yc * lax.rsqrt(var + EPS) * g + b


def _silu(x):
    return x * jax.nn.sigmoid(x)


def _half_masks():
    lane = lax.broadcasted_iota(jnp.int32, (1, LANES), 1)
    lo = (lane < HEAD_DIM).astype(F32)
    return lo, 1.0 - lo


def _ada_kernel(cond_ref, w_ref, b_ref, o_ref):
    s = _silu(cond_ref[...])
    o_ref[...] = jnp.dot(s, w_ref[...], precision=lax.Precision.HIGHEST,
                         preferred_element_type=F32) + b_ref[...]


def _ada_call(cond, w_ada, b_ada):
    return pl.pallas_call(
        _ada_kernel,
        grid=(DEPTH, N_MOD // ADA_TN),
        in_specs=[
            pl.BlockSpec((MOD_ROWS, D_MODEL), lambda l, n: (0, 0)),
            pl.BlockSpec((None, D_MODEL, ADA_TN), lambda l, n: (l, 0, n)),
            pl.BlockSpec((None, 1, ADA_TN), lambda l, n: (l, 0, n)),
        ],
        out_specs=pl.BlockSpec((None, MOD_ROWS, ADA_TN), lambda l, n: (l, 0, n)),
        out_shape=jax.ShapeDtypeStruct((DEPTH, MOD_ROWS, N_MOD), F32),
        compiler_params=_params(2),
        name="ada_modulation",
    )(cond, w_ada, b_ada.reshape(DEPTH, 1, N_MOD))


def _head_sumsq(x):
    r = lax.broadcasted_iota(jnp.int32, (LANES, LANES), 0) // HEAD_DIM
    c = lax.broadcasted_iota(jnp.int32, (LANES, LANES), 1) // HEAD_DIM
    ones = (r == c).astype(BF16)
    sq = x * x
    hi = sq.astype(BF16)
    lo = (sq - hi.astype(F32)).astype(BF16)
    return (jnp.dot(hi, ones, preferred_element_type=F32)
            + jnp.dot(lo, ones, preferred_element_type=F32))


def _inproj_kernel(*refs, rope, caches):
    x_ref, mod_ref, w_ref, gq_ref, gk_ref = refs[:5]
    pos = 5
    if rope:
        cos_ref, sa_ref, sb_ref = refs[pos:pos + 3]
        pos += 3
    q_out, kv_out = refs[pos:pos + 2]
    pos += 2
    if caches:
        ak_out, av_out, bk_out, bv_out = refs[pos:pos + 4]
        pos += 4
    wbf = refs[pos]

    @pl.when(pl.program_id(0) == 0)
    def _():
        wbf[...] = w_ref[...].astype(BF16)

    shift = mod_ref[:, 0:D_MODEL]
    scale = mod_ref[:, D_MODEL:2 * D_MODEL]
    h = (x_ref[...] * (1.0 + scale) + shift).astype(BF16)
    proj = jnp.dot(h, wbf[...], preferred_element_type=F32)

    def group(base, g):
        return proj[:, base + g * LANES: base + (g + 1) * LANES]

    def rot(v):
        if not rope:
            return v
        return (v * cos_ref[...] + pltpu.roll(v, LANES - HEAD_DIM // 4, axis=1) * sa_ref[...]
                + pltpu.roll(v, HEAD_DIM // 4, axis=1) * sb_ref[...])

    def normed(v, gain):
        return v * lax.rsqrt(_head_sumsq(v) * (1.0 / HEAD_DIM) + EPS) * gain

    qk_scale = HEAD_DIM ** -0.5
    off_ak, off_av, off_bq = DIFF_WIDTH, 2 * DIFF_WIDTH, 3 * DIFF_WIDTH
    off_bk, off_bv = off_bq + GQA_WIDTH, off_bq + GQA_WIDTH + KV_WIDTH

    for g in range(DIFF_WIDTH // LANES):
        q_out[:, g * LANES:(g + 1) * LANES] = (rot(group(0, g)) * qk_scale).astype(BF16)
        a_k = group(off_ak, g)
        a_v = group(off_av, g)
        if caches:
            ak_out[:, g * LANES:(g + 1) * LANES] = a_k
            av_out[:, g * LANES:(g + 1) * LANES] = a_v
        kv_out[:, g * LANES:(g + 1) * LANES] = rot(a_k).astype(BF16)
        kv_out[:, DIFF_WIDTH + g * LANES: DIFF_WIDTH + (g + 1) * LANES] = a_v.astype(BF16)
    for g in range(GQA_WIDTH // LANES):
        b_q = rot(normed(group(off_bq, g), gq_ref[...]))
        q_out[:, DIFF_WIDTH + g * LANES: DIFF_WIDTH + (g + 1) * LANES] = (b_q * qk_scale).astype(BF16)
    b_k = normed(group(off_bk, 0), gk_ref[...])
    b_v = group(off_bv, 0)
    if caches:
        bk_out[...] = b_k
        bv_out[...] = b_v
    b_k = rot(b_k)
    base = 2 * DIFF_WIDTH
    kv_out[:, base:base + LANES] = b_k.astype(BF16)
    kv_out[:, base + LANES:base + 2 * LANES] = b_v.astype(BF16)
    kv_out[:, base + 2 * LANES:base + 3 * LANES] = pltpu.roll(b_k, HEAD_DIM, axis=1).astype(BF16)
    kv_out[:, base + 3 * LANES:base + 4 * LANES] = pltpu.roll(b_v, HEAD_DIM, axis=1).astype(BF16)


def _inproj_call(x, row_off, n_rows, mod3, mod_row_fn, w_in, gq, gk, layer, rope_tabs, caches):
    tm = PROJ_TM
    n_tiles = n_rows // tm
    blk_off = row_off // tm
    rope = rope_tabs is not None
    in_specs = [
        pl.BlockSpec((tm, D_MODEL), lambda i: (i + blk_off, 0)),
        pl.BlockSpec((None, 1, N_MOD), lambda i: (mod_row_fn(i), 0, 0)),
        pl.BlockSpec((None, D_MODEL, IN_COLS), lambda i: (layer, 0, 0)),
        pl.BlockSpec((1, LANES), lambda i: (0, 0)),
        pl.BlockSpec((1, LANES), lambda i: (0, 0)),
    ]
    args = [x, mod3, w_in, gq, gk]
    if rope:
        pos_tiles = DEC_SEQ // tm
        for t in rope_tabs:
            in_specs.append(pl.BlockSpec((tm, LANES), lambda i: (i % pos_tiles, 0)))
            args.append(t)
    out_shape = [jax.ShapeDtypeStruct((n_rows, 2 * DIFF_WIDTH), BF16),
                 jax.ShapeDtypeStruct((n_rows, KV_NEW_COLS), BF16)]
    out_specs = [pl.BlockSpec((tm, 2 * DIFF_WIDTH), lambda i: (i, 0)),
                 pl.BlockSpec((tm, KV_NEW_COLS), lambda i: (i, 0))]
    if caches:
        for width in (DIFF_WIDTH, DIFF_WIDTH, KV_WIDTH, KV_WIDTH):
            out_shape.append(jax.ShapeDtypeStruct((n_rows, width), F32))
            out_specs.append(pl.BlockSpec((tm, width), lambda i: (i, 0)))
    return pl.pallas_call(
        functools.partial(_inproj_kernel, rope=rope, caches=caches),
        grid=(n_tiles,),
        in_specs=in_specs,
        out_specs=out_specs,
        out_shape=out_shape,
        scratch_shapes=[pltpu.VMEM((D_MODEL, IN_COLS), BF16)],
        compiler_params=_params(1),
        name="in_projection_rope" if rope else "in_projection_ctx",
    )(*args)


def _attn_kernel(*refs, n_new, n_cache, lam_init):
    q_ref, kv_ref = refs[:2]
    pos = 2
    if n_cache:
        cak_ref, cav_ref, cbk_ref, cbv_ref = refs[pos:pos + 4]
        pos += 4
    x_ref, mod_ref, wout_ref, lam_ref, subln_ref, lng_ref, lnb_ref, o_ref = refs[pos:pos + 8]
    kvs, wbf, oscr = refs[pos + 8:pos + 11]

    b = pl.program_id(0)
    qi = pl.program_id(1)
    lo_f, hi_f = _half_masks()
    lo_b, hi_b = lo_f.astype(BF16), hi_f.astype(BF16)

    @pl.when((b == 0) & (qi == 0))
    def _():
        wbf[...] = wout_ref[...].astype(BF16)

    @pl.when(qi == 0)
    def _():
        for h in range(DIFF_HEADS):
            k = kv_ref[:, h * LANES:(h + 1) * LANES]
            c0 = h * KV_DIFF_STRIDE
            kvs[0:n_new, c0:c0 + LANES] = k * lo_b
            kvs[0:n_new, c0 + LANES:c0 + 2 * LANES] = k * hi_b
            kvs[0:n_new, c0 + 2 * LANES:c0 + 3 * LANES] = kv_ref[:, DIFF_WIDTH + h * LANES:
                                                                 DIFF_WIDTH + (h + 1) * LANES]
            if n_cache:
                kc = cak_ref[:, h * LANES:(h + 1) * LANES]
                kvs[n_new:n_new + n_cache, c0:c0 + LANES] = (kc * lo_f).astype(BF16)
                kvs[n_new:n_new + n_cache, c0 + LANES:c0 + 2 * LANES] = (kc * hi_f).astype(BF16)
                kvs[n_new:n_new + n_cache, c0 + 2 * LANES:c0 + 3 * LANES] = (
                    cav_ref[:, h * LANES:(h + 1) * LANES].astype(BF16))
        kvs[0:n_new, KV_GQA_OFF:KV_GQA_OFF + 4 * LANES] = kv_ref[:, 2 * DIFF_WIDTH:2 * DIFF_WIDTH + 4 * LANES]
        if n_cache:
            ck = cbk_ref[...]
            cv = cbv_ref[...]
            rows = slice(n_new, n_new + n_cache)
            kvs[rows, KV_GQA_OFF:KV_GQA_OFF + LANES] = ck.astype(BF16)
            kvs[rows, KV_GQA_OFF + LANES:KV_GQA_OFF + 2 * LANES] = cv.astype(BF16)
            kvs[rows, KV_GQA_OFF + 2 * LANES:KV_GQA_OFF + 3 * LANES] = pltpu.roll(ck, HEAD_DIM, axis=1).astype(BF16)
            kvs[rows, KV_GQA_OFF + 3 * LANES:KV_GQA_OFF + 4 * LANES] = pltpu.roll(cv, HEAD_DIM, axis=1).astype(BF16)

    lp = lam_ref[...]
    lam = (jnp.exp(jnp.sum(lp[0:1] * lp[1:2], axis=-1, keepdims=True))
           - jnp.exp(jnp.sum(lp[2:3] * lp[3:4], axis=-1, keepdims=True)) + lam_init)

    def scores(q, k):
        return lax.dot_general(q, k, (((1,), (1,)), ((), ())), preferred_element_type=F32)

    def softmax_parts(s):
        e = jnp.exp(s - jnp.max(s, axis=-1, keepdims=True))
        return e, 1.0 / jnp.sum(e, axis=-1, keepdims=True)

    for h in range(DIFF_HEADS):
        c0 = h * KV_DIFF_STRIDE
        q = q_ref[:, h * LANES:(h + 1) * LANES]
        e1, r1 = softmax_parts(scores(q, kvs[:, c0:c0 + LANES]))
        e2, r2 = softmax_parts(scores(q, kvs[:, c0 + LANES:c0 + 2 * LANES]))
        a = (e1 * r1 - e2 * (lam * r2)).astype(BF16)
        o = jnp.dot(a, kvs[:, c0 + 2 * LANES:c0 + 3 * LANES], preferred_element_type=F32)
        o = o * lax.rsqrt(jnp.mean(o * o, axis=-1, keepdims=True) + EPS) * subln_ref[...]
        oscr[:, h * LANES:(h + 1) * LANES] = (o * (1.0 - lam_init)).astype(BF16)

    for pair in range(GQA_HEADS // 2):
        q_pair = q_ref[:, DIFF_WIDTH + pair * LANES: DIFF_WIDTH + (pair + 1) * LANES]
        halves = []
        for c in range(2):
            kv_head = (2 * pair + c) // (GQA_HEADS // GQA_KV_HEADS)
            koff = KV_GQA_OFF if kv_head == c else KV_GQA_OFF + 2 * LANES
            e, r = softmax_parts(scores(q_pair * (lo_b if c == 0 else hi_b), kvs[:, koff:koff + LANES]))
            o = jnp.dot(e.astype(BF16), kvs[:, koff + LANES:koff + 2 * LANES], preferred_element_type=F32)
            halves.append(o * r)
        o_pair = halves[0] * lo_f + halves[1] * hi_f
        oscr[:, DIFF_WIDTH + pair * LANES: DIFF_WIDTH + (pair + 1) * LANES] = o_pair.astype(BF16)

    mix = jnp.dot(oscr[...], wbf[...], preferred_element_type=F32)
    gate = mod_ref[:, 2 * D_MODEL:3 * D_MODEL]
    y = DEEPNORM_ALPHA * x_ref[...] + gate * mix
    o_ref[...] = _layer_norm(y, lng_ref[...], lnb_ref[...])


def _attn_call(q, kv, cache, x, x_row_off, mod3, mod_row_fn, w_out, diff_lambda, subln, ln_g, ln_b,
               layer, n_batch, n_new):
    tq = ATT_TQ
    nq = n_new // tq
    x_blk_off = x_row_off // tq
    n_cache = PAST_LEN if cache is not None else 0
    lam_init = 0.8 - 0.6 * math.exp(-0.3 * layer)
    in_specs = [
        pl.BlockSpec((tq, 2 * DIFF_WIDTH), lambda b, i: (b * nq + i, 0)),
        pl.BlockSpec((n_new, KV_NEW_COLS), lambda b, i: (b, 0)),
    ]
    args = [q, kv]
    if cache is not None:
        for arr in cache:
            width = arr.shape[-1]
            in_specs.append(pl.BlockSpec((None, None, PAST_LEN, width), lambda b, i: (b, layer, 0, 0)))
            args.append(arr)
    in_specs += [
        pl.BlockSpec((tq, D_MODEL), lambda b, i: (b * nq + i + x_blk_off, 0)),
        pl.BlockSpec((None, 1, N_MOD), lambda b, i: (mod_row_fn(b), 0, 0)),
        pl.BlockSpec((None, D_MODEL, D_MODEL), lambda b, i: (layer, 0, 0), pipeline_mode=pl.Buffered(1)),
        pl.BlockSpec((None, 4, HEAD_DIM), lambda b, i: (layer, 0, 0)),
        pl.BlockSpec((1, LANES), lambda b, i: (0, 0)),
        pl.BlockSpec((1, D_MODEL), lambda b, i: (0, 0)),
        pl.BlockSpec((1, D_MODEL), lambda b, i: (0, 0)),
    ]
    args += [x, mod3, w_out, diff_lambda, subln, ln_g, ln_b]
    return pl.pallas_call(
        functools.partial(_attn_kernel, n_new=n_new, n_cache=n_cache, lam_init=lam_init),
        grid=(n_batch, nq),
        in_specs=in_specs,
        out_specs=pl.BlockSpec((tq, D_MODEL), lambda b, i: (b * nq + i, 0)),
        out_shape=jax.ShapeDtypeStruct((n_batch * n_new, D_MODEL), F32),
        scratch_shapes=[pltpu.VMEM((n_new + n_cache, KV_PREP_COLS), BF16),
                        pltpu.VMEM((D_MODEL, D_MODEL), BF16),
                        pltpu.VMEM((tq, D_MODEL), BF16)],
        compiler_params=_params(2),
        name="token_mixer_latent" if cache is not None else "token_mixer_ctx",
    )(*args)


def _router_kernel(xp_ref, xs_ref, mod_ref, wr_ref, h_ref, gsplit_ref, meta_ref, meta_t_ref, cum_ref,
                   total_ref, carry, *, n_prompt_tiles):
    i = pl.program_id(0)

    @pl.when(i == 0)
    def _():
        carry[...] = jnp.zeros_like(carry)

    def run(x_ref):
        tm = x_ref.shape[0]
        shift = mod_ref[:, 3 * D_MODEL:4 * D_MODEL]
        scale = mod_ref[:, 4 * D_MODEL:5 * D_MODEL]
        h = x_ref[...] * (1.0 + scale) + shift
        h_ref[...] = h.astype(BF16)
        logits = jnp.dot(h, wr_ref[...], precision=lax.Precision.HIGHEST, preferred_element_type=F32)
        lane = lax.broadcasted_iota(jnp.int32, logits.shape, 1).astype(F32)
        neg = jnp.float32(-jnp.inf)
        logits = jnp.where(lane < N_EXPERTS, logits, neg)
        m1 = jnp.max(logits, axis=-1, keepdims=True)
        i1 = jnp.min(jnp.where(logits == m1, lane, float(LANES)), axis=-1, keepdims=True)
        rest = jnp.where(lane == i1, neg, logits)
        m2 = jnp.max(rest, axis=-1, keepdims=True)
        i2 = jnp.min(jnp.where(rest == m2, lane, float(LANES)), axis=-1, keepdims=True)
        e2 = jnp.exp(m2 - m1)
        p1 = 1.0 / (1.0 + e2)
        p2 = e2 / (1.0 + e2)
        hit1 = lane == i1
        hit2 = lane == i2
        gates = jnp.where(hit1, p1, 0.0) + jnp.where(hit2, p2, 0.0)
        g_hi = gates.astype(BF16)
        gsplit_ref[:, 0:LANES] = g_hi
        gsplit_ref[:, LANES:2 * LANES] = (gates - g_hi.astype(F32)).astype(BF16)

        sel = jnp.where(hit1, 1.0, 0.0) + jnp.where(hit2, 1.0, 0.0)
        r = lax.broadcasted_iota(jnp.int32, (tm, tm), 0)
        c = lax.broadcasted_iota(jnp.int32, (tm, tm), 1)
        before = jnp.where(c < r, 1.0, 0.0).astype(BF16)
        cumx = jnp.dot(before, sel.astype(BF16), preferred_element_type=F32) + carry[0:1, :]
        rank1 = jnp.sum(jnp.where(hit1, cumx, 0.0), axis=-1, keepdims=True)
        rank2 = jnp.sum(jnp.where(hit2, cumx, 0.0), axis=-1, keepdims=True)
        meta = jnp.zeros_like(logits)
        for k, val in enumerate((i1, i2, rank1, rank2)):
            meta = jnp.where(lane == float(k), val, meta)
        meta_ref[...] = meta
        meta_t_ref[...] = jnp.transpose(meta)[0:8, :]
        for k in range(tm // MOE_HALF):
            cum_ref[k] = jnp.broadcast_to(cumx[k * MOE_HALF:k * MOE_HALF + 1, :], (8, LANES))
        new_carry = carry[0:1, :] + jnp.sum(sel, axis=0, keepdims=True)
        carry[...] = jnp.broadcast_to(new_carry, carry.shape)
        total_ref[...] = jnp.broadcast_to(new_carry, total_ref.shape)

    @pl.when(i < n_prompt_tiles)
    def _():
        run(xp_ref)

    @pl.when(i >= n_prompt_tiles)
    def _():
        run(xs_ref)


def _router_call(xp, xs, mod3, mod_row_fn, w_router_pad):
    tm = FFN_TM
    npt = N_PROMPT // tm
    halves = tm // MOE_HALF
    return pl.pallas_call(
        functools.partial(_router_kernel, n_prompt_tiles=npt),
        grid=(N_TOK // tm,),
        in_specs=[
            pl.BlockSpec((tm, D_MODEL), lambda i: (jnp.minimum(i, npt - 1), 0)),
            pl.BlockSpec((tm, D_MODEL), lambda i: (jnp.maximum(i - npt, 0), 0)),
            pl.BlockSpec((None, 1, N_MOD), lambda i: (mod_row_fn(i), 0, 0)),
            pl.BlockSpec((D_MODEL, LANES), lambda i: (0, 0)),
        ],
        out_specs=[
            pl.BlockSpec((tm, D_MODEL), lambda i: (i, 0)),
            pl.BlockSpec((tm, 2 * LANES), lambda i: (i, 0)),
            pl.BlockSpec((tm, LANES), lambda i: (i, 0)),
            pl.BlockSpec((8, tm), lambda i: (0, i)),
            pl.BlockSpec((halves, 8, LANES), lambda i: (i, 0, 0)),
            pl.BlockSpec((8, LANES), lambda i: (0, 0)),
        ],
        out_shape=[
            jax.ShapeDtypeStruct((N_TOK, D_MODEL), BF16),
            jax.ShapeDtypeStruct((N_TOK, 2 * LANES), BF16),
            jax.ShapeDtypeStruct((N_TOK, LANES), F32),
            jax.ShapeDtypeStruct((8, N_TOK), F32),
            jax.ShapeDtypeStruct((N_TOK // MOE_HALF, 8, LANES), F32),
            jax.ShapeDtypeStruct((8, LANES), F32),
        ],
        scratch_shapes=[pltpu.VMEM((8, LANES), F32)],
        compiler_params=_params(1),
        name="router",
    )(xp, xs, mod3, w_router_pad)


def _ffn_kernel(xp_ref, xs_ref, mod_ref, wg_ref, wu_ref, wd_ref, lng_ref, lnb_ref, o_ref, hscr, acc, *,
                n_prompt_tiles):
    i = pl.program_id(0)
    j = pl.program_id(1)
    first = j == 0
    last = j == pl.num_programs(1) - 1

    def modulate(x_ref):
        shift = mod_ref[:, 3 * D_MODEL:4 * D_MODEL]
        scale = mod_ref[:, 4 * D_MODEL:5 * D_MODEL]
        hscr[...] = (x_ref[...] * (1.0 + scale) + shift).astype(BF16)
        acc[...] = jnp.zeros_like(acc)

    @pl.when(first & (i < n_prompt_tiles))
    def _():
        modulate(xp_ref)

    @pl.when(first & (i >= n_prompt_tiles))
    def _():
        modulate(xs_ref)

    h = hscr[...]
    g = jnp.dot(h, wg_ref[...].astype(BF16), preferred_element_type=F32)
    u = jnp.dot(h, wu_ref[...].astype(BF16), preferred_element_type=F32)
    a = _silu(g) * u
    acc[...] += jnp.dot(a.astype(BF16), wd_ref[...].astype(BF16), preferred_element_type=F32)

    def finish(x_ref):
        gate = mod_ref[:, 5 * D_MODEL:6 * D_MODEL]
        y = DEEPNORM_ALPHA * x_ref[...] + gate * acc[...]
        o_ref[...] = _layer_norm(y, lng_ref[...], lnb_ref[...])

    @pl.when(last & (i < n_prompt_tiles))
    def _():
        finish(xp_ref)

    @pl.when(last & (i >= n_prompt_tiles))
    def _():
        finish(xs_ref)


def _ffn_call(xp, xs, mod3, mod_row_fn, wg, wu, wd, layer_idx, ln_g, ln_b):
    tm, tf = FFN_TM, FFN_TF
    npt = N_PROMPT // tm
    return pl.pallas_call(
        functools.partial(_ffn_kernel, n_prompt_tiles=npt),
        grid=(N_TOK // tm, D_FF // tf),
        in_specs=[
            pl.BlockSpec((tm, D_MODEL), lambda i, j: (jnp.minimum(i, npt - 1), 0)),
            pl.BlockSpec((tm, D_MODEL), lambda i, j: (jnp.maximum(i - npt, 0), 0)),
            pl.BlockSpec((None, 1, N_MOD), lambda i, j: (mod_row_fn(i), 0, 0)),
            pl.BlockSpec((None, D_MODEL, tf), lambda i, j: (layer_idx, 0, j)),
            pl.BlockSpec((None, D_MODEL, tf), lambda i, j: (layer_idx, 0, j)),
            pl.BlockSpec((None, tf, D_MODEL), lambda i, j: (layer_idx, j, 0)),
            pl.BlockSpec((1, D_MODEL), lambda i, j: (0, 0)),
            pl.BlockSpec((1, D_MODEL), lambda i, j: (0, 0)),
        ],
        out_specs=pl.BlockSpec((tm, D_MODEL), lambda i, j: (i, 0)),
        out_shape=jax.ShapeDtypeStruct((N_TOK, D_MODEL), F32),
        scratch_shapes=[pltpu.VMEM((tm, D_MODEL), BF16), pltpu.VMEM((tm, D_MODEL), F32)],
        compiler_params=_params(2),
        name="channel_mixer_dense",
    )(xp, xs, mod3, wg, wu, wd, ln_g, ln_b)


VISIT_FIELDS = 8
NO_ROW = -1.0e9


def _moe_ffn_kernel(vinfo, gwin, h_ref, gsplit_ref, mt_ref, wg_ref, wu_ref, wd_ref, y_ref,
                    hs, gate_s, acc, wgb, wub, wdb):
    v = pl.program_id(0)
    j = pl.program_id(1)
    tile = vinfo[v * VISIT_FIELDS + 0]
    expert = vinfo[v * VISIT_FIELDS + 1]
    valid = vinfo[v * VISIT_FIELDS + 2] == 1
    row_lo = vinfo[v * VISIT_FIELDS + 3]
    row_hi = vinfo[v * VISIT_FIELDS + 4]
    expert_row0 = vinfo[v * VISIT_FIELDS + 5]
    n_g = MOE_GATHER_CHUNKS

    def sub_rows(s):
        return slice(s * MOE_CHUNK, (s + 1) * MOE_CHUNK)

    def active(s):
        return (row_lo < (s + 1) * MOE_CHUNK) & (row_hi > s * MOE_CHUNK)

    def owns_start(s):
        return row_lo <= s * MOE_CHUNK

    @pl.when(valid & (j == 0))
    def _():
        row_iota = lax.broadcasted_iota(jnp.int32, (MOE_CHUNK, MOE_CHUNK), 0).astype(F32)
        gate_lane = lax.broadcasted_iota(jnp.int32, (MOE_CHUNK, 2 * LANES), 1) % LANES
        expert_f = expert.astype(F32)
        for s in range(MOE_SUBS):
            rows = sub_rows(s)

            @pl.when(active(s) & owns_start(s))
            def _():
                hs[rows, :] = jnp.zeros((MOE_CHUNK, D_MODEL), BF16)
                gate_s[rows, :] = jnp.zeros((MOE_CHUNK, 1), F32)

            @pl.when(active(s))
            def _():
                acc[rows, :] = jnp.zeros((MOE_CHUNK, D_MODEL), F32)
                first_chunk = gwin[(v * MOE_SUBS + s) * 2]
                rank0 = (tile * MOE_TM + s * MOE_CHUNK - expert_row0).astype(F32)

                def body(w, carry):
                    want = first_chunk + w * n_g
                    cs = jnp.minimum(want, MOE_NCHUNK - n_g)
                    pieces = []
                    for k in range(n_g):
                        mt = mt_ref[cs + k]
                        rank = jnp.where(mt[0:1] == expert_f, mt[2:3],
                                         jnp.where(mt[1:2] == expert_f, mt[3:4], NO_ROW))
                        rank = rank + jnp.where(cs + k >= want, 0.0, NO_ROW)
                        pieces.append(jnp.where(row_iota + rank0 == rank, 1.0, 0.0).astype(BF16))
                    onehot = jnp.concatenate(pieces, axis=1)
                    start = pl.multiple_of(cs * MOE_CHUNK, MOE_CHUNK)
                    part = jnp.dot(onehot, h_ref[pl.ds(start, n_g * MOE_CHUNK), :], preferred_element_type=F32)
                    hs[rows, :] = hs[rows, :] + part.astype(BF16)
                    gpart = jnp.dot(onehot, gsplit_ref[pl.ds(start, n_g * MOE_CHUNK), :],
                                    preferred_element_type=F32)
                    gate_s[rows, :] = gate_s[rows, :] + jnp.sum(
                        jnp.where(gate_lane == expert, gpart, 0.0), axis=-1, keepdims=True)
                    return carry

                lax.fori_loop(0, gwin[(v * MOE_SUBS + s) * 2 + 1], body, 0)

    @pl.when(valid)
    def _():
        wgb[...] = wg_ref[...].astype(BF16)
        wub[...] = wu_ref[...].astype(BF16)
        wdb[...] = wd_ref[...].astype(BF16)
        first_sub = row_lo // MOE_CHUNK
        n_active = (row_hi + MOE_CHUNK - 1) // MOE_CHUNK - first_sub
        for n in range(1, MOE_SUBS + 1):
            @pl.when(n_active == n)
            def _():
                rows = pl.ds(pl.multiple_of(first_sub * MOE_CHUNK, MOE_CHUNK), n * MOE_CHUNK)
                h = hs[rows, :]
                g = jnp.dot(h, wgb[...], preferred_element_type=F32)
                u = jnp.dot(h, wub[...], preferred_element_type=F32)
                a = _silu(g) * u * gate_s[rows, :]
                acc[rows, :] += jnp.dot(a.astype(BF16), wdb[...], preferred_element_type=F32)

    @pl.when(valid & (j == pl.num_programs(1) - 1))
    def _():
        for s in range(MOE_SUBS):
            rows = sub_rows(s)
            row = lax.broadcasted_iota(jnp.int32, (MOE_CHUNK, 1), 0) + s * MOE_CHUNK
            mine = jnp.where(row >= row_lo, 1.0, 0.0) * jnp.where(row < row_hi, 1.0, 0.0)

            @pl.when(active(s) & owns_start(s))
            def _():
                y_ref[rows, :] = (acc[rows, :] * mine).astype(BF16)

            @pl.when(active(s) & jnp.logical_not(owns_start(s)))
            def _():
                y_ref[rows, :] = jnp.where(mine > 0.0, acc[rows, :], y_ref[rows, :].astype(F32)).astype(BF16)


def _moe_ffn_call(vinfo, gwin, h, gsplit, meta_chunks, wg, wu, wd):
    tm, tf = MOE_TM, FFN_TF
    n_j = D_FF // tf

    def expert_of(v, vinfo):
        return vinfo[v * VISIT_FIELDS + 1]

    def w_col(v, j, vinfo):
        return jnp.where(vinfo[v * VISIT_FIELDS + 2] == 1, j, n_j - 1)

    grid_spec = pltpu.PrefetchScalarGridSpec(
        num_scalar_prefetch=2,
        grid=(MOE_MAX_VISITS, n_j),
        in_specs=[
            pl.BlockSpec((N_TOK, D_MODEL), lambda v, j, vinfo, gwin: (0, 0), pipeline_mode=pl.Buffered(1)),
            pl.BlockSpec((N_TOK, 2 * LANES), lambda v, j, vinfo, gwin: (0, 0), pipeline_mode=pl.Buffered(1)),
            pl.BlockSpec((MOE_NCHUNK, 8, MOE_CHUNK), lambda v, j, vinfo, gwin: (0, 0, 0),
                         pipeline_mode=pl.Buffered(1)),
            pl.BlockSpec((None, D_MODEL, tf),
                         lambda v, j, vinfo, gwin: (expert_of(v, vinfo), 0, w_col(v, j, vinfo))),
            pl.BlockSpec((None, D_MODEL, tf),
                         lambda v, j, vinfo, gwin: (expert_of(v, vinfo), 0, w_col(v, j, vinfo))),
            pl.BlockSpec((None, tf, D_MODEL),
                         lambda v, j, vinfo, gwin: (expert_of(v, vinfo), w_col(v, j, vinfo), 0)),
        ],
        out_specs=pl.BlockSpec((tm, D_MODEL), lambda v, j, vinfo, gwin: (vinfo[v * VISIT_FIELDS], 0)),
        scratch_shapes=[pltpu.VMEM((tm, D_MODEL), BF16), pltpu.VMEM((tm, 1), F32),
                        pltpu.VMEM((tm, D_MODEL), F32),
                        pltpu.VMEM((D_MODEL, tf), BF16), pltpu.VMEM((D_MODEL, tf), BF16),
                        pltpu.VMEM((tf, D_MODEL), BF16)],
    )
    return pl.pallas_call(
        _moe_ffn_kernel,
        grid_spec=grid_spec,
        out_shape=jax.ShapeDtypeStruct((MOE_ROWS, D_MODEL), BF16),
        compiler_params=_params(2),
        name="channel_mixer_experts",
    )(vinfo, gwin, h, gsplit, meta_chunks, wg, wu, wd)


def _combine_kernel(cstart, row0, xp_ref, xs_ref, meta_ref, y_ref, mod_ref, lng_ref, lnb_ref, op_ref, os_ref, *,
                    n_prompt_tiles):
    c = pl.program_id(0)

    def run(x_ref, o_ref):
        gate = mod_ref[:, 5 * D_MODEL:6 * D_MODEL]
        col = lax.broadcasted_iota(jnp.int32, (MOE_HALF, MOE_WINDOW), 1).astype(F32)
        for half in range(MOE_CHUNK // MOE_HALF):
            rows = slice(half * MOE_HALF, (half + 1) * MOE_HALF)
            meta = meta_ref[rows, :]
            e1, e2, r1, r2 = meta[:, 0:1], meta[:, 1:2], meta[:, 2:3], meta[:, 3:4]
            total = None
            for e in range(N_EXPERTS):
                start = pl.multiple_of(cstart[(c * 2 + half) * N_EXPERTS + e], MOE_HALF)
                rank = jnp.where(e1 == float(e), r1, jnp.where(e2 == float(e), r2, NO_ROW))
                onehot = jnp.where(col == rank + (row0[e] - start).astype(F32), 1.0, 0.0).astype(BF16)
                part = jnp.dot(onehot, y_ref[pl.ds(start, MOE_WINDOW), :], preferred_element_type=F32)
                total = part if total is None else total + part
            y = DEEPNORM_ALPHA * x_ref[rows, :] + gate * total
            o_ref[rows, :] = _layer_norm(y, lng_ref[...], lnb_ref[...])

    @pl.when(c < n_prompt_tiles)
    def _():
        run(xp_ref, op_ref)

    @pl.when(c >= n_prompt_tiles)
    def _():
        run(xs_ref, os_ref)


def _combine_call(cstart, row0, xp, xs, meta, y, mod3, mod_row_fn, ln_g, ln_b):
    tm = MOE_CHUNK
    npt = N_PROMPT // tm
    grid_spec = pltpu.PrefetchScalarGridSpec(
        num_scalar_prefetch=2,
        grid=(MOE_NCHUNK,),
        in_specs=[
            pl.BlockSpec((tm, D_MODEL), lambda c, a, b: (jnp.minimum(c, npt - 1), 0)),
            pl.BlockSpec((tm, D_MODEL), lambda c, a, b: (jnp.maximum(c - npt, 0), 0)),
            pl.BlockSpec((tm, LANES), lambda c, a, b: (c, 0)),
            pl.BlockSpec((MOE_ROWS, D_MODEL), lambda c, a, b: (0, 0), pipeline_mode=pl.Buffered(1)),
            pl.BlockSpec((None, 1, N_MOD), lambda c, a, b: (mod_row_fn(c), 0, 0)),
            pl.BlockSpec((1, D_MODEL), lambda c, a, b: (0, 0)),
            pl.BlockSpec((1, D_MODEL), lambda c, a, b: (0, 0)),
        ],
        out_specs=[
            pl.BlockSpec((tm, D_MODEL), lambda c, a, b: (jnp.minimum(c, npt - 1), 0)),
            pl.BlockSpec((tm, D_MODEL), lambda c, a, b: (jnp.maximum(c - npt, 0), 0)),
        ],
    )
    return pl.pallas_call(
        functools.partial(_combine_kernel, n_prompt_tiles=npt),
        grid_spec=grid_spec,
        out_shape=[jax.ShapeDtypeStruct((N_PROMPT, D_MODEL), F32),
                   jax.ShapeDtypeStruct((N_SAMPLE, D_MODEL), F32)],
        compiler_params=_params(1),
        name="expert_combine",
    )(cstart, row0, xp, xs, meta, y, mod3, ln_g, ln_b)


def _moe_plan(cum, total):
    i32 = jnp.int32
    cnt = total[0, :N_EXPERTS].astype(i32)
    off = jnp.cumsum(cnt) - cnt
    cumh = jnp.concatenate([cum[:, 0, :N_EXPERTS], total[0:1, :N_EXPERTS]], axis=0).astype(i32)
    cumc = cumh[::MOE_CHUNK // MOE_HALF]

    n_tiles = MOE_ROWS // MOE_TM
    t0 = jnp.arange(n_tiles, dtype=i32)[:, None] * MOE_TM
    lo = jnp.maximum(t0, off[None, :]).reshape(-1)
    hi = jnp.minimum(t0 + MOE_TM, (off + cnt)[None, :]).reshape(-1)
    ok = hi > lo
    n_visits = jnp.sum(ok.astype(i32))
    order = jnp.argsort(jnp.logical_not(ok), stable=True)[:MOE_MAX_VISITS].astype(i32)
    slot = jnp.arange(MOE_MAX_VISITS, dtype=i32)
    valid = slot < n_visits
    order = order[jnp.minimum(slot, n_visits - 1)]
    vt, ve = order // N_EXPERTS, order % N_EXPERTS
    vlo = jnp.where(valid, lo[order] - vt * MOE_TM, 0)
    vhi = jnp.where(valid, hi[order] - vt * MOE_TM, 0)
    zero = jnp.zeros_like(vt)
    vinfo = jnp.stack([vt, ve, valid.astype(i32), vlo, vhi, off[ve], zero, zero], axis=1).reshape(-1).astype(i32)

    s0 = jnp.arange(MOE_SUBS, dtype=i32)[None, :] * MOE_CHUNK
    rlo = jnp.maximum(vlo[:, None], s0)
    rhi = jnp.minimum(vhi[:, None], s0 + MOE_CHUNK)
    to_rank = (vt * MOE_TM - off[ve])[:, None]
    cum_v = cumc[:, ve]
    c_lo = jnp.sum((cum_v[:, :, None] <= (rlo + to_rank)[None]).astype(i32), axis=0) - 1
    c_hi = jnp.sum((cum_v[:, :, None] < (rhi + to_rank)[None]).astype(i32), axis=0) - 1
    c_lo = jnp.clip(c_lo, 0, MOE_NCHUNK - 1)
    c_hi = jnp.clip(c_hi, 0, MOE_NCHUNK - 1)
    n_win = jnp.where(rhi <= rlo, 0, (c_hi - c_lo) // MOE_GATHER_CHUNKS + 1)
    gwin = jnp.stack([c_lo, n_win], axis=-1).reshape(-1).astype(i32)

    seg_lo = off[None, :] + cumh[:-1]
    cstart = jnp.clip((seg_lo // MOE_HALF) * MOE_HALF, 0, MOE_ROWS - MOE_WINDOW)
    return vinfo, gwin, cstart.reshape(-1).astype(i32), off.astype(i32)


def _rope_tables():
    rows = DEC_SEQ // GRID_W
    row = jnp.repeat(jnp.arange(rows, dtype=F32), GRID_W)
    col = jnp.tile(jnp.arange(GRID_W, dtype=F32), rows)
    n_freq = HEAD_DIM // 4
    inv = ROPE_THETA ** (-jnp.arange(n_freq, dtype=F32) / n_freq)
    ar = row[:, None] * inv
    ac = col[:, None] * inv
    ang = jnp.concatenate([ar, ar, ac, ac], axis=-1)
    cos = jnp.tile(jnp.cos(ang), (1, LANES // HEAD_DIM))
    sin = jnp.tile(jnp.sin(ang), (1, LANES // HEAD_DIM))
    first_half = (jnp.arange(LANES) % (2 * n_freq)) < n_freq
    sin_next = jnp.where(first_half, -sin, 0.0)
    sin_prev = jnp.where(first_half, 0.0, sin)
    return cos, sin_next, sin_prev


def kernel(x_prompt, x_sample, cache_a_k, cache_a_v, cache_b_k, cache_b_v, c, c_ctx, w_ada, b_ada, w_in,
           w_out, diff_lambda, diff_subln, qk_norm_gain, ln_gain, ln_bias, w_ffn_gate, w_ffn_up,
           w_ffn_down, w_router, w_moe_gate, w_moe_up, w_moe_down):
    cond = jnp.zeros((MOD_ROWS, D_MODEL), F32).at[0].set(c_ctx).at[1:1 + DEC_BATCH].set(c)
    mod3 = _ada_call(cond, w_ada, b_ada).reshape(DEPTH * MOD_ROWS, 1, N_MOD)
    rope_tabs = _rope_tables()
    cache = (cache_a_k.reshape(DEC_BATCH, DEPTH, PAST_LEN, DIFF_WIDTH),
             cache_a_v.reshape(DEC_BATCH, DEPTH, PAST_LEN, DIFF_WIDTH),
             cache_b_k.reshape(DEC_BATCH, DEPTH, PAST_LEN, KV_WIDTH),
             cache_b_v.reshape(DEC_BATCH, DEPTH, PAST_LEN, KV_WIDTH))

    xp = x_prompt.reshape(N_PROMPT, D_MODEL)
    xs = x_sample.reshape(N_SAMPLE, D_MODEL)
    xs_off = 0
    new_caches = []
    for l in range(DEPTH):
        base = l * MOD_ROWS
        gq = jnp.tile(qk_norm_gain[l, 0], LANES // HEAD_DIM).reshape(1, LANES)
        gk = jnp.tile(qk_norm_gain[l, 1], LANES // HEAD_DIM).reshape(1, LANES)
        subln = diff_subln[l].reshape(1, LANES)
        ln1_g, ln1_b = ln_gain[l, 0].reshape(1, D_MODEL), ln_bias[l, 0].reshape(1, D_MODEL)
        ln2_g, ln2_b = ln_gain[l, 1].reshape(1, D_MODEL), ln_bias[l, 1].reshape(1, D_MODEL)

        q_p, kv_p, ak, av, bk, bv = _inproj_call(
            xp, 0, N_PROMPT, mod3, lambda i: base, w_in, gq, gk, l, None, True)
        new_caches.append((ak, av, bk, bv))
        x1_p = _attn_call(q_p, kv_p, None, xp, 0, mod3, lambda b: base, w_out, diff_lambda, subln,
                          ln1_g, ln1_b, l, BATCH, SEQ)
        s_tiles = DEC_SEQ // PROJ_TM
        q_s, kv_s = _inproj_call(
            xs, xs_off, N_SAMPLE, mod3, lambda i: base + 1 + i // s_tiles, w_in, gq, gk, l, rope_tabs, False)
        x1_s = _attn_call(q_s, kv_s, cache, xs, xs_off, mod3, lambda b: base + 1 + b, w_out, diff_lambda,
                          subln, ln1_g, ln1_b, l, DEC_BATCH, DEC_SEQ)

        npt = N_PROMPT // FFN_TM
        s_ffn_tiles = DEC_SEQ // FFN_TM
        mod_row = lambda i: base + jnp.where(i < npt, 0, 1 + (i - npt) // s_ffn_tiles)
        k = l // 2
        if l % 2 == 0:
            x2 = _ffn_call(x1_p, x1_s, mod3, mod_row, w_ffn_gate, w_ffn_up, w_ffn_down, k, ln2_g, ln2_b)
        else:
            wr = jnp.zeros((D_MODEL, LANES), F32).at[:, :N_EXPERTS].set(w_router[k])
            h2, gsplit, meta, meta_t, cum, total = _router_call(x1_p, x1_s, mod3, mod_row, wr)
            vinfo, gwin, cstart, row0 = _moe_plan(cum, total)
            meta_chunks = meta_t.reshape(8, MOE_NCHUNK, MOE_CHUNK).transpose(1, 0, 2)
            y = _moe_ffn_call(vinfo, gwin, h2, gsplit, meta_chunks, w_moe_gate[k], w_moe_up[k], w_moe_down[k])
            npc = N_PROMPT // MOE_CHUNK
            s_chunks = DEC_SEQ // MOE_CHUNK
            mod_row_c = lambda c: base + jnp.where(c < npc, 0, 1 + (c - npc) // s_chunks)
            xp, xs = _combine_call(cstart, row0, x1_p, x1_s, meta, y, mod3, mod_row_c, ln2_g, ln2_b)
            xs_off = 0
            continue
        xp, xs, xs_off = x2, x2, N_PROMPT

    y_prompt = xp[:N_PROMPT].reshape(BATCH, SEQ, D_MODEL)
    y_sample = xs[xs_off:xs_off + N_SAMPLE].reshape(DEC_BATCH, DEC_SEQ, D_MODEL)
    stack = lambda idx, shape: jnp.stack([nc[idx] for nc in new_caches], axis=0).reshape(
        (DEPTH, BATCH, SEQ) + shape).transpose((1, 0, 2) + tuple(range(3, 3 + len(shape))))
    return (y_prompt, y_sample,
            stack(0, (DIFF_HEADS, 2, HEAD_DIM)), stack(1, (DIFF_HEADS, 2 * HEAD_DIM)),
            stack(2, (GQA_KV_HEADS, HEAD_DIM)), stack(3, (GQA_KV_HEADS, HEAD_DIM)))
```

```python
import functools
import math

import jax
import jax.numpy as jnp
from jax import lax
from jax.experimental import pallas as pl
from jax.experimental.pallas import tpu as pltpu

D_MODEL = 1024
BATCH = 16
SEQ = 256
DEPTH = 2
DEC_BATCH = 2
DEC_SEQ = 2048
PAST_LEN = 256
GRID_W = 64
HEAD_DIM = 64
DIFF_HEADS = 4
GQA_HEADS = 8
GQA_KV_HEADS = 2
DIFF_WIDTH = DIFF_HEADS * 2 * HEAD_DIM
GQA_WIDTH = GQA_HEADS * HEAD_DIM
KV_WIDTH = GQA_KV_HEADS * HEAD_DIM
IN_COLS = 3 * DIFF_WIDTH + GQA_WIDTH + 2 * KV_WIDTH
D_FF = 2816
N_EXPERTS = 8
ROPE_THETA = 10000.0
EPS = 1e-6
DEEPNORM_ALPHA = (2 * DEPTH) ** 0.25

N_PROMPT = BATCH * SEQ
N_SAMPLE = DEC_BATCH * DEC_SEQ
N_TOK = N_PROMPT + N_SAMPLE
N_MOD = 6 * D_MODEL
MOD_ROWS = 8

LANES = 128
VMEM_LIMIT = 56 * 1024 * 1024

ADA_TN = 1536
PROJ_TM = 512
ATT_TQ = 256
ATT_CTX_BATCHES = 2
FFN_TM = 1024
FFN_TF = 256
MOE_TM = 2048
MOE_CHUNK = 256
MOE_HALF = MOE_CHUNK // 2
MOE_WINDOW = 2 * MOE_HALF
MOE_GATHER_CHUNKS = 6
MOE_ROWS = 2 * N_TOK
MOE_NCHUNK = N_TOK // MOE_CHUNK
MOE_SUBS = MOE_TM // MOE_CHUNK
MOE_MAX_VISITS = MOE_ROWS // MOE_TM + N_EXPERTS - 1

KV_DIFF_STRIDE = 3 * LANES
KV_GQA_OFF = DIFF_HEADS * KV_DIFF_STRIDE
KV_PREP_COLS = KV_GQA_OFF + 4 * LANES
KV_NEW_COLS = 2 * DIFF_WIDTH + 4 * KV_WIDTH

F32 = jnp.float32
BF16 = jnp.bfloat16


def _params(n_axes):
    return pltpu.CompilerParams(dimension_semantics=("arbitrary",) * n_axes,
                                vmem_limit_bytes=VMEM_LIMIT)


def _layer_norm(y, g, b):
    mu = jnp.mean(y, axis=-1, keepdims=True)
    yc = y - mu
    var = jnp.mean(yc * yc, axis=-1, keepdims=True)
    return yc * lax.rsqrt(var + EPS) * g + b


def _silu(x):
    return x * jax.nn.sigmoid(x)


def _half_masks():
    lane = lax.broadcasted_iota(jnp.int32, (1, LANES), 1)
    lo = (lane < HEAD_DIM).astype(F32)
    return lo, 1.0 - lo


def _ada_kernel(cond_ref, w_ref, b_ref, o_ref):
    s = _silu(cond_ref[...])
    o_ref[...] = jnp.dot(s, w_ref[...], precision=lax.Precision.HIGHEST,
                         preferred_element_type=F32) + b_ref[...]


def _ada_call(cond, w_ada, b_ada):
    return pl.pallas_call(
        _ada_kernel,
        grid=(DEPTH, N_MOD // ADA_TN),
        in_specs=[
            pl.BlockSpec((MOD_ROWS, D_MODEL), lambda l, n: (0, 0)),
            pl.BlockSpec((None, D_MODEL, ADA_TN), lambda l, n: (l, 0, n)),
            pl.BlockSpec((None, 1, ADA_TN), lambda l, n: (l, 0, n)),
        ],
        out_specs=pl.BlockSpec((None, MOD_ROWS, ADA_TN), lambda l, n: (l, 0, n)),
        out_shape=jax.ShapeDtypeStruct((DEPTH, MOD_ROWS, N_MOD), F32),
        compiler_params=_params(2),
        name="ada_modulation",
    )(cond, w_ada, b_ada.reshape(DEPTH, 1, N_MOD))


def _head_sumsq(x):
    r = lax.broadcasted_iota(jnp.int32, (LANES, LANES), 0) // HEAD_DIM
    c = lax.broadcasted_iota(jnp.int32, (LANES, LANES), 1) // HEAD_DIM
    ones = (r == c).astype(BF16)
    sq = x * x
    hi = sq.astype(BF16)
    lo = (sq - hi.astype(F32)).astype(BF16)
    return (jnp.dot(hi, ones, preferred_element_type=F32)
            + jnp.dot(lo, ones, preferred_element_type=F32))


def _inproj_kernel(*refs, rope, caches):
    x_ref, mod_ref, w_ref, gq_ref, gk_ref = refs[:5]
    pos = 5
    if rope:
        cos_ref, sa_ref, sb_ref = refs[pos:pos + 3]
        pos += 3
    q_out, kv_out = refs[pos:pos + 2]
    pos += 2
    if caches:
        ak_out, av_out, bk_out, bv_out = refs[pos:pos + 4]
        pos += 4
    wbf = refs[pos]

    @pl.when(pl.program_id(0) == 0)
    def _():
        wbf[...] = w_ref[...].astype(BF16)

    shift = mod_ref[:, 0:D_MODEL]
    scale = mod_ref[:, D_MODEL:2 * D_MODEL]
    h = (x_ref[...] * (1.0 + scale) + shift).astype(BF16)
    proj = jnp.dot(h, wbf[...], preferred_element_type=F32)

    def group(base, g):
        return proj[:, base + g * LANES: base + (g + 1) * LANES]

    def rot(v):
        if not rope:
            return v
        return (v * cos_ref[...] + pltpu.roll(v, LANES - HEAD_DIM // 4, axis=1) * sa_ref[...]
                + pltpu.roll(v, HEAD_DIM // 4, axis=1) * sb_ref[...])

    def normed(v, gain):
        return v * lax.rsqrt(_head_sumsq(v) * (1.0 / HEAD_DIM) + EPS) * gain

    qk_scale = HEAD_DIM ** -0.5
    off_ak, off_av, off_bq = DIFF_WIDTH, 2 * DIFF_WIDTH, 3 * DIFF_WIDTH
    off_bk, off_bv = off_bq + GQA_WIDTH, off_bq + GQA_WIDTH + KV_WIDTH

    for g in range(DIFF_WIDTH // LANES):
        q_out[:, g * LANES:(g + 1) * LANES] = (rot(group(0, g)) * qk_scale).astype(BF16)
        a_k = group(off_ak, g)
        a_v = group(off_av, g)
        if caches:
            ak_out[:, g * LANES:(g + 1) * LANES] = a_k
            av_out[:, g * LANES:(g + 1) * LANES] = a_v
        kv_out[:, g * LANES:(g + 1) * LANES] = rot(a_k).astype(BF16)
        kv_out[:, DIFF_WIDTH + g * LANES: DIFF_WIDTH + (g + 1) * LANES] = a_v.astype(BF16)
    for g in range(GQA_WIDTH // LANES):
        b_q = rot(normed(group(off_bq, g), gq_ref[...]))
        q_out[:, DIFF_WIDTH + g * LANES: DIFF_WIDTH + (g + 1) * LANES] = (b_q * qk_scale).astype(BF16)
    b_k = normed(group(off_bk, 0), gk_ref[...])
    b_v = group(off_bv, 0)
    if caches:
        bk_out[...] = b_k
        bv_out[...] = b_v
    b_k = rot(b_k)
    base = 2 * DIFF_WIDTH
    kv_out[:, base:base + LANES] = b_k.astype(BF16)
    kv_out[:, base + LANES:base + 2 * LANES] = b_v.astype(BF16)
    kv_out[:, base + 2 * LANES:base + 3 * LANES] = pltpu.roll(b_k, HEAD_DIM, axis=1).astype(BF16)
    kv_out[:, base + 3 * LANES:base + 4 * LANES] = pltpu.roll(b_v, HEAD_DIM, axis=1).astype(BF16)


def _inproj_call(x, row_off, n_rows, mod3, mod_row_fn, w_in, gq, gk, layer, rope_tabs, caches):
    tm = PROJ_TM
    n_tiles = n_rows // tm
    blk_off = row_off // tm
    rope = rope_tabs is not None
    in_specs = [
        pl.BlockSpec((tm, D_MODEL), lambda i: (i + blk_off, 0)),
        pl.BlockSpec((None, 1, N_MOD), lambda i: (mod_row_fn(i), 0, 0)),
        pl.BlockSpec((None, D_MODEL, IN_COLS), lambda i: (layer, 0, 0)),
        pl.BlockSpec((1, LANES), lambda i: (0, 0)),
        pl.BlockSpec((1, LANES), lambda i: (0, 0)),
    ]
    args = [x, mod3, w_in, gq, gk]
    if rope:
        pos_tiles = DEC_SEQ // tm
        for t in rope_tabs:
            in_specs.append(pl.BlockSpec((tm, LANES), lambda i: (i % pos_tiles, 0)))
            args.append(t)
    out_shape = [jax.ShapeDtypeStruct((n_rows, 2 * DIFF_WIDTH), BF16),
                 jax.ShapeDtypeStruct((n_rows, KV_NEW_COLS), BF16)]
    out_specs = [pl.BlockSpec((tm, 2 * DIFF_WIDTH), lambda i: (i, 0)),
                 pl.BlockSpec((tm, KV_NEW_COLS), lambda i: (i, 0))]
    if caches:
        for width in (DIFF_WIDTH, DIFF_WIDTH, KV_WIDTH, KV_WIDTH):
            out_shape.append(jax.ShapeDtypeStruct((n_rows, width), F32))
            out_specs.append(pl.BlockSpec((tm, width), lambda i: (i, 0)))
    return pl.pallas_call(
        functools.partial(_inproj_kernel, rope=rope, caches=caches),
        grid=(n_tiles,),
        in_specs=in_specs,
        out_specs=out_specs,
        out_shape=out_shape,
        scratch_shapes=[pltpu.VMEM((D_MODEL, IN_COLS), BF16)],
        compiler_params=_params(1),
        name="in_projection_rope" if rope else "in_projection_ctx",
    )(*args)


def _attn_kernel(*refs, n_new, n_cache, n_sub, lam_init):
    q_ref, kv_ref = refs[:2]
    pos = 2
    if n_cache:
        cak_ref, cav_ref, cbk_ref, cbv_ref = refs[pos:pos + 4]
        pos += 4
    x_ref, mod_ref, wout_ref, lam_ref, subln_ref, lng_ref, lnb_ref, o_ref = refs[pos:pos + 8]
    kvs, wbf, oscr = refs[pos + 8:pos + 11]

    b = pl.program_id(0)
    qi = pl.program_id(1)
    lo_f, hi_f = _half_masks()
    lo_b, hi_b = lo_f.astype(BF16), hi_f.astype(BF16)

    @pl.when((b == 0) & (qi == 0))
    def _():
        wbf[...] = wout_ref[...].astype(BF16)

    @pl.when(qi == 0)
    def _():
        for sb in range(n_sub):
            new = slice(sb * n_new, (sb + 1) * n_new)
            for h in range(DIFF_HEADS):
                k = kv_ref[new, h * LANES:(h + 1) * LANES]
                c0 = h * KV_DIFF_STRIDE
                kvs[sb, 0:n_new, c0:c0 + LANES] = k * lo_b
                kvs[sb, 0:n_new, c0 + LANES:c0 + 2 * LANES] = k * hi_b
                kvs[sb, 0:n_new, c0 + 2 * LANES:c0 + 3 * LANES] = kv_ref[new, DIFF_WIDTH + h * LANES:
                                                                         DIFF_WIDTH + (h + 1) * LANES]
                if n_cache:
                    kc = cak_ref[:, h * LANES:(h + 1) * LANES]
                    kvs[sb, n_new:n_new + n_cache, c0:c0 + LANES] = (kc * lo_f).astype(BF16)
                    kvs[sb, n_new:n_new + n_cache, c0 + LANES:c0 + 2 * LANES] = (kc * hi_f).astype(BF16)
                    kvs[sb, n_new:n_new + n_cache, c0 + 2 * LANES:c0 + 3 * LANES] = (
                        cav_ref[:, h * LANES:(h + 1) * LANES].astype(BF16))
            kvs[sb, 0:n_new, KV_GQA_OFF:KV_GQA_OFF + 4 * LANES] = kv_ref[new, 2 * DIFF_WIDTH:
                                                                          2 * DIFF_WIDTH + 4 * LANES]
            if n_cache:
                ck = cbk_ref[...]
                cv = cbv_ref[...]
                rows = slice(n_new, n_new + n_cache)
                kvs[sb, rows, KV_GQA_OFF:KV_GQA_OFF + LANES] = ck.astype(BF16)
                kvs[sb, rows, KV_GQA_OFF + LANES:KV_GQA_OFF + 2 * LANES] = cv.astype(BF16)
                kvs[sb, rows, KV_GQA_OFF + 2 * LANES:KV_GQA_OFF + 3 * LANES] = (
                    pltpu.roll(ck, HEAD_DIM, axis=1).astype(BF16))
                kvs[sb, rows, KV_GQA_OFF + 3 * LANES:KV_GQA_OFF + 4 * LANES] = (
                    pltpu.roll(cv, HEAD_DIM, axis=1).astype(BF16))

    lp = lam_ref[...]
    lam = (jnp.exp(jnp.sum(lp[0:1] * lp[1:2], axis=-1, keepdims=True))
           - jnp.exp(jnp.sum(lp[2:3] * lp[3:4], axis=-1, keepdims=True)) + lam_init)

    def scores(q, k):
        return lax.dot_general(q, k, (((1,), (1,)), ((), ())), preferred_element_type=F32)

    def softmax_parts(s):
        e = jnp.exp(s - jnp.max(s, axis=-1, keepdims=True))
        return e, 1.0 / jnp.sum(e, axis=-1, keepdims=True)

    tq = q_ref.shape[0] // n_sub
    for sb in range(n_sub):
        qrows = slice(sb * tq, (sb + 1) * tq)
        for h in range(DIFF_HEADS):
            c0 = h * KV_DIFF_STRIDE
            q = q_ref[qrows, h * LANES:(h + 1) * LANES]
            e1, r1 = softmax_parts(scores(q, kvs[sb, :, c0:c0 + LANES]))
            e2, r2 = softmax_parts(scores(q, kvs[sb, :, c0 + LANES:c0 + 2 * LANES]))
            a = (e1 * r1 - e2 * (lam * r2)).astype(BF16)
            o = jnp.dot(a, kvs[sb, :, c0 + 2 * LANES:c0 + 3 * LANES], preferred_element_type=F32)
            o = o * lax.rsqrt(jnp.mean(o * o, axis=-1, keepdims=True) + EPS) * subln_ref[...]
            oscr[qrows, h * LANES:(h + 1) * LANES] = (o * (1.0 - lam_init)).astype(BF16)

        for pair in range(GQA_HEADS // 2):
            q_pair = q_ref[qrows, DIFF_WIDTH + pair * LANES: DIFF_WIDTH + (pair + 1) * LANES]
            halves = []
            for c in range(2):
                kv_head = (2 * pair + c) // (GQA_HEADS // GQA_KV_HEADS)
                koff = KV_GQA_OFF if kv_head == c else KV_GQA_OFF + 2 * LANES
                e, r = softmax_parts(scores(q_pair * (lo_b if c == 0 else hi_b), kvs[sb, :, koff:koff + LANES]))
                o = jnp.dot(e.astype(BF16), kvs[sb, :, koff + LANES:koff + 2 * LANES],
                            preferred_element_type=F32)
                halves.append(o * r)
            o_pair = halves[0] * lo_f + halves[1] * hi_f
            oscr[qrows, DIFF_WIDTH + pair * LANES: DIFF_WIDTH + (pair + 1) * LANES] = o_pair.astype(BF16)

    mix = jnp.dot(oscr[...], wbf[...], preferred_element_type=F32)
    gate = mod_ref[:, 2 * D_MODEL:3 * D_MODEL]
    y = DEEPNORM_ALPHA * x_ref[...] + gate * mix
    o_ref[...] = _layer_norm(y, lng_ref[...], lnb_ref[...])


def _attn_call(q, kv, cache, x, x_row_off, mod3, mod_row_fn, w_out, diff_lambda, subln, ln_g, ln_b,
               layer, n_batch, n_new):
    nq = n_new // ATT_TQ
    n_sub = ATT_CTX_BATCHES if (nq == 1 and cache is None) else 1
    tq = ATT_TQ * n_sub
    n_batch = n_batch // n_sub
    x_blk_off = x_row_off // tq
    n_cache = PAST_LEN if cache is not None else 0
    lam_init = 0.8 - 0.6 * math.exp(-0.3 * layer)
    in_specs = [
        pl.BlockSpec((tq, 2 * DIFF_WIDTH), lambda b, i: (b * nq + i, 0)),
        pl.BlockSpec((n_new * n_sub, KV_NEW_COLS), lambda b, i: (b, 0)),
    ]
    args = [q, kv]
    if cache is not None:
        for arr in cache:
            width = arr.shape[-1]
            in_specs.append(pl.BlockSpec((None, None, PAST_LEN, width), lambda b, i: (b, layer, 0, 0)))
            args.append(arr)
    in_specs += [
        pl.BlockSpec((tq, D_MODEL), lambda b, i: (b * nq + i + x_blk_off, 0)),
        pl.BlockSpec((None, 1, N_MOD), lambda b, i: (mod_row_fn(b), 0, 0)),
        pl.BlockSpec((None, D_MODEL, D_MODEL), lambda b, i: (layer, 0, 0), pipeline_mode=pl.Buffered(1)),
        pl.BlockSpec((None, 4, HEAD_DIM), lambda b, i: (layer, 0, 0)),
        pl.BlockSpec((1, LANES), lambda b, i: (0, 0)),
        pl.BlockSpec((1, D_MODEL), lambda b, i: (0, 0)),
        pl.BlockSpec((1, D_MODEL), lambda b, i: (0, 0)),
    ]
    args += [x, mod3, w_out, diff_lambda, subln, ln_g, ln_b]
    return pl.pallas_call(
        functools.partial(_attn_kernel, n_new=n_new, n_cache=n_cache, n_sub=n_sub, lam_init=lam_init),
        grid=(n_batch, nq),
        in_specs=in_specs,
        out_specs=pl.BlockSpec((tq, D_MODEL), lambda b, i: (b * nq + i, 0)),
        out_shape=jax.ShapeDtypeStruct((n_batch * n_sub * n_new, D_MODEL), F32),
        scratch_shapes=[pltpu.VMEM((n_sub, n_new + n_cache, KV_PREP_COLS), BF16),
                        pltpu.VMEM((D_MODEL, D_MODEL), BF16),
                        pltpu.VMEM((tq, D_MODEL), BF16)],
        compiler_params=_params(2),
        name="token_mixer_latent" if cache is not None else "token_mixer_ctx",
    )(*args)


def _router_kernel(xp_ref, xs_ref, mod_ref, wr_ref, h_ref, gsplit_ref, meta_ref, meta_t_ref, cum_ref,
                   total_ref, carry, *, n_prompt_tiles):
    i = pl.program_id(0)

    @pl.when(i == 0)
    def _():
        carry[...] = jnp.zeros_like(carry)

    def run(x_ref):
        tm = x_ref.shape[0]
        shift = mod_ref[:, 3 * D_MODEL:4 * D_MODEL]
        scale = mod_ref[:, 4 * D_MODEL:5 * D_MODEL]
        h = x_ref[...] * (1.0 + scale) + shift
        h_ref[...] = h.astype(BF16)
        logits = jnp.dot(h, wr_ref[...], precision=lax.Precision.HIGHEST, preferred_element_type=F32)
        lane = lax.broadcasted_iota(jnp.int32, logits.shape, 1).astype(F32)
        neg = jnp.float32(-jnp.inf)
        logits = jnp.where(lane < N_EXPERTS, logits, neg)
        m1 = jnp.max(logits, axis=-1, keepdims=True)
        i1 = jnp.min(jnp.where(logits == m1, lane, float(LANES)), axis=-1, keepdims=True)
        rest = jnp.where(lane == i1, neg, logits)
        m2 = jnp.max(rest, axis=-1, keepdims=True)
        i2 = jnp.min(jnp.where(rest == m2, lane, float(LANES)), axis=-1, keepdims=True)
        e2 = jnp.exp(m2 - m1)
        p1 = 1.0 / (1.0 + e2)
        p2 = e2 / (1.0 + e2)
        hit1 = lane == i1
        hit2 = lane == i2
        gates = jnp.where(hit1, p1, 0.0) + jnp.where(hit2, p2, 0.0)
        g_hi = gates.astype(BF16)
        gsplit_ref[:, 0:LANES] = g_hi
        gsplit_ref[:, LANES:2 * LANES] = (gates - g_hi.astype(F32)).astype(BF16)

        sel = jnp.where(hit1, 1.0, 0.0) + jnp.where(hit2, 1.0, 0.0)
        r = lax.broadcasted_iota(jnp.int32, (tm, tm), 0)
        c = lax.broadcasted_iota(jnp.int32, (tm, tm), 1)
        before = jnp.where(c < r, 1.0, 0.0).astype(BF16)
        cumx = jnp.dot(before, sel.astype(BF16), preferred_element_type=F32) + carry[0:1, :]
        rank1 = jnp.sum(jnp.where(hit1, cumx, 0.0), axis=-1, keepdims=True)
        rank2 = jnp.sum(jnp.where(hit2, cumx, 0.0), axis=-1, keepdims=True)
        meta = jnp.zeros_like(logits)
        for k, val in enumerate((i1, i2, rank1, rank2)):
            meta = jnp.where(lane == float(k), val, meta)
        meta_ref[...] = meta
        meta_t_ref[...] = jnp.transpose(meta)[0:8, :]
        for k in range(tm // MOE_HALF):
            cum_ref[k] = jnp.broadcast_to(cumx[k * MOE_HALF:k * MOE_HALF + 1, :], (8, LANES))
        new_carry = carry[0:1, :] + jnp.sum(sel, axis=0, keepdims=True)
        carry[...] = jnp.broadcast_to(new_carry, carry.shape)
        total_ref[...] = jnp.broadcast_to(new_carry, total_ref.shape)

    @pl.when(i < n_prompt_tiles)
    def _():
        run(xp_ref)

    @pl.when(i >= n_prompt_tiles)
    def _():
        run(xs_ref)


def _router_call(xp, xs, mod3, mod_row_fn, w_router_pad):
    tm = FFN_TM
    npt = N_PROMPT // tm
    halves = tm // MOE_HALF
    return pl.pallas_call(
        functools.partial(_router_kernel, n_prompt_tiles=npt),
        grid=(N_TOK // tm,),
        in_specs=[
            pl.BlockSpec((tm, D_MODEL), lambda i: (jnp.minimum(i, npt - 1), 0)),
            pl.BlockSpec((tm, D_MODEL), lambda i: (jnp.maximum(i - npt, 0), 0)),
            pl.BlockSpec((None, 1, N_MOD), lambda i: (mod_row_fn(i), 0, 0)),
            pl.BlockSpec((D_MODEL, LANES), lambda i: (0, 0)),
        ],
        out_specs=[
            pl.BlockSpec((tm, D_MODEL), lambda i: (i, 0)),
            pl.BlockSpec((tm, 2 * LANES), lambda i: (i, 0)),
            pl.BlockSpec((tm, LANES), lambda i: (i, 0)),
            pl.BlockSpec((8, tm), lambda i: (0, i)),
            pl.BlockSpec((halves, 8, LANES), lambda i: (i, 0, 0)),
            pl.BlockSpec((8, LANES), lambda i: (0, 0)),
        ],
        out_shape=[
            jax.ShapeDtypeStruct((N_TOK, D_MODEL), BF16),
            jax.ShapeDtypeStruct((N_TOK, 2 * LANES), BF16),
            jax.ShapeDtypeStruct((N_TOK, LANES), F32),
            jax.ShapeDtypeStruct((8, N_TOK), F32),
            jax.ShapeDtypeStruct((N_TOK // MOE_HALF, 8, LANES), F32),
            jax.ShapeDtypeStruct((8, LANES), F32),
        ],
        scratch_shapes=[pltpu.VMEM((8, LANES), F32)],
        compiler_params=_params(1),
        name="router",
    )(xp, xs, mod3, w_router_pad)


def _ffn_kernel(xp_ref, xs_ref, mod_ref, wg_ref, wu_ref, wd_ref, lng_ref, lnb_ref, o_ref, hscr, acc, *,
                n_prompt_tiles):
    i = pl.program_id(0)
    j = pl.program_id(1)
    first = j == 0
    last = j == pl.num_programs(1) - 1

    def modulate(x_ref):
        shift = mod_ref[:, 3 * D_MODEL:4 * D_MODEL]
        scale = mod_ref[:, 4 * D_MODEL:5 * D_MODEL]
        hscr[...] = (x_ref[...] * (1.0 + scale) + shift).astype(BF16)
        acc[...] = jnp.zeros_like(acc)

    @pl.when(first & (i < n_prompt_tiles))
    def _():
        modulate(xp_ref)

    @pl.when(first & (i >= n_prompt_tiles))
    def _():
        modulate(xs_ref)

    h = hscr[...]
    g = jnp.dot(h, wg_ref[...].astype(BF16), preferred_element_type=F32)
    u = jnp.dot(h, wu_ref[...].astype(BF16), preferred_element_type=F32)
    a = _silu(g) * u
    acc[...] += jnp.dot(a.astype(BF16), wd_ref[...].astype(BF16), preferred_element_type=F32)

    def finish(x_ref):
        gate = mod_ref[:, 5 * D_MODEL:6 * D_MODEL]
        y = DEEPNORM_ALPHA * x_ref[...] + gate * acc[...]
        o_ref[...] = _layer_norm(y, lng_ref[...], lnb_ref[...])

    @pl.when(last & (i < n_prompt_tiles))
    def _():
        finish(xp_ref)

    @pl.when(last & (i >= n_prompt_tiles))
    def _():
        finish(xs_ref)


def _ffn_call(xp, xs, mod3, mod_row_fn, wg, wu, wd, layer_idx, ln_g, ln_b):
    tm, tf = FFN_TM, FFN_TF
    npt = N_PROMPT // tm
    return pl.pallas_call(
        functools.partial(_ffn_kernel, n_prompt_tiles=npt),
        grid=(N_TOK // tm, D_FF // tf),
        in_specs=[
            pl.BlockSpec((tm, D_MODEL), lambda i, j: (jnp.minimum(i, npt - 1), 0)),
            pl.BlockSpec((tm, D_MODEL), lambda i, j: (jnp.maximum(i - npt, 0), 0)),
            pl.BlockSpec((None, 1, N_MOD), lambda i, j: (mod_row_fn(i), 0, 0)),
            pl.BlockSpec((None, D_MODEL, tf), lambda i, j: (layer_idx, 0, j)),
            pl.BlockSpec((None, D_MODEL, tf), lambda i, j: (layer_idx, 0, j)),
            pl.BlockSpec((None, tf, D_MODEL), lambda i, j: (layer_idx, j, 0)),
            pl.BlockSpec((1, D_MODEL), lambda i, j: (0, 0)),
            pl.BlockSpec((1, D_MODEL), lambda i, j: (0, 0)),
        ],
        out_specs=pl.BlockSpec((tm, D_MODEL), lambda i, j: (i, 0)),
        out_shape=jax.ShapeDtypeStruct((N_TOK, D_MODEL), F32),
        scratch_shapes=[pltpu.VMEM((tm, D_MODEL), BF16), pltpu.VMEM((tm, D_MODEL), F32)],
        compiler_params=_params(2),
        name="channel_mixer_dense",
    )(xp, xs, mod3, wg, wu, wd, ln_g, ln_b)


VISIT_FIELDS = 8
NO_ROW = -1.0e9


def _moe_ffn_kernel(vinfo, gwin, h_ref, gsplit_ref, mt_ref, wg_ref, wu_ref, wd_ref, y_ref,
                    hs, gate_s, acc):
    v = pl.program_id(0)
    j = pl.program_id(1)
    tile = vinfo[v * VISIT_FIELDS + 0]
    expert = vinfo[v * VISIT_FIELDS + 1]
    valid = vinfo[v * VISIT_FIELDS + 2] == 1
    row_lo = vinfo[v * VISIT_FIELDS + 3]
    row_hi = vinfo[v * VISIT_FIELDS + 4]
    expert_row0 = vinfo[v * VISIT_FIELDS + 5]
    n_g = MOE_GATHER_CHUNKS

    def sub_rows(s):
        return slice(s * MOE_CHUNK, (s + 1) * MOE_CHUNK)

    def active(s):
        return (row_lo < (s + 1) * MOE_CHUNK) & (row_hi > s * MOE_CHUNK)

    def owns_start(s):
        return row_lo <= s * MOE_CHUNK

    @pl.when(valid & (j == 0))
    def _():
        row_iota = lax.broadcasted_iota(jnp.int32, (MOE_CHUNK, MOE_CHUNK), 0).astype(F32)
        gate_lane = lax.broadcasted_iota(jnp.int32, (MOE_CHUNK, 2 * LANES), 1) % LANES
        expert_f = expert.astype(F32)
        for s in range(MOE_SUBS):
            rows = sub_rows(s)

            @pl.when(active(s) & owns_start(s))
            def _():
                hs[rows, :] = jnp.zeros((MOE_CHUNK, D_MODEL), BF16)
                gate_s[rows, :] = jnp.zeros((MOE_CHUNK, 1), F32)

            @pl.when(active(s))
            def _():
                acc[rows, :] = jnp.zeros((MOE_CHUNK, D_MODEL), F32)
                first_chunk = gwin[(v * MOE_SUBS + s) * 2]
                rank0 = (tile * MOE_TM + s * MOE_CHUNK - expert_row0).astype(F32)

                def body(w, carry):
                    want = first_chunk + w * n_g
                    cs = jnp.minimum(want, MOE_NCHUNK - n_g)
                    pieces = []
                    for k in range(n_g):
                        mt = mt_ref[cs + k]
                        rank = jnp.where(mt[0:1] == expert_f, mt[2:3],
                                         jnp.where(mt[1:2] == expert_f, mt[3:4], NO_ROW))
                        rank = rank + jnp.where(cs + k >= want, 0.0, NO_ROW)
                        pieces.append(jnp.where(row_iota + rank0 == rank, 1.0, 0.0).astype(BF16))
                    onehot = jnp.concatenate(pieces, axis=1)
                    start = pl.multiple_of(cs * MOE_CHUNK, MOE_CHUNK)
                    part = jnp.dot(onehot, h_ref[pl.ds(start, n_g * MOE_CHUNK), :], preferred_element_type=F32)
                    hs[rows, :] = hs[rows, :] + part.astype(BF16)
                    gpart = jnp.dot(onehot, gsplit_ref[pl.ds(start, n_g * MOE_CHUNK), :],
                                    preferred_element_type=F32)
                    gate_s[rows, :] = gate_s[rows, :] + jnp.sum(
                        jnp.where(gate_lane == expert, gpart, 0.0), axis=-1, keepdims=True)
                    return carry

                lax.fori_loop(0, gwin[(v * MOE_SUBS + s) * 2 + 1], body, 0)

    @pl.when(valid)
    def _():
        first_sub = row_lo // MOE_CHUNK
        n_active = (row_hi + MOE_CHUNK - 1) // MOE_CHUNK - first_sub
        for n in range(1, MOE_SUBS + 1):
            @pl.when(n_active == n)
            def _():
                rows = pl.ds(pl.multiple_of(first_sub * MOE_CHUNK, MOE_CHUNK), n * MOE_CHUNK)
                h = hs[rows, :]
                g = jnp.dot(h, wg_ref[...].astype(BF16), preferred_element_type=F32)
                u = jnp.dot(h, wu_ref[...].astype(BF16), preferred_element_type=F32)
                a = _silu(g) * u * gate_s[rows, :]
                acc[rows, :] += jnp.dot(a.astype(BF16), wd_ref[...].astype(BF16), preferred_element_type=F32)

    @pl.when(valid & (j == pl.num_programs(1) - 1))
    def _():
        for s in range(MOE_SUBS):
            rows = sub_rows(s)
            row = lax.broadcasted_iota(jnp.int32, (MOE_CHUNK, 1), 0) + s * MOE_CHUNK
            mine = jnp.where(row >= row_lo, 1.0, 0.0) * jnp.where(row < row_hi, 1.0, 0.0)

            @pl.when(active(s) & owns_start(s))
            def _():
                y_ref[rows, :] = (acc[rows, :] * mine).astype(BF16)

            @pl.when(active(s) & jnp.logical_not(owns_start(s)))
            def _():
                y_ref[rows, :] = jnp.where(mine > 0.0, acc[rows, :], y_ref[rows, :].astype(F32)).astype(BF16)


def _moe_ffn_call(vinfo, gwin, h, gsplit, meta_chunks, wg, wu, wd):
    tm, tf = MOE_TM, FFN_TF
    n_j = D_FF // tf

    def expert_of(v, vinfo):
        return vinfo[v * VISIT_FIELDS + 1]

    def w_col(v, j, vinfo):
        return jnp.where(vinfo[v * VISIT_FIELDS + 2] == 1, j, n_j - 1)

    grid_spec = pltpu.PrefetchScalarGridSpec(
        num_scalar_prefetch=2,
        grid=(MOE_MAX_VISITS, n_j),
        in_specs=[
            pl.BlockSpec((N_TOK, D_MODEL), lambda v, j, vinfo, gwin: (0, 0), pipeline_mode=pl.Buffered(1)),
            pl.BlockSpec((N_TOK, 2 * LANES), lambda v, j, vinfo, gwin: (0, 0), pipeline_mode=pl.Buffered(1)),
            pl.BlockSpec((MOE_NCHUNK, 8, MOE_CHUNK), lambda v, j, vinfo, gwin: (0, 0, 0),
                         pipeline_mode=pl.Buffered(1)),
            pl.BlockSpec((None, D_MODEL, tf),
                         lambda v, j, vinfo, gwin: (expert_of(v, vinfo), 0, w_col(v, j, vinfo))),
            pl.BlockSpec((None, D_MODEL, tf),
                         lambda v, j, vinfo, gwin: (expert_of(v, vinfo), 0, w_col(v, j, vinfo))),
            pl.BlockSpec((None, tf, D_MODEL),
                         lambda v, j, vinfo, gwin: (expert_of(v, vinfo), w_col(v, j, vinfo), 0)),
        ],
        out_specs=pl.BlockSpec((tm, D_MODEL), lambda v, j, vinfo, gwin: (vinfo[v * VISIT_FIELDS], 0)),
        scratch_shapes=[pltpu.VMEM((tm, D_MODEL), BF16), pltpu.VMEM((tm, 1), F32),
                        pltpu.VMEM((tm, D_MODEL), F32)],
    )
    return pl.pallas_call(
        _moe_ffn_kernel,
        grid_spec=grid_spec,
        out_shape=jax.ShapeDtypeStruct((MOE_ROWS, D_MODEL), BF16),
        compiler_params=_params(2),
        name="channel_mixer_experts",
    )(vinfo, gwin, h, gsplit, meta_chunks, wg, wu, wd)


def _combine_kernel(cstart, row0, xp_ref, xs_ref, meta_ref, y_ref, mod_ref, lng_ref, lnb_ref, op_ref, os_ref, *,
                    n_prompt_tiles):
    c = pl.program_id(0)

    def run(x_ref, o_ref):
        gate = mod_ref[:, 5 * D_MODEL:6 * D_MODEL]
        col = lax.broadcasted_iota(jnp.int32, (MOE_HALF, MOE_WINDOW), 1).astype(F32)
        for half in range(MOE_CHUNK // MOE_HALF):
            rows = slice(half * MOE_HALF, (half + 1) * MOE_HALF)
            meta = meta_ref[rows, :]
            e1, e2, r1, r2 = meta[:, 0:1], meta[:, 1:2], meta[:, 2:3], meta[:, 3:4]
            total = None
            for e in range(N_EXPERTS):
                start = pl.multiple_of(cstart[(c * 2 + half) * N_EXPERTS + e], MOE_HALF)
                rank = jnp.where(e1 == float(e), r1, jnp.where(e2 == float(e), r2, NO_ROW))
                onehot = jnp.where(col == rank + (row0[e] - start).astype(F32), 1.0, 0.0).astype(BF16)
                part = jnp.dot(onehot, y_ref[pl.ds(start, MOE_WINDOW), :], preferred_element_type=F32)
                total = part if total is None else total + part
            y = DEEPNORM_ALPHA * x_ref[rows, :] + gate * total
            o_ref[rows, :] = _layer_norm(y, lng_ref[...], lnb_ref[...])

    @pl.when(c < n_prompt_tiles)
    def _():
        run(xp_ref, op_ref)

    @pl.when(c >= n_prompt_tiles)
    def _():
        run(xs_ref, os_ref)


def _combine_call(cstart, row0, xp, xs, meta, y, mod3, mod_row_fn, ln_g, ln_b):
    tm = MOE_CHUNK
    npt = N_PROMPT // tm
    grid_spec = pltpu.PrefetchScalarGridSpec(
        num_scalar_prefetch=2,
        grid=(MOE_NCHUNK,),
        in_specs=[
            pl.BlockSpec((tm, D_MODEL), lambda c, a, b: (jnp.minimum(c, npt - 1), 0)),
            pl.BlockSpec((tm, D_MODEL), lambda c, a, b: (jnp.maximum(c - npt, 0), 0)),
            pl.BlockSpec((tm, LANES), lambda c, a, b: (c, 0)),
            pl.BlockSpec((MOE_ROWS, D_MODEL), lambda c, a, b: (0, 0), pipeline_mode=pl.Buffered(1)),
            pl.BlockSpec((None, 1, N_MOD), lambda c, a, b: (mod_row_fn(c), 0, 0)),
            pl.BlockSpec((1, D_MODEL), lambda c, a, b: (0, 0)),
            pl.BlockSpec((1, D_MODEL), lambda c, a, b: (0, 0)),
        ],
        out_specs=[
            pl.BlockSpec((tm, D_MODEL), lambda c, a, b: (jnp.minimum(c, npt - 1), 0)),
            pl.BlockSpec((tm, D_MODEL), lambda c, a, b: (jnp.maximum(c - npt, 0), 0)),
        ],
    )
    return pl.pallas_call(
        functools.partial(_combine_kernel, n_prompt_tiles=npt),
        grid_spec=grid_spec,
        out_shape=[jax.ShapeDtypeStruct((N_PROMPT, D_MODEL), F32),
                   jax.ShapeDtypeStruct((N_SAMPLE, D_MODEL), F32)],
        compiler_params=_params(1),
        name="expert_combine",
    )(cstart, row0, xp, xs, meta, y, mod3, ln_g, ln_b)


def _moe_plan(cum, total):
    i32 = jnp.int32
    cnt = total[0, :N_EXPERTS].astype(i32)
    off = jnp.cumsum(cnt) - cnt
    cumh = jnp.concatenate([cum[:, 0, :N_EXPERTS], total[0:1, :N_EXPERTS]], axis=0).astype(i32)
    cumc = cumh[::MOE_CHUNK // MOE_HALF]

    n_tiles = MOE_ROWS // MOE_TM
    t0 = jnp.arange(n_tiles, dtype=i32)[:, None] * MOE_TM
    lo = jnp.maximum(t0, off[None, :]).reshape(-1)
    hi = jnp.minimum(t0 + MOE_TM, (off + cnt)[None, :]).reshape(-1)
    ok = hi > lo
    n_visits = jnp.sum(ok.astype(i32))
    order = jnp.argsort(jnp.logical_not(ok), stable=True)[:MOE_MAX_VISITS].astype(i32)
    slot = jnp.arange(MOE_MAX_VISITS, dtype=i32)
    valid = slot < n_visits
    order = order[jnp.minimum(slot, n_visits - 1)]
    vt, ve = order // N_EXPERTS, order % N_EXPERTS
    vlo = jnp.where(valid, lo[order] - vt * MOE_TM, 0)
    vhi = jnp.where(valid, hi[order] - vt * MOE_TM, 0)
    zero = jnp.zeros_like(vt)
    vinfo = jnp.stack([vt, ve, valid.astype(i32), vlo, vhi, off[ve], zero, zero], axis=1).reshape(-1).astype(i32)

    s0 = jnp.arange(MOE_SUBS, dtype=i32)[None, :] * MOE_CHUNK
    rlo = jnp.maximum(vlo[:, None], s0)
    rhi = jnp.minimum(vhi[:, None], s0 + MOE_CHUNK)
    to_rank = (vt * MOE_TM - off[ve])[:, None]
    cum_v = cumc[:, ve]
    c_lo = jnp.sum((cum_v[:, :, None] <= (rlo + to_rank)[None]).astype(i32), axis=0) - 1
    c_hi = jnp.sum((cum_v[:, :, None] < (rhi + to_rank)[None]).astype(i32), axis=0) - 1
    c_lo = jnp.clip(c_lo, 0, MOE_NCHUNK - 1)
    c_hi = jnp.clip(c_hi, 0, MOE_NCHUNK - 1)
    n_win = jnp.where(rhi <= rlo, 0, (c_hi - c_lo) // MOE_GATHER_CHUNKS + 1)
    gwin = jnp.stack([c_lo, n_win], axis=-1).reshape(-1).astype(i32)

    seg_lo = off[None, :] + cumh[:-1]
    cstart = jnp.clip((seg_lo // MOE_HALF) * MOE_HALF, 0, MOE_ROWS - MOE_WINDOW)
    return vinfo, gwin, cstart.reshape(-1).astype(i32), off.astype(i32)


def _rope_tables():
    rows = DEC_SEQ // GRID_W
    row = jnp.repeat(jnp.arange(rows, dtype=F32), GRID_W)
    col = jnp.tile(jnp.arange(GRID_W, dtype=F32), rows)
    n_freq = HEAD_DIM // 4
    inv = ROPE_THETA ** (-jnp.arange(n_freq, dtype=F32) / n_freq)
    ar = row[:, None] * inv
    ac = col[:, None] * inv
    ang = jnp.concatenate([ar, ar, ac, ac], axis=-1)
    cos = jnp.tile(jnp.cos(ang), (1, LANES // HEAD_DIM))
    sin = jnp.tile(jnp.sin(ang), (1, LANES // HEAD_DIM))
    first_half = (jnp.arange(LANES) % (2 * n_freq)) < n_freq
    sin_next = jnp.where(first_half, -sin, 0.0)
    sin_prev = jnp.where(first_half, 0.0, sin)
    return cos, sin_next, sin_prev


def kernel(x_prompt, x_sample, cache_a_k, cache_a_v, cache_b_k, cache_b_v, c, c_ctx, w_ada, b_ada, w_in,
           w_out, diff_lambda, diff_subln, qk_norm_gain, ln_gain, ln_bias, w_ffn_gate, w_ffn_up,
           w_ffn_down, w_router, w_moe_gate, w_moe_up, w_moe_down):
    cond = jnp.zeros((MOD_ROWS, D_MODEL), F32).at[0].set(c_ctx).at[1:1 + DEC_BATCH].set(c)
    mod3 = _ada_call(cond, w_ada, b_ada).reshape(DEPTH * MOD_ROWS, 1, N_MOD)
    rope_tabs = _rope_tables()
    cache = (cache_a_k.reshape(DEC_BATCH, DEPTH, PAST_LEN, DIFF_WIDTH),
             cache_a_v.reshape(DEC_BATCH, DEPTH, PAST_LEN, DIFF_WIDTH),
             cache_b_k.reshape(DEC_BATCH, DEPTH, PAST_LEN, KV_WIDTH),
             cache_b_v.reshape(DEC_BATCH, DEPTH, PAST_LEN, KV_WIDTH))

    xp = x_prompt.reshape(N_PROMPT, D_MODEL)
    xs = x_sample.reshape(N_SAMPLE, D_MODEL)
    xs_off = 0
    new_caches = []
    for l in range(DEPTH):
        base = l * MOD_ROWS
        gq = jnp.tile(qk_norm_gain[l, 0], LANES // HEAD_DIM).reshape(1, LANES)
        gk = jnp.tile(qk_norm_gain[l, 1], LANES // HEAD_DIM).reshape(1, LANES)
        subln = diff_subln[l].reshape(1, LANES)
        ln1_g, ln1_b = ln_gain[l, 0].reshape(1, D_MODEL), ln_bias[l, 0].reshape(1, D_MODEL)
        ln2_g, ln2_b = ln_gain[l, 1].reshape(1, D_MODEL), ln_bias[l, 1].reshape(1, D_MODEL)

        q_p, kv_p, ak, av, bk, bv = _inproj_call(
            xp, 0, N_PROMPT, mod3, lambda i: base, w_in, gq, gk, l, None, True)
        new_caches.append((ak, av, bk, bv))
        x1_p = _attn_call(q_p, kv_p, None, xp, 0, mod3, lambda b: base, w_out, diff_lambda, subln,
                          ln1_g, ln1_b, l, BATCH, SEQ)
        s_tiles = DEC_SEQ // PROJ_TM
        q_s, kv_s = _inproj_call(
            xs, xs_off, N_SAMPLE, mod3, lambda i: base + 1 + i // s_tiles, w_in, gq, gk, l, rope_tabs, False)
        x1_s = _attn_call(q_s, kv_s, cache, xs, xs_off, mod3, lambda b: base + 1 + b, w_out, diff_lambda,
                          subln, ln1_g, ln1_b, l, DEC_BATCH, DEC_SEQ)

        npt = N_PROMPT // FFN_TM
        s_ffn_tiles = DEC_SEQ // FFN_TM
        mod_row = lambda i: base + jnp.where(i < npt, 0, 1 + (i - npt) // s_ffn_tiles)
        k = l // 2
        if l % 2 == 0:
            x2 = _ffn_call(x1_p, x1_s, mod3, mod_row, w_ffn_gate, w_ffn_up, w_ffn_down, k, ln2_g, ln2_b)
        else:
            wr = jnp.zeros((D_MODEL, LANES), F32).at[:, :N_EXPERTS].set(w_router[k])
            h2, gsplit, meta, meta_t, cum, total = _router_call(x1_p, x1_s, mod3, mod_row, wr)
            vinfo, gwin, cstart, row0 = _moe_plan(cum, total)
            meta_chunks = meta_t.reshape(8, MOE_NCHUNK, MOE_CHUNK).transpose(1, 0, 2)
            y = _moe_ffn_call(vinfo, gwin, h2, gsplit, meta_chunks, w_moe_gate[k], w_moe_up[k], w_moe_down[k])
            npc = N_PROMPT // MOE_CHUNK
            s_chunks = DEC_SEQ // MOE_CHUNK
            mod_row_c = lambda c: base + jnp.where(c < npc, 0, 1 + (c - npc) // s_chunks)
            xp, xs = _combine_call(cstart, row0, x1_p, x1_s, meta, y, mod3, mod_row_c, ln2_g, ln2_b)
            xs_off = 0
            continue
        xp, xs, xs_off = x2, x2, N_PROMPT

    y_prompt = xp[:N_PROMPT].reshape(BATCH, SEQ, D_MODEL)
    y_sample = xs[xs_off:xs_off + N_SAMPLE].reshape(DEC_BATCH, DEC_SEQ, D_MODEL)
    stack = lambda idx, shape: jnp.stack([nc[idx] for nc in new_caches], axis=0).reshape(
        (DEPTH, BATCH, SEQ) + shape).transpose((1, 0, 2) + tuple(range(3, 3 + len(shape))))
    return (y_prompt, y_sample,
            stack(0, (DIFF_HEADS, 2, HEAD_DIM)), stack(1, (DIFF_HEADS, 2 * HEAD_DIM)),
            stack(2, (GQA_KV_HEADS, HEAD_DIM)), stack(3, (GQA_KV_HEADS, HEAD_DIM)))
```

```python
import functools
import math

import jax
import jax.numpy as jnp
from jax import lax
from jax.experimental import pallas as pl
from jax.experimental.pallas import tpu as pltpu

D_MODEL = 1024
BATCH = 16
SEQ = 256
DEPTH = 2
DEC_BATCH = 2
DEC_SEQ = 2048
PAST_LEN = 256
GRID_W = 64
HEAD_DIM = 64
DIFF_HEADS = 4
GQA_HEADS = 8
GQA_KV_HEADS = 2
DIFF_WIDTH = DIFF_HEADS * 2 * HEAD_DIM
GQA_WIDTH = GQA_HEADS * HEAD_DIM
KV_WIDTH = GQA_KV_HEADS * HEAD_DIM
IN_COLS = 3 * DIFF_WIDTH + GQA_WIDTH + 2 * KV_WIDTH
D_FF = 2816
N_EXPERTS = 8
ROPE_THETA = 10000.0
EPS = 1e-6
DEEPNORM_ALPHA = (2 * DEPTH) ** 0.25

N_PROMPT = BATCH * SEQ
N_SAMPLE = DEC_BATCH * DEC_SEQ
N_TOK = N_PROMPT + N_SAMPLE
N_MOD = 6 * D_MODEL
MOD_ROWS = 8

LANES = 128
VMEM_LIMIT = 56 * 1024 * 1024

ADA_TN = 1536
PROJ_TM = 512
ATT_TQ = 256
ATT_CTX_BATCHES = 2
FFN_TM = 1024
FFN_TF = 256
MOE_TM = 2048
MOE_CHUNK = 256
MOE_HALF = MOE_CHUNK // 2
MOE_WINDOW = 2 * MOE_HALF
MOE_GATHER_CHUNKS = 6
MOE_ROWS = 2 * N_TOK
MOE_NCHUNK = N_TOK // MOE_CHUNK
MOE_SUBS = MOE_TM // MOE_CHUNK
MOE_MAX_VISITS = MOE_ROWS // MOE_TM + N_EXPERTS - 1

KV_DIFF_STRIDE = 3 * LANES
KV_GQA_OFF = DIFF_HEADS * KV_DIFF_STRIDE
KV_PREP_COLS = KV_GQA_OFF + 4 * LANES
KV_NEW_COLS = 2 * DIFF_WIDTH + 4 * KV_WIDTH

F32 = jnp.float32
BF16 = jnp.bfloat16


def _params(n_axes):
    return pltpu.CompilerParams(dimension_semantics=("arbitrary",) * n_axes,
                                vmem_limit_bytes=VMEM_LIMIT)


def _layer_norm(y, g, b):
    mu = jnp.mean(y, axis=-1, keepdims=True)
    yc = y - mu
    var = jnp.mean(yc * yc, axis=-1, keepdims=True)
    return yc * lax.rsqrt(var + EPS) * g + b


def _silu(x):
    return x * jax.nn.sigmoid(x)


def _half_masks():
    lane = lax.broadcasted_iota(jnp.int32, (1, LANES), 1)
    lo = (lane < HEAD_DIM).astype(F32)
    return lo, 1.0 - lo


def _split_bf16(x):
    hi = x.astype(BF16)
    return hi, (x - hi.astype(F32)).astype(BF16)


def _dot_3pass(x, w):
    x_hi, x_lo = _split_bf16(x)
    w_hi, w_lo = _split_bf16(w)
    return (jnp.dot(x_hi, w_hi, preferred_element_type=F32) + jnp.dot(x_lo, w_hi, preferred_element_type=F32)
            + jnp.dot(x_hi, w_lo, preferred_element_type=F32))


def _ada_kernel(cond_ref, w_ref, b_ref, o_ref):
    o_ref[...] = _dot_3pass(_silu(cond_ref[...]), w_ref[...]) + b_ref[...]


def _ada_call(cond, w_ada, b_ada):
    return pl.pallas_call(
        _ada_kernel,
        grid=(DEPTH, N_MOD // ADA_TN),
        in_specs=[
            pl.BlockSpec((MOD_ROWS, D_MODEL), lambda l, n: (0, 0)),
            pl.BlockSpec((None, D_MODEL, ADA_TN), lambda l, n: (l, 0, n)),
            pl.BlockSpec((None, 1, ADA_TN), lambda l, n: (l, 0, n)),
        ],
        out_specs=pl.BlockSpec((None, MOD_ROWS, ADA_TN), lambda l, n: (l, 0, n)),
        out_shape=jax.ShapeDtypeStruct((DEPTH, MOD_ROWS, N_MOD), F32),
        compiler_params=_params(2),
        name="ada_modulation",
    )(cond, w_ada, b_ada.reshape(DEPTH, 1, N_MOD))


def _head_sumsq(x):
    r = lax.broadcasted_iota(jnp.int32, (LANES, LANES), 0) // HEAD_DIM
    c = lax.broadcasted_iota(jnp.int32, (LANES, LANES), 1) // HEAD_DIM
    ones = (r == c).astype(BF16)
    sq = x * x
    hi = sq.astype(BF16)
    lo = (sq - hi.astype(F32)).astype(BF16)
    return (jnp.dot(hi, ones, preferred_element_type=F32)
            + jnp.dot(lo, ones, preferred_element_type=F32))


def _inproj_kernel(*refs, rope, caches):
    x_ref, mod_ref, w_ref, gq_ref, gk_ref = refs[:5]
    pos = 5
    if rope:
        cos_ref, sa_ref, sb_ref = refs[pos:pos + 3]
        pos += 3
    q_out, kv_out = refs[pos:pos + 2]
    pos += 2
    if caches:
        ak_out, av_out, bk_out, bv_out = refs[pos:pos + 4]
        pos += 4
    wbf = refs[pos]

    @pl.when(pl.program_id(0) == 0)
    def _():
        wbf[...] = w_ref[...].astype(BF16)

    shift = mod_ref[:, 0:D_MODEL]
    scale = mod_ref[:, D_MODEL:2 * D_MODEL]
    h = (x_ref[...] * (1.0 + scale) + shift).astype(BF16)
    proj = jnp.dot(h, wbf[...], preferred_element_type=F32)

    def group(base, g):
        return proj[:, base + g * LANES: base + (g + 1) * LANES]

    def rot(v):
        if not rope:
            return v
        return (v * cos_ref[...] + pltpu.roll(v, LANES - HEAD_DIM // 4, axis=1) * sa_ref[...]
                + pltpu.roll(v, HEAD_DIM // 4, axis=1) * sb_ref[...])

    def normed(v, gain):
        return v * lax.rsqrt(_head_sumsq(v) * (1.0 / HEAD_DIM) + EPS) * gain

    qk_scale = HEAD_DIM ** -0.5
    off_ak, off_av, off_bq = DIFF_WIDTH, 2 * DIFF_WIDTH, 3 * DIFF_WIDTH
    off_bk, off_bv = off_bq + GQA_WIDTH, off_bq + GQA_WIDTH + KV_WIDTH

    for g in range(DIFF_WIDTH // LANES):
        q_out[:, g * LANES:(g + 1) * LANES] = (rot(group(0, g)) * qk_scale).astype(BF16)
        a_k = group(off_ak, g)
        a_v = group(off_av, g)
        if caches:
            ak_out[:, g * LANES:(g + 1) * LANES] = a_k
            av_out[:, g * LANES:(g + 1) * LANES] = a_v
        kv_out[:, g * LANES:(g + 1) * LANES] = rot(a_k).astype(BF16)
        kv_out[:, DIFF_WIDTH + g * LANES: DIFF_WIDTH + (g + 1) * LANES] = a_v.astype(BF16)
    for g in range(GQA_WIDTH // LANES):
        b_q = rot(normed(group(off_bq, g), gq_ref[...]))
        q_out[:, DIFF_WIDTH + g * LANES: DIFF_WIDTH + (g + 1) * LANES] = (b_q * qk_scale).astype(BF16)
    b_k = normed(group(off_bk, 0), gk_ref[...])
    b_v = group(off_bv, 0)
    if caches:
        bk_out[...] = b_k
        bv_out[...] = b_v
    b_k = rot(b_k)
    base = 2 * DIFF_WIDTH
    kv_out[:, base:base + LANES] = b_k.astype(BF16)
    kv_out[:, base + LANES:base + 2 * LANES] = b_v.astype(BF16)
    kv_out[:, base + 2 * LANES:base + 3 * LANES] = pltpu.roll(b_k, HEAD_DIM, axis=1).astype(BF16)
    kv_out[:, base + 3 * LANES:base + 4 * LANES] = pltpu.roll(b_v, HEAD_DIM, axis=1).astype(BF16)


def _inproj_call(x, row_off, n_rows, mod3, mod_row_fn, w_in, gq, gk, layer, rope_tabs, caches):
    tm = PROJ_TM
    n_tiles = n_rows // tm
    blk_off = row_off // tm
    rope = rope_tabs is not None
    in_specs = [
        pl.BlockSpec((tm, D_MODEL), lambda i: (i + blk_off, 0)),
        pl.BlockSpec((None, 1, N_MOD), lambda i: (mod_row_fn(i), 0, 0)),
        pl.BlockSpec((None, D_MODEL, IN_COLS), lambda i: (layer, 0, 0)),
        pl.BlockSpec((1, LANES), lambda i: (0, 0)),
        pl.BlockSpec((1, LANES), lambda i: (0, 0)),
    ]
    args = [x, mod3, w_in, gq, gk]
    if rope:
        pos_tiles = DEC_SEQ // tm
        for t in rope_tabs:
            in_specs.append(pl.BlockSpec((tm, LANES), lambda i: (i % pos_tiles, 0)))
            args.append(t)
    out_shape = [jax.ShapeDtypeStruct((n_rows, 2 * DIFF_WIDTH), BF16),
                 jax.ShapeDtypeStruct((n_rows, KV_NEW_COLS), BF16)]
    out_specs = [pl.BlockSpec((tm, 2 * DIFF_WIDTH), lambda i: (i, 0)),
                 pl.BlockSpec((tm, KV_NEW_COLS), lambda i: (i, 0))]
    if caches:
        for width in (DIFF_WIDTH, DIFF_WIDTH, KV_WIDTH, KV_WIDTH):
            out_shape.append(jax.ShapeDtypeStruct((n_rows, width), F32))
            out_specs.append(pl.BlockSpec((tm, width), lambda i: (i, 0)))
    return pl.pallas_call(
        functools.partial(_inproj_kernel, rope=rope, caches=caches),
        grid=(n_tiles,),
        in_specs=in_specs,
        out_specs=out_specs,
        out_shape=out_shape,
        scratch_shapes=[pltpu.VMEM((D_MODEL, IN_COLS), BF16)],
        compiler_params=_params(1),
        name="in_projection_rope" if rope else "in_projection_ctx",
    )(*args)


def _attn_kernel(*refs, n_new, n_cache, n_sub, lam_init):
    q_ref, kv_ref = refs[:2]
    pos = 2
    if n_cache:
        cak_ref, cav_ref, cbk_ref, cbv_ref = refs[pos:pos + 4]
        pos += 4
    x_ref, mod_ref, wout_ref, lam_ref, subln_ref, lng_ref, lnb_ref, o_ref = refs[pos:pos + 8]
    kvs, wbf, oscr = refs[pos + 8:pos + 11]

    b = pl.program_id(0)
    qi = pl.program_id(1)
    lo_f, hi_f = _half_masks()
    lo_b, hi_b = lo_f.astype(BF16), hi_f.astype(BF16)

    @pl.when((b == 0) & (qi == 0))
    def _():
        wbf[...] = wout_ref[...].astype(BF16)

    @pl.when(qi == 0)
    def _():
        for sb in range(n_sub):
            new = slice(sb * n_new, (sb + 1) * n_new)
            for h in range(DIFF_HEADS):
                k = kv_ref[new, h * LANES:(h + 1) * LANES]
                c0 = h * KV_DIFF_STRIDE
                kvs[sb, 0:n_new, c0:c0 + LANES] = k * lo_b
                kvs[sb, 0:n_new, c0 + LANES:c0 + 2 * LANES] = k * hi_b
                kvs[sb, 0:n_new, c0 + 2 * LANES:c0 + 3 * LANES] = kv_ref[new, DIFF_WIDTH + h * LANES:
                                                                         DIFF_WIDTH + (h + 1) * LANES]
                if n_cache:
                    kc = cak_ref[:, h * LANES:(h + 1) * LANES]
                    kvs[sb, n_new:n_new + n_cache, c0:c0 + LANES] = (kc * lo_f).astype(BF16)
                    kvs[sb, n_new:n_new + n_cache, c0 + LANES:c0 + 2 * LANES] = (kc * hi_f).astype(BF16)
                    kvs[sb, n_new:n_new + n_cache, c0 + 2 * LANES:c0 + 3 * LANES] = (
                        cav_ref[:, h * LANES:(h + 1) * LANES].astype(BF16))
            kvs[sb, 0:n_new, KV_GQA_OFF:KV_GQA_OFF + 4 * LANES] = kv_ref[new, 2 * DIFF_WIDTH:
                                                                          2 * DIFF_WIDTH + 4 * LANES]
            if n_cache:
                ck = cbk_ref[...]
                cv = cbv_ref[...]
                rows = slice(n_new, n_new + n_cache)
                kvs[sb, rows, KV_GQA_OFF:KV_GQA_OFF + LANES] = ck.astype(BF16)
                kvs[sb, rows, KV_GQA_OFF + LANES:KV_GQA_OFF + 2 * LANES] = cv.astype(BF16)
                kvs[sb, rows, KV_GQA_OFF + 2 * LANES:KV_GQA_OFF + 3 * LANES] = (
                    pltpu.roll(ck, HEAD_DIM, axis=1).astype(BF16))
                kvs[sb, rows, KV_GQA_OFF + 3 * LANES:KV_GQA_OFF + 4 * LANES] = (
                    pltpu.roll(cv, HEAD_DIM, axis=1).astype(BF16))

    lp = lam_ref[...]
    lam = (jnp.exp(jnp.sum(lp[0:1] * lp[1:2], axis=-1, keepdims=True))
           - jnp.exp(jnp.sum(lp[2:3] * lp[3:4], axis=-1, keepdims=True)) + lam_init)

    def scores(q, k):
        return lax.dot_general(q, k, (((1,), (1,)), ((), ())), preferred_element_type=F32)

    def softmax_parts(s):
        e = jnp.exp(s - jnp.max(s, axis=-1, keepdims=True))
        return e, 1.0 / jnp.sum(e, axis=-1, keepdims=True)

    tq = q_ref.shape[0] // n_sub
    for sb in range(n_sub):
        qrows = slice(sb * tq, (sb + 1) * tq)
        for h in range(DIFF_HEADS):
            c0 = h * KV_DIFF_STRIDE
            q = q_ref[qrows, h * LANES:(h + 1) * LANES]
            e1, r1 = softmax_parts(scores(q, kvs[sb, :, c0:c0 + LANES]))
            e2, r2 = softmax_parts(scores(q, kvs[sb, :, c0 + LANES:c0 + 2 * LANES]))
            a = (e1 - e2 * (lam * r2 / r1)).astype(BF16)
            o = jnp.dot(a, kvs[sb, :, c0 + 2 * LANES:c0 + 3 * LANES], preferred_element_type=F32) * r1
            o = o * lax.rsqrt(jnp.mean(o * o, axis=-1, keepdims=True) + EPS) * subln_ref[...]
            oscr[qrows, h * LANES:(h + 1) * LANES] = (o * (1.0 - lam_init)).astype(BF16)

        for pair in range(GQA_HEADS // 2):
            q_pair = q_ref[qrows, DIFF_WIDTH + pair * LANES: DIFF_WIDTH + (pair + 1) * LANES]
            halves = []
            for c in range(2):
                kv_head = (2 * pair + c) // (GQA_HEADS // GQA_KV_HEADS)
                koff = KV_GQA_OFF if kv_head == c else KV_GQA_OFF + 2 * LANES
                e, r = softmax_parts(scores(q_pair * (lo_b if c == 0 else hi_b), kvs[sb, :, koff:koff + LANES]))
                o = jnp.dot(e.astype(BF16), kvs[sb, :, koff + LANES:koff + 2 * LANES],
                            preferred_element_type=F32)
                halves.append(o * r)
            o_pair = halves[0] * lo_f + halves[1] * hi_f
            oscr[qrows, DIFF_WIDTH + pair * LANES: DIFF_WIDTH + (pair + 1) * LANES] = o_pair.astype(BF16)

    mix = jnp.dot(oscr[...], wbf[...], preferred_element_type=F32)
    gate = mod_ref[:, 2 * D_MODEL:3 * D_MODEL]
    y = DEEPNORM_ALPHA * x_ref[...] + gate * mix
    o_ref[...] = _layer_norm(y, lng_ref[...], lnb_ref[...])


def _attn_call(q, kv, cache, x, x_row_off, mod3, mod_row_fn, w_out, diff_lambda, subln, ln_g, ln_b,
               layer, n_batch, n_new):
    nq = n_new // ATT_TQ
    n_sub = ATT_CTX_BATCHES if (nq == 1 and cache is None) else 1
    tq = ATT_TQ * n_sub
    n_batch = n_batch // n_sub
    x_blk_off = x_row_off // tq
    n_cache = PAST_LEN if cache is not None else 0
    lam_init = 0.8 - 0.6 * math.exp(-0.3 * layer)
    in_specs = [
        pl.BlockSpec((tq, 2 * DIFF_WIDTH), lambda b, i: (b * nq + i, 0)),
        pl.BlockSpec((n_new * n_sub, KV_NEW_COLS), lambda b, i: (b, 0)),
    ]
    args = [q, kv]
    if cache is not None:
        for arr in cache:
            width = arr.shape[-1]
            in_specs.append(pl.BlockSpec((None, None, PAST_LEN, width), lambda b, i: (b, layer, 0, 0)))
            args.append(arr)
    in_specs += [
        pl.BlockSpec((tq, D_MODEL), lambda b, i: (b * nq + i + x_blk_off, 0)),
        pl.BlockSpec((None, 1, N_MOD), lambda b, i: (mod_row_fn(b), 0, 0)),
        pl.BlockSpec((None, D_MODEL, D_MODEL), lambda b, i: (layer, 0, 0), pipeline_mode=pl.Buffered(1)),
        pl.BlockSpec((None, 4, HEAD_DIM), lambda b, i: (layer, 0, 0)),
        pl.BlockSpec((1, LANES), lambda b, i: (0, 0)),
        pl.BlockSpec((1, D_MODEL), lambda b, i: (0, 0)),
        pl.BlockSpec((1, D_MODEL), lambda b, i: (0, 0)),
    ]
    args += [x, mod3, w_out, diff_lambda, subln, ln_g, ln_b]
    return pl.pallas_call(
        functools.partial(_attn_kernel, n_new=n_new, n_cache=n_cache, n_sub=n_sub, lam_init=lam_init),
        grid=(n_batch, nq),
        in_specs=in_specs,
        out_specs=pl.BlockSpec((tq, D_MODEL), lambda b, i: (b * nq + i, 0)),
        out_shape=jax.ShapeDtypeStruct((n_batch * n_sub * n_new, D_MODEL), F32),
        scratch_shapes=[pltpu.VMEM((n_sub, n_new + n_cache, KV_PREP_COLS), BF16),
                        pltpu.VMEM((D_MODEL, D_MODEL), BF16),
                        pltpu.VMEM((tq, D_MODEL), BF16)],
        compiler_params=_params(2),
        name="token_mixer_latent" if cache is not None else "token_mixer_ctx",
    )(*args)


def _router_kernel(xp_ref, xs_ref, mod_ref, wr_ref, h_ref, gsplit_ref, meta_ref, meta_t_ref, cum_ref,
                   total_ref, carry, *, n_prompt_tiles):
    i = pl.program_id(0)

    @pl.when(i == 0)
    def _():
        carry[...] = jnp.zeros_like(carry)

    def run(x_ref):
        tm = x_ref.shape[0]
        shift = mod_ref[:, 3 * D_MODEL:4 * D_MODEL]
        scale = mod_ref[:, 4 * D_MODEL:5 * D_MODEL]
        h = x_ref[...] * (1.0 + scale) + shift
        h_ref[...] = h.astype(BF16)
        logits = _dot_3pass(h, wr_ref[...])
        lane = lax.broadcasted_iota(jnp.int32, logits.shape, 1).astype(F32)
        neg = jnp.float32(-jnp.inf)
        logits = jnp.where(lane < N_EXPERTS, logits, neg)
        m1 = jnp.max(logits, axis=-1, keepdims=True)
        i1 = jnp.min(jnp.where(logits == m1, lane, float(LANES)), axis=-1, keepdims=True)
        rest = jnp.where(lane == i1, neg, logits)
        m2 = jnp.max(rest, axis=-1, keepdims=True)
        i2 = jnp.min(jnp.where(rest == m2, lane, float(LANES)), axis=-1, keepdims=True)
        e2 = jnp.exp(m2 - m1)
        p1 = 1.0 / (1.0 + e2)
        p2 = e2 / (1.0 + e2)
        hit1 = lane == i1
        hit2 = lane == i2
        gates = jnp.where(hit1, p1, 0.0) + jnp.where(hit2, p2, 0.0)
        g_hi = gates.astype(BF16)
        gsplit_ref[:, 0:LANES] = g_hi
        gsplit_ref[:, LANES:2 * LANES] = (gates - g_hi.astype(F32)).astype(BF16)

        sel = jnp.where(hit1, 1.0, 0.0) + jnp.where(hit2, 1.0, 0.0)
        r = lax.broadcasted_iota(jnp.int32, (tm, tm), 0)
        c = lax.broadcasted_iota(jnp.int32, (tm, tm), 1)
        before = jnp.where(c < r, 1.0, 0.0).astype(BF16)
        cumx = jnp.dot(before, sel.astype(BF16), preferred_element_type=F32) + carry[0:1, :]
        rank1 = jnp.sum(jnp.where(hit1, cumx, 0.0), axis=-1, keepdims=True)
        rank2 = jnp.sum(jnp.where(hit2, cumx, 0.0), axis=-1, keepdims=True)
        meta = jnp.zeros_like(logits)
        for k, val in enumerate((i1, i2, rank1, rank2)):
            meta = jnp.where(lane == float(k), val, meta)
        meta_ref[...] = meta
        meta_t_ref[...] = jnp.transpose(meta)[0:8, :]
        for k in range(tm // MOE_HALF):
            cum_ref[k] = jnp.broadcast_to(cumx[k * MOE_HALF:k * MOE_HALF + 1, :], (8, LANES))
        new_carry = carry[0:1, :] + jnp.sum(sel, axis=0, keepdims=True)
        carry[...] = jnp.broadcast_to(new_carry, carry.shape)
        total_ref[...] = jnp.broadcast_to(new_carry, total_ref.shape)

    @pl.when(i < n_prompt_tiles)
    def _():
        run(xp_ref)

    @pl.when(i >= n_prompt_tiles)
    def _():
        run(xs_ref)


def _router_call(xp, xs, mod3, mod_row_fn, w_router_pad):
    tm = FFN_TM
    npt = N_PROMPT // tm
    halves = tm // MOE_HALF
    return pl.pallas_call(
        functools.partial(_router_kernel, n_prompt_tiles=npt),
        grid=(N_TOK // tm,),
        in_specs=[
            pl.BlockSpec((tm, D_MODEL), lambda i: (jnp.minimum(i, npt - 1), 0)),
            pl.BlockSpec((tm, D_MODEL), lambda i: (jnp.maximum(i - npt, 0), 0)),
            pl.BlockSpec((None, 1, N_MOD), lambda i: (mod_row_fn(i), 0, 0)),
            pl.BlockSpec((D_MODEL, LANES), lambda i: (0, 0)),
        ],
        out_specs=[
            pl.BlockSpec((tm, D_MODEL), lambda i: (i, 0)),
            pl.BlockSpec((tm, 2 * LANES), lambda i: (i, 0)),
            pl.BlockSpec((tm, LANES), lambda i: (i, 0)),
            pl.BlockSpec((8, tm), lambda i: (0, i)),
            pl.BlockSpec((halves, 8, LANES), lambda i: (i, 0, 0)),
            pl.BlockSpec((8, LANES), lambda i: (0, 0)),
        ],
        out_shape=[
            jax.ShapeDtypeStruct((N_TOK, D_MODEL), BF16),
            jax.ShapeDtypeStruct((N_TOK, 2 * LANES), BF16),
            jax.ShapeDtypeStruct((N_TOK, LANES), F32),
            jax.ShapeDtypeStruct((8, N_TOK), F32),
            jax.ShapeDtypeStruct((N_TOK // MOE_HALF, 8, LANES), F32),
            jax.ShapeDtypeStruct((8, LANES), F32),
        ],
        scratch_shapes=[pltpu.VMEM((8, LANES), F32)],
        compiler_params=_params(1),
        name="router",
    )(xp, xs, mod3, w_router_pad)


def _ffn_kernel(xp_ref, xs_ref, mod_ref, wg_ref, wu_ref, wd_ref, lng_ref, lnb_ref, o_ref, hscr, acc, *,
                n_prompt_tiles):
    i = pl.program_id(0)
    j = pl.program_id(1)
    first = j == 0
    last = j == pl.num_programs(1) - 1

    def modulate(x_ref):
        shift = mod_ref[:, 3 * D_MODEL:4 * D_MODEL]
        scale = mod_ref[:, 4 * D_MODEL:5 * D_MODEL]
        hscr[...] = (x_ref[...] * (1.0 + scale) + shift).astype(BF16)
        acc[...] = jnp.zeros_like(acc)

    @pl.when(first & (i < n_prompt_tiles))
    def _():
        modulate(xp_ref)

    @pl.when(first & (i >= n_prompt_tiles))
    def _():
        modulate(xs_ref)

    h = hscr[...]
    g = jnp.dot(h, wg_ref[...].astype(BF16), preferred_element_type=F32)
    u = jnp.dot(h, wu_ref[...].astype(BF16), preferred_element_type=F32)
    a = _silu(g) * u
    acc[...] += jnp.dot(a.astype(BF16), wd_ref[...].astype(BF16), preferred_element_type=F32)

    def finish(x_ref):
        gate = mod_ref[:, 5 * D_MODEL:6 * D_MODEL]
        y = DEEPNORM_ALPHA * x_ref[...] + gate * acc[...]
        o_ref[...] = _layer_norm(y, lng_ref[...], lnb_ref[...])

    @pl.when(last & (i < n_prompt_tiles))
    def _():
        finish(xp_ref)

    @pl.when(last & (i >= n_prompt_tiles))
    def _():
        finish(xs_ref)


def _ffn_call(xp, xs, mod3, mod_row_fn, wg, wu, wd, layer_idx, ln_g, ln_b):
    tm, tf = FFN_TM, FFN_TF
    npt = N_PROMPT // tm
    return pl.pallas_call(
        functools.partial(_ffn_kernel, n_prompt_tiles=npt),
        grid=(N_TOK // tm, D_FF // tf),
        in_specs=[
            pl.BlockSpec((tm, D_MODEL), lambda i, j: (jnp.minimum(i, npt - 1), 0)),
            pl.BlockSpec((tm, D_MODEL), lambda i, j: (jnp.maximum(i - npt, 0), 0)),
            pl.BlockSpec((None, 1, N_MOD), lambda i, j: (mod_row_fn(i), 0, 0)),
            pl.BlockSpec((None, D_MODEL, tf), lambda i, j: (layer_idx, 0, j)),
            pl.BlockSpec((None, D_MODEL, tf), lambda i, j: (layer_idx, 0, j)),
            pl.BlockSpec((None, tf, D_MODEL), lambda i, j: (layer_idx, j, 0)),
            pl.BlockSpec((1, D_MODEL), lambda i, j: (0, 0)),
            pl.BlockSpec((1, D_MODEL), lambda i, j: (0, 0)),
        ],
        out_specs=pl.BlockSpec((tm, D_MODEL), lambda i, j: (i, 0)),
        out_shape=jax.ShapeDtypeStruct((N_TOK, D_MODEL), F32),
        scratch_shapes=[pltpu.VMEM((tm, D_MODEL), BF16), pltpu.VMEM((tm, D_MODEL), F32)],
        compiler_params=_params(2),
        name="channel_mixer_dense",
    )(xp, xs, mod3, wg, wu, wd, ln_g, ln_b)


VISIT_FIELDS = 8
NO_ROW = -1.0e9


def _moe_ffn_kernel(vinfo, gwin, h_ref, gsplit_ref, mt_ref, wg_ref, wu_ref, wd_ref, y_ref,
                    hs, gate_s, acc):
    v = pl.program_id(0)
    j = pl.program_id(1)
    tile = vinfo[v * VISIT_FIELDS + 0]
    expert = vinfo[v * VISIT_FIELDS + 1]
    valid = vinfo[v * VISIT_FIELDS + 2] == 1
    row_lo = vinfo[v * VISIT_FIELDS + 3]
    row_hi = vinfo[v * VISIT_FIELDS + 4]
    expert_row0 = vinfo[v * VISIT_FIELDS + 5]
    n_g = MOE_GATHER_CHUNKS

    def sub_rows(s):
        return slice(s * MOE_CHUNK, (s + 1) * MOE_CHUNK)

    def active(s):
        return (row_lo < (s + 1) * MOE_CHUNK) & (row_hi > s * MOE_CHUNK)

    def owns_start(s):
        return row_lo <= s * MOE_CHUNK

    @pl.when(valid & (j == 0))
    def _():
        row_iota = lax.broadcasted_iota(jnp.int32, (MOE_CHUNK, MOE_CHUNK), 0).astype(F32)
        gate_lane = lax.broadcasted_iota(jnp.int32, (MOE_CHUNK, 2 * LANES), 1) % LANES
        expert_f = expert.astype(F32)
        for s in range(MOE_SUBS):
            rows = sub_rows(s)

            @pl.when(active(s) & owns_start(s))
            def _():
                hs[rows, :] = jnp.zeros((MOE_CHUNK, D_MODEL), BF16)
                gate_s[rows, :] = jnp.zeros((MOE_CHUNK, 1), F32)

            @pl.when(active(s))
            def _():
                acc[rows, :] = jnp.zeros((MOE_CHUNK, D_MODEL), F32)
                first_chunk = gwin[(v * MOE_SUBS + s) * 2]
                rank0 = (tile * MOE_TM + s * MOE_CHUNK - expert_row0).astype(F32)

                def body(w, carry):
                    want = first_chunk + w * n_g
                    cs = jnp.minimum(want, MOE_NCHUNK - n_g)
                    pieces = []
                    for k in range(n_g):
                        mt = mt_ref[cs + k]
                        rank = jnp.where(mt[0:1] == expert_f, mt[2:3],
                                         jnp.where(mt[1:2] == expert_f, mt[3:4], NO_ROW))
                        rank = rank + jnp.where(cs + k >= want, 0.0, NO_ROW)
                        pieces.append(jnp.where(row_iota + rank0 == rank, 1.0, 0.0).astype(BF16))
                    onehot = jnp.concatenate(pieces, axis=1)
                    start = pl.multiple_of(cs * MOE_CHUNK, MOE_CHUNK)
                    part = jnp.dot(onehot, h_ref[pl.ds(start, n_g * MOE_CHUNK), :], preferred_element_type=F32)
                    hs[rows, :] = hs[rows, :] + part.astype(BF16)
                    gpart = jnp.dot(onehot, gsplit_ref[pl.ds(start, n_g * MOE_CHUNK), :],
                                    preferred_element_type=F32)
                    gate_s[rows, :] = gate_s[rows, :] + jnp.sum(
                        jnp.where(gate_lane == expert, gpart, 0.0), axis=-1, keepdims=True)
                    return carry

                lax.fori_loop(0, gwin[(v * MOE_SUBS + s) * 2 + 1], body, 0)

    @pl.when(valid)
    def _():
        first_sub = row_lo // MOE_CHUNK
        n_active = (row_hi + MOE_CHUNK - 1) // MOE_CHUNK - first_sub
        for n in range(1, MOE_SUBS + 1):
            @pl.when(n_active == n)
            def _():
                rows = pl.ds(pl.multiple_of(first_sub * MOE_CHUNK, MOE_CHUNK), n * MOE_CHUNK)
                h = hs[rows, :]
                g = jnp.dot(h, wg_ref[...].astype(BF16), preferred_element_type=F32)
                u = jnp.dot(h, wu_ref[...].astype(BF16), preferred_element_type=F32)
                a = _silu(g) * u * gate_s[rows, :]
                acc[rows, :] += jnp.dot(a.astype(BF16), wd_ref[...].astype(BF16), preferred_element_type=F32)

    @pl.when(valid & (j == pl.num_programs(1) - 1))
    def _():
        for s in range(MOE_SUBS):
            rows = sub_rows(s)
            row = lax.broadcasted_iota(jnp.int32, (MOE_CHUNK, 1), 0) + s * MOE_CHUNK
            mine = jnp.where(row >= row_lo, 1.0, 0.0) * jnp.where(row < row_hi, 1.0, 0.0)

            @pl.when(active(s) & owns_start(s))
            def _():
                y_ref[rows, :] = (acc[rows, :] * mine).astype(BF16)

            @pl.when(active(s) & jnp.logical_not(owns_start(s)))
            def _():
                y_ref[rows, :] = jnp.where(mine > 0.0, acc[rows, :], y_ref[rows, :].astype(F32)).astype(BF16)


def _moe_ffn_call(vinfo, gwin, h, gsplit, meta_chunks, wg, wu, wd):
    tm, tf = MOE_TM, FFN_TF
    n_j = D_FF // tf

    def expert_of(v, vinfo):
        return vinfo[v * VISIT_FIELDS + 1]

    def w_col(v, j, vinfo):
        return jnp.where(vinfo[v * VISIT_FIELDS + 2] == 1, j, n_j - 1)

    grid_spec = pltpu.PrefetchScalarGridSpec(
        num_scalar_prefetch=2,
        grid=(MOE_MAX_VISITS, n_j),
        in_specs=[
            pl.BlockSpec((N_TOK, D_MODEL), lambda v, j, vinfo, gwin: (0, 0), pipeline_mode=pl.Buffered(1)),
            pl.BlockSpec((N_TOK, 2 * LANES), lambda v, j, vinfo, gwin: (0, 0), pipeline_mode=pl.Buffered(1)),
            pl.BlockSpec((MOE_NCHUNK, 8, MOE_CHUNK), lambda v, j, vinfo, gwin: (0, 0, 0),
                         pipeline_mode=pl.Buffered(1)),
            pl.BlockSpec((None, D_MODEL, tf),
                         lambda v, j, vinfo, gwin: (expert_of(v, vinfo), 0, w_col(v, j, vinfo))),
            pl.BlockSpec((None, D_MODEL, tf),
                         lambda v, j, vinfo, gwin: (expert_of(v, vinfo), 0, w_col(v, j, vinfo))),
            pl.BlockSpec((None, tf, D_MODEL),
                         lambda v, j, vinfo, gwin: (expert_of(v, vinfo), w_col(v, j, vinfo), 0)),
        ],
        out_specs=pl.BlockSpec((tm, D_MODEL), lambda v, j, vinfo, gwin: (vinfo[v * VISIT_FIELDS], 0)),
        scratch_shapes=[pltpu.VMEM((tm, D_MODEL), BF16), pltpu.VMEM((tm, 1), F32),
                        pltpu.VMEM((tm, D_MODEL), F32)],
    )
    return pl.pallas_call(
        _moe_ffn_kernel,
        grid_spec=grid_spec,
        out_shape=jax.ShapeDtypeStruct((MOE_ROWS, D_MODEL), BF16),
        compiler_params=_params(2),
        name="channel_mixer_experts",
    )(vinfo, gwin, h, gsplit, meta_chunks, wg, wu, wd)


def _combine_kernel(cstart, row0, xp_ref, xs_ref, meta_ref, y_ref, mod_ref, lng_ref, lnb_ref, op_ref, os_ref, *,
                    n_prompt_tiles):
    c = pl.program_id(0)

    def run(x_ref, o_ref):
        gate = mod_ref[:, 5 * D_MODEL:6 * D_MODEL]
        col = lax.broadcasted_iota(jnp.int32, (MOE_HALF, MOE_WINDOW), 1).astype(F32)
        for half in range(MOE_CHUNK // MOE_HALF):
            rows = slice(half * MOE_HALF, (half + 1) * MOE_HALF)
            meta = meta_ref[rows, :]
            e1, e2, r1, r2 = meta[:, 0:1], meta[:, 1:2], meta[:, 2:3], meta[:, 3:4]
            total = None
            for e in range(N_EXPERTS):
                start = pl.multiple_of(cstart[(c * 2 + half) * N_EXPERTS + e], MOE_HALF)
                rank = jnp.where(e1 == float(e), r1, jnp.where(e2 == float(e), r2, NO_ROW))
                onehot = jnp.where(col == rank + (row0[e] - start).astype(F32), 1.0, 0.0).astype(BF16)
                part = jnp.dot(onehot, y_ref[pl.ds(start, MOE_WINDOW), :], preferred_element_type=F32)
                total = part if total is None else total + part
            y = DEEPNORM_ALPHA * x_ref[rows, :] + gate * total
            o_ref[rows, :] = _layer_norm(y, lng_ref[...], lnb_ref[...])

    @pl.when(c < n_prompt_tiles)
    def _():
        run(xp_ref, op_ref)

    @pl.when(c >= n_prompt_tiles)
    def _():
        run(xs_ref, os_ref)


def _combine_call(cstart, row0, xp, xs, meta, y, mod3, mod_row_fn, ln_g, ln_b):
    tm = MOE_CHUNK
    npt = N_PROMPT // tm
    grid_spec = pltpu.PrefetchScalarGridSpec(
        num_scalar_prefetch=2,
        grid=(MOE_NCHUNK,),
        in_specs=[
            pl.BlockSpec((tm, D_MODEL), lambda c, a, b: (jnp.minimum(c, npt - 1), 0)),
            pl.BlockSpec((tm, D_MODEL), lambda c, a, b: (jnp.maximum(c - npt, 0), 0)),
            pl.BlockSpec((tm, LANES), lambda c, a, b: (c, 0)),
            pl.BlockSpec((MOE_ROWS, D_MODEL), lambda c, a, b: (0, 0), pipeline_mode=pl.Buffered(1)),
            pl.BlockSpec((None, 1, N_MOD), lambda c, a, b: (mod_row_fn(c), 0, 0)),
            pl.BlockSpec((1, D_MODEL), lambda c, a, b: (0, 0)),
            pl.BlockSpec((1, D_MODEL), lambda c, a, b: (0, 0)),
        ],
        out_specs=[
            pl.BlockSpec((tm, D_MODEL), lambda c, a, b: (jnp.minimum(c, npt - 1), 0)),
            pl.BlockSpec((tm, D_MODEL), lambda c, a, b: (jnp.maximum(c - npt, 0), 0)),
        ],
    )
    return pl.pallas_call(
        functools.partial(_combine_kernel, n_prompt_tiles=npt),
        grid_spec=grid_spec,
        out_shape=[jax.ShapeDtypeStruct((N_PROMPT, D_MODEL), F32),
                   jax.ShapeDtypeStruct((N_SAMPLE, D_MODEL), F32)],
        compiler_params=_params(1),
        name="expert_combine",
    )(cstart, row0, xp, xs, meta, y, mod3, ln_g, ln_b)


def _moe_plan(cum, total):
    i32 = jnp.int32
    cnt = total[0, :N_EXPERTS].astype(i32)
    off = jnp.cumsum(cnt) - cnt
    cumh = jnp.concatenate([cum[:, 0, :N_EXPERTS], total[0:1, :N_EXPERTS]], axis=0).astype(i32)
    cumc = cumh[::MOE_CHUNK // MOE_HALF]

    n_tiles = MOE_ROWS // MOE_TM
    t0 = jnp.arange(n_tiles, dtype=i32)[:, None] * MOE_TM
    lo = jnp.maximum(t0, off[None, :]).reshape(-1)
    hi = jnp.minimum(t0 + MOE_TM, (off + cnt)[None, :]).reshape(-1)
    ok = hi > lo
    n_visits = jnp.sum(ok.astype(i32))
    order = jnp.argsort(jnp.logical_not(ok), stable=True)[:MOE_MAX_VISITS].astype(i32)
    slot = jnp.arange(MOE_MAX_VISITS, dtype=i32)
    valid = slot < n_visits
    order = order[jnp.minimum(slot, n_visits - 1)]
    vt, ve = order // N_EXPERTS, order % N_EXPERTS
    vlo = jnp.where(valid, lo[order] - vt * MOE_TM, 0)
    vhi = jnp.where(valid, hi[order] - vt * MOE_TM, 0)
    zero = jnp.zeros_like(vt)
    vinfo = jnp.stack([vt, ve, valid.astype(i32), vlo, vhi, off[ve], zero, zero], axis=1).reshape(-1).astype(i32)

    s0 = jnp.arange(MOE_SUBS, dtype=i32)[None, :] * MOE_CHUNK
    rlo = jnp.maximum(vlo[:, None], s0)
    rhi = jnp.minimum(vhi[:, None], s0 + MOE_CHUNK)
    to_rank = (vt * MOE_TM - off[ve])[:, None]
    cum_v = cumc[:, ve]
    c_lo = jnp.sum((cum_v[:, :, None] <= (rlo + to_rank)[None]).astype(i32), axis=0) - 1
    c_hi = jnp.sum((cum_v[:, :, None] < (rhi + to_rank)[None]).astype(i32), axis=0) - 1
    c_lo = jnp.clip(c_lo, 0, MOE_NCHUNK - 1)
    c_hi = jnp.clip(c_hi, 0, MOE_NCHUNK - 1)
    n_win = jnp.where(rhi <= rlo, 0, (c_hi - c_lo) // MOE_GATHER_CHUNKS + 1)
    gwin = jnp.stack([c_lo, n_win], axis=-1).reshape(-1).astype(i32)

    seg_lo = off[None, :] + cumh[:-1]
    cstart = jnp.clip((seg_lo // MOE_HALF) * MOE_HALF, 0, MOE_ROWS - MOE_WINDOW)
    return vinfo, gwin, cstart.reshape(-1).astype(i32), off.astype(i32)


def _rope_tables():
    rows = DEC_SEQ // GRID_W
    row = jnp.repeat(jnp.arange(rows, dtype=F32), GRID_W)
    col = jnp.tile(jnp.arange(GRID_W, dtype=F32), rows)
    n_freq = HEAD_DIM // 4
    inv = ROPE_THETA ** (-jnp.arange(n_freq, dtype=F32) / n_freq)
    ar = row[:, None] * inv
    ac = col[:, None] * inv
    ang = jnp.concatenate([ar, ar, ac, ac], axis=-1)
    cos = jnp.tile(jnp.cos(ang), (1, LANES // HEAD_DIM))
    sin = jnp.tile(jnp.sin(ang), (1, LANES // HEAD_DIM))
    first_half = (jnp.arange(LANES) % (2 * n_freq)) < n_freq
    sin_next = jnp.where(first_half, -sin, 0.0)
    sin_prev = jnp.where(first_half, 0.0, sin)
    return cos, sin_next, sin_prev


def kernel(x_prompt, x_sample, cache_a_k, cache_a_v, cache_b_k, cache_b_v, c, c_ctx, w_ada, b_ada, w_in,
           w_out, diff_lambda, diff_subln, qk_norm_gain, ln_gain, ln_bias, w_ffn_gate, w_ffn_up,
           w_ffn_down, w_router, w_moe_gate, w_moe_up, w_moe_down):
    cond = jnp.zeros((MOD_ROWS, D_MODEL), F32).at[0].set(c_ctx).at[1:1 + DEC_BATCH].set(c)
    mod3 = _ada_call(cond, w_ada, b_ada).reshape(DEPTH * MOD_ROWS, 1, N_MOD)
    rope_tabs = _rope_tables()
    cache = (cache_a_k.reshape(DEC_BATCH, DEPTH, PAST_LEN, DIFF_WIDTH),
             cache_a_v.reshape(DEC_BATCH, DEPTH, PAST_LEN, DIFF_WIDTH),
             cache_b_k.reshape(DEC_BATCH, DEPTH, PAST_LEN, KV_WIDTH),
             cache_b_v.reshape(DEC_BATCH, DEPTH, PAST_LEN, KV_WIDTH))

    xp = x_prompt.reshape(N_PROMPT, D_MODEL)
    xs = x_sample.reshape(N_SAMPLE, D_MODEL)
    xs_off = 0
    new_caches = []
    for l in range(DEPTH):
        base = l * MOD_ROWS
        gq = jnp.tile(qk_norm_gain[l, 0], LANES // HEAD_DIM).reshape(1, LANES)
        gk = jnp.tile(qk_norm_gain[l, 1], LANES // HEAD_DIM).reshape(1, LANES)
        subln = diff_subln[l].reshape(1, LANES)
        ln1_g, ln1_b = ln_gain[l, 0].reshape(1, D_MODEL), ln_bias[l, 0].reshape(1, D_MODEL)
        ln2_g, ln2_b = ln_gain[l, 1].reshape(1, D_MODEL), ln_bias[l, 1].reshape(1, D_MODEL)

        q_p, kv_p, ak, av, bk, bv = _inproj_call(
            xp, 0, N_PROMPT, mod3, lambda i: base, w_in, gq, gk, l, None, True)
        new_caches.append((ak, av, bk, bv))
        x1_p = _attn_call(q_p, kv_p, None, xp, 0, mod3, lambda b: base, w_out, diff_lambda, subln,
                          ln1_g, ln1_b, l, BATCH, SEQ)
        s_tiles = DEC_SEQ // PROJ_TM
        q_s, kv_s = _inproj_call(
            xs, xs_off, N_SAMPLE, mod3, lambda i: base + 1 + i // s_tiles, w_in, gq, gk, l, rope_tabs, False)
        x1_s = _attn_call(q_s, kv_s, cache, xs, xs_off, mod3, lambda b: base + 1 + b, w_out, diff_lambda,
                          subln, ln1_g, ln1_b, l, DEC_BATCH, DEC_SEQ)

        npt = N_PROMPT // FFN_TM
        s_ffn_tiles = DEC_SEQ // FFN_TM
        mod_row = lambda i: base + jnp.where(i < npt, 0, 1 + (i - npt) // s_ffn_tiles)
        k = l // 2
        if l % 2 == 0:
            x2 = _ffn_call(x1_p, x1_s, mod3, mod_row, w_ffn_gate, w_ffn_up, w_ffn_down, k, ln2_g, ln2_b)
        else:
            wr = jnp.zeros((D_MODEL, LANES), F32).at[:, :N_EXPERTS].set(w_router[k])
            h2, gsplit, meta, meta_t, cum, total = _router_call(x1_p, x1_s, mod3, mod_row, wr)
            vinfo, gwin, cstart, row0 = _moe_plan(cum, total)
            meta_chunks = meta_t.reshape(8, MOE_NCHUNK, MOE_CHUNK).transpose(1, 0, 2)
            y = _moe_ffn_call(vinfo, gwin, h2, gsplit, meta_chunks, w_moe_gate[k], w_moe_up[k], w_moe_down[k])
            npc = N_PROMPT // MOE_CHUNK
            s_chunks = DEC_SEQ // MOE_CHUNK
            mod_row_c = lambda c: base + jnp.where(c < npc, 0, 1 + (c - npc) // s_chunks)
            xp, xs = _combine_call(cstart, row0, x1_p, x1_s, meta, y, mod3, mod_row_c, ln2_g, ln2_b)
            xs_off = 0
            continue
        xp, xs, xs_off = x2, x2, N_PROMPT

    y_prompt = xp[:N_PROMPT].reshape(BATCH, SEQ, D_MODEL)
    y_sample = xs[xs_off:xs_off + N_SAMPLE].reshape(DEC_BATCH, DEC_SEQ, D_MODEL)
    stack = lambda idx, shape: jnp.stack([nc[idx] for nc in new_caches], axis=0).reshape(
        (DEPTH, BATCH, SEQ) + shape).transpose((1, 0, 2) + tuple(range(3, 3 + len(shape))))
    return (y_prompt, y_sample,
            stack(0, (DIFF_HEADS, 2, HEAD_DIM)), stack(1, (DIFF_HEADS, 2 * HEAD_DIM)),
            stack(2, (GQA_KV_HEADS, HEAD_DIM)), stack(3, (GQA_KV_HEADS, HEAD_DIM)))
```

```python
import functools
import math

import jax
import jax.numpy as jnp
from jax import lax
from jax.experimental import pallas as pl
from jax.experimental.pallas import tpu as pltpu

D_MODEL = 1024
BATCH = 16
SEQ = 256
DEPTH = 2
DEC_BATCH = 2
DEC_SEQ = 2048
PAST_LEN = 256
GRID_W = 64
HEAD_DIM = 64
DIFF_HEADS = 4
GQA_HEADS = 8
GQA_KV_HEADS = 2
DIFF_WIDTH = DIFF_HEADS * 2 * HEAD_DIM
GQA_WIDTH = GQA_HEADS * HEAD_DIM
KV_WIDTH = GQA_KV_HEADS * HEAD_DIM
IN_COLS = 3 * DIFF_WIDTH + GQA_WIDTH + 2 * KV_WIDTH
D_FF = 2816
N_EXPERTS = 8
ROPE_THETA = 10000.0
EPS = 1e-6
DEEPNORM_ALPHA = (2 * DEPTH) ** 0.25

N_PROMPT = BATCH * SEQ
N_SAMPLE = DEC_BATCH * DEC_SEQ
N_TOK = N_PROMPT + N_SAMPLE
N_MOD = 6 * D_MODEL
MOD_ROWS = 8

LANES = 128
VMEM_LIMIT = 56 * 1024 * 1024

ADA_TN = 1536
PROJ_TM = 512
ATT_TQ = 256
ATT_CTX_BATCHES = 2
FFN_TM = 1024
FFN_TF = 256
MOE_TM = 2048
MOE_CHUNK = 256
MOE_HALF = MOE_CHUNK // 2
MOE_WINDOW = 2 * MOE_HALF
MOE_GATHER_CHUNKS = 6
MOE_ROWS = 2 * N_TOK
MOE_NCHUNK = N_TOK // MOE_CHUNK
MOE_SUBS = MOE_TM // MOE_CHUNK
MOE_MAX_VISITS = MOE_ROWS // MOE_TM + N_EXPERTS - 1

KV_DIFF_STRIDE = 3 * LANES
KV_GQA_OFF = DIFF_HEADS * KV_DIFF_STRIDE
KV_PREP_COLS = KV_GQA_OFF + 4 * LANES
KV_NEW_COLS = 2 * DIFF_WIDTH + 4 * KV_WIDTH

F32 = jnp.float32
BF16 = jnp.bfloat16


def _params(n_axes):
    return pltpu.CompilerParams(dimension_semantics=("arbitrary",) * n_axes,
                                vmem_limit_bytes=VMEM_LIMIT)


def _layer_norm(y, g, b):
    mu = jnp.mean(y, axis=-1, keepdims=True)
    yc = y - mu
    var = jnp.mean(yc * yc, axis=-1, keepdims=True)
    return yc * lax.rsqrt(var + EPS) * g + b


def _silu(x):
    return x * jax.nn.sigmoid(x)


def _half_masks():
    lane = lax.broadcasted_iota(jnp.int32, (1, LANES), 1)
    lo = (lane < HEAD_DIM).astype(F32)
    return lo, 1.0 - lo


def _split_bf16(x):
    hi = x.astype(BF16)
    return hi, (x - hi.astype(F32)).astype(BF16)


def _dot_3pass(x, w):
    x_hi, x_lo = _split_bf16(x)
    w_hi, w_lo = _split_bf16(w)
    return (jnp.dot(x_hi, w_hi, preferred_element_type=F32) + jnp.dot(x_lo, w_hi, preferred_element_type=F32)
            + jnp.dot(x_hi, w_lo, preferred_element_type=F32))


def _ada_kernel(cond_ref, w_ref, b_ref, o_ref):
    o_ref[...] = _dot_3pass(_silu(cond_ref[...]), w_ref[...]) + b_ref[...]


def _ada_call(cond, w_ada, b_ada):
    return pl.pallas_call(
        _ada_kernel,
        grid=(DEPTH, N_MOD // ADA_TN),
        in_specs=[
            pl.BlockSpec((MOD_ROWS, D_MODEL), lambda l, n: (0, 0)),
            pl.BlockSpec((None, D_MODEL, ADA_TN), lambda l, n: (l, 0, n)),
            pl.BlockSpec((None, 1, ADA_TN), lambda l, n: (l, 0, n)),
        ],
        out_specs=pl.BlockSpec((None, MOD_ROWS, ADA_TN), lambda l, n: (l, 0, n)),
        out_shape=jax.ShapeDtypeStruct((DEPTH, MOD_ROWS, N_MOD), F32),
        compiler_params=_params(2),
        name="ada_modulation",
    )(cond, w_ada, b_ada.reshape(DEPTH, 1, N_MOD))


def _head_sumsq(x):
    r = lax.broadcasted_iota(jnp.int32, (LANES, LANES), 0) // HEAD_DIM
    c = lax.broadcasted_iota(jnp.int32, (LANES, LANES), 1) // HEAD_DIM
    ones = (r == c).astype(BF16)
    sq = x * x
    hi = sq.astype(BF16)
    lo = (sq - hi.astype(F32)).astype(BF16)
    return (jnp.dot(hi, ones, preferred_element_type=F32)
            + jnp.dot(lo, ones, preferred_element_type=F32))


def _inproj_kernel(*refs, rope, caches):
    x_ref, mod_ref, w_ref, gq_ref, gk_ref = refs[:5]
    pos = 5
    if rope:
        cos_ref, sa_ref, sb_ref = refs[pos:pos + 3]
        pos += 3
    q_out, kv_out = refs[pos:pos + 2]
    pos += 2
    if caches:
        ak_out, av_out, bk_out, bv_out = refs[pos:pos + 4]
        pos += 4
    wbf = refs[pos]

    @pl.when(pl.program_id(0) == 0)
    def _():
        wbf[...] = w_ref[...].astype(BF16)

    shift = mod_ref[:, 0:D_MODEL]
    scale = mod_ref[:, D_MODEL:2 * D_MODEL]
    h = (x_ref[...] * (1.0 + scale) + shift).astype(BF16)
    proj = jnp.dot(h, wbf[...], preferred_element_type=F32)

    def group(base, g):
        return proj[:, base + g * LANES: base + (g + 1) * LANES]

    def rot(v):
        if not rope:
            return v
        return (v * cos_ref[...] + pltpu.roll(v, LANES - HEAD_DIM // 4, axis=1) * sa_ref[...]
                + pltpu.roll(v, HEAD_DIM // 4, axis=1) * sb_ref[...])

    def normed(v, gain):
        return v * lax.rsqrt(_head_sumsq(v) * (1.0 / HEAD_DIM) + EPS) * gain

    qk_scale = HEAD_DIM ** -0.5
    off_ak, off_av, off_bq = DIFF_WIDTH, 2 * DIFF_WIDTH, 3 * DIFF_WIDTH
    off_bk, off_bv = off_bq + GQA_WIDTH, off_bq + GQA_WIDTH + KV_WIDTH

    for g in range(DIFF_WIDTH // LANES):
        q_out[:, g * LANES:(g + 1) * LANES] = (rot(group(0, g)) * qk_scale).astype(BF16)
        a_k = group(off_ak, g)
        a_v = group(off_av, g)
        if caches:
            ak_out[:, g * LANES:(g + 1) * LANES] = a_k
            av_out[:, g * LANES:(g + 1) * LANES] = a_v
        kv_out[:, g * LANES:(g + 1) * LANES] = rot(a_k).astype(BF16)
        kv_out[:, DIFF_WIDTH + g * LANES: DIFF_WIDTH + (g + 1) * LANES] = a_v.astype(BF16)
    for g in range(GQA_WIDTH // LANES):
        b_q = rot(normed(group(off_bq, g), gq_ref[...]))
        q_out[:, DIFF_WIDTH + g * LANES: DIFF_WIDTH + (g + 1) * LANES] = (b_q * qk_scale).astype(BF16)
    b_k = normed(group(off_bk, 0), gk_ref[...])
    b_v = group(off_bv, 0)
    if caches:
        bk_out[...] = b_k
        bv_out[...] = b_v
    b_k = rot(b_k)
    base = 2 * DIFF_WIDTH
    kv_out[:, base:base + LANES] = b_k.astype(BF16)
    kv_out[:, base + LANES:base + 2 * LANES] = b_v.astype(BF16)
    kv_out[:, base + 2 * LANES:base + 3 * LANES] = pltpu.roll(b_k, HEAD_DIM, axis=1).astype(BF16)
    kv_out[:, base + 3 * LANES:base + 4 * LANES] = pltpu.roll(b_v, HEAD_DIM, axis=1).astype(BF16)


def _inproj_call(x, row_off, n_rows, mod3, mod_row_fn, w_in, gq, gk, layer, rope_tabs, caches):
    tm = PROJ_TM
    n_tiles = n_rows // tm
    blk_off = row_off // tm
    rope = rope_tabs is not None
    in_specs = [
        pl.BlockSpec((tm, D_MODEL), lambda i: (i + blk_off, 0)),
        pl.BlockSpec((None, 1, N_MOD), lambda i: (mod_row_fn(i), 0, 0)),
        pl.BlockSpec((None, D_MODEL, IN_COLS), lambda i: (layer, 0, 0)),
        pl.BlockSpec((1, LANES), lambda i: (0, 0)),
        pl.BlockSpec((1, LANES), lambda i: (0, 0)),
    ]
    args = [x, mod3, w_in, gq, gk]
    if rope:
        pos_tiles = DEC_SEQ // tm
        for t in rope_tabs:
            in_specs.append(pl.BlockSpec((tm, LANES), lambda i: (i % pos_tiles, 0)))
            args.append(t)
    out_shape = [jax.ShapeDtypeStruct((n_rows, 2 * DIFF_WIDTH), BF16),
                 jax.ShapeDtypeStruct((n_rows, KV_NEW_COLS), BF16)]
    out_specs = [pl.BlockSpec((tm, 2 * DIFF_WIDTH), lambda i: (i, 0)),
                 pl.BlockSpec((tm, KV_NEW_COLS), lambda i: (i, 0))]
    if caches:
        for width in (DIFF_WIDTH, DIFF_WIDTH, KV_WIDTH, KV_WIDTH):
            out_shape.append(jax.ShapeDtypeStruct((n_rows, width), F32))
            out_specs.append(pl.BlockSpec((tm, width), lambda i: (i, 0)))
    return pl.pallas_call(
        functools.partial(_inproj_kernel, rope=rope, caches=caches),
        grid=(n_tiles,),
        in_specs=in_specs,
        out_specs=out_specs,
        out_shape=out_shape,
        scratch_shapes=[pltpu.VMEM((D_MODEL, IN_COLS), BF16)],
        compiler_params=_params(1),
        name="in_projection_rope" if rope else "in_projection_ctx",
    )(*args)


def _attn_kernel(*refs, n_new, n_cache, n_sub, lam_init):
    q_ref, kv_ref = refs[:2]
    pos = 2
    if n_cache:
        cak_ref, cav_ref, cbk_ref, cbv_ref = refs[pos:pos + 4]
        pos += 4
    x_ref, mod_ref, wout_ref, lam_ref, subln_ref, lng_ref, lnb_ref, o_ref = refs[pos:pos + 8]
    kvs, wbf, oscr = refs[pos + 8:pos + 11]

    b = pl.program_id(0)
    qi = pl.program_id(1)
    lo_f, hi_f = _half_masks()
    lo_b, hi_b = lo_f.astype(BF16), hi_f.astype(BF16)

    @pl.when((b == 0) & (qi == 0))
    def _():
        wbf[...] = wout_ref[...].astype(BF16)

    @pl.when(qi == 0)
    def _():
        for sb in range(n_sub):
            new = slice(sb * n_new, (sb + 1) * n_new)
            for h in range(DIFF_HEADS):
                k = kv_ref[new, h * LANES:(h + 1) * LANES]
                c0 = h * KV_DIFF_STRIDE
                kvs[sb, 0:n_new, c0:c0 + LANES] = k * lo_b
                kvs[sb, 0:n_new, c0 + LANES:c0 + 2 * LANES] = k * hi_b
                kvs[sb, 0:n_new, c0 + 2 * LANES:c0 + 3 * LANES] = kv_ref[new, DIFF_WIDTH + h * LANES:
                                                                         DIFF_WIDTH + (h + 1) * LANES]
                if n_cache:
                    kc = cak_ref[:, h * LANES:(h + 1) * LANES]
                    kvs[sb, n_new:n_new + n_cache, c0:c0 + LANES] = (kc * lo_f).astype(BF16)
                    kvs[sb, n_new:n_new + n_cache, c0 + LANES:c0 + 2 * LANES] = (kc * hi_f).astype(BF16)
                    kvs[sb, n_new:n_new + n_cache, c0 + 2 * LANES:c0 + 3 * LANES] = (
                        cav_ref[:, h * LANES:(h + 1) * LANES].astype(BF16))
            kvs[sb, 0:n_new, KV_GQA_OFF:KV_GQA_OFF + 4 * LANES] = kv_ref[new, 2 * DIFF_WIDTH:
                                                                          2 * DIFF_WIDTH + 4 * LANES]
            if n_cache:
                ck = cbk_ref[...]
                cv = cbv_ref[...]
                rows = slice(n_new, n_new + n_cache)
                kvs[sb, rows, KV_GQA_OFF:KV_GQA_OFF + LANES] = ck.astype(BF16)
                kvs[sb, rows, KV_GQA_OFF + LANES:KV_GQA_OFF + 2 * LANES] = cv.astype(BF16)
                kvs[sb, rows, KV_GQA_OFF + 2 * LANES:KV_GQA_OFF + 3 * LANES] = (
                    pltpu.roll(ck, HEAD_DIM, axis=1).astype(BF16))
                kvs[sb, rows, KV_GQA_OFF + 3 * LANES:KV_GQA_OFF + 4 * LANES] = (
                    pltpu.roll(cv, HEAD_DIM, axis=1).astype(BF16))

    lp = lam_ref[...]
    lam = (jnp.exp(jnp.sum(lp[0:1] * lp[1:2], axis=-1, keepdims=True))
           - jnp.exp(jnp.sum(lp[2:3] * lp[3:4], axis=-1, keepdims=True)) + lam_init)

    def scores(q, k):
        return lax.dot_general(q, k, (((1,), (1,)), ((), ())), preferred_element_type=F32)

    def softmax_parts(s):
        e = jnp.exp(s - jnp.max(s, axis=-1, keepdims=True))
        return e, 1.0 / jnp.sum(e, axis=-1, keepdims=True)

    tq = q_ref.shape[0] // n_sub
    for sb in range(n_sub):
        qrows = slice(sb * tq, (sb + 1) * tq)
        for h in range(DIFF_HEADS):
            c0 = h * KV_DIFF_STRIDE
            q = q_ref[qrows, h * LANES:(h + 1) * LANES]
            e1, r1 = softmax_parts(scores(q, kvs[sb, :, c0:c0 + LANES]))
            e2, r2 = softmax_parts(scores(q, kvs[sb, :, c0 + LANES:c0 + 2 * LANES]))
            a = (e1 - e2 * (lam * r2 / r1)).astype(BF16)
            o = jnp.dot(a, kvs[sb, :, c0 + 2 * LANES:c0 + 3 * LANES], preferred_element_type=F32) * r1
            o = o * lax.rsqrt(jnp.mean(o * o, axis=-1, keepdims=True) + EPS) * subln_ref[...]
            oscr[qrows, h * LANES:(h + 1) * LANES] = (o * (1.0 - lam_init)).astype(BF16)

        for pair in range(GQA_HEADS // 2):
            q_pair = q_ref[qrows, DIFF_WIDTH + pair * LANES: DIFF_WIDTH + (pair + 1) * LANES]
            halves = []
            for c in range(2):
                kv_head = (2 * pair + c) // (GQA_HEADS // GQA_KV_HEADS)
                koff = KV_GQA_OFF if kv_head == c else KV_GQA_OFF + 2 * LANES
                e, r = softmax_parts(scores(q_pair * (lo_b if c == 0 else hi_b), kvs[sb, :, koff:koff + LANES]))
                o = jnp.dot(e.astype(BF16), kvs[sb, :, koff + LANES:koff + 2 * LANES],
                            preferred_element_type=F32)
                halves.append(o * r)
            o_pair = halves[0] * lo_f + halves[1] * hi_f
            oscr[qrows, DIFF_WIDTH + pair * LANES: DIFF_WIDTH + (pair + 1) * LANES] = o_pair.astype(BF16)

    mix = jnp.dot(oscr[...], wbf[...], preferred_element_type=F32)
    gate = mod_ref[:, 2 * D_MODEL:3 * D_MODEL]
    y = DEEPNORM_ALPHA * x_ref[...] + gate * mix
    o_ref[...] = _layer_norm(y, lng_ref[...], lnb_ref[...])


def _attn_call(q, kv, cache, x, x_row_off, mod3, mod_row_fn, w_out, diff_lambda, subln, ln_g, ln_b,
               layer, n_batch, n_new):
    nq = n_new // ATT_TQ
    n_sub = ATT_CTX_BATCHES if (nq == 1 and cache is None) else 1
    tq = ATT_TQ * n_sub
    n_batch = n_batch // n_sub
    x_blk_off = x_row_off // tq
    n_cache = PAST_LEN if cache is not None else 0
    lam_init = 0.8 - 0.6 * math.exp(-0.3 * layer)
    in_specs = [
        pl.BlockSpec((tq, 2 * DIFF_WIDTH), lambda b, i: (b * nq + i, 0)),
        pl.BlockSpec((n_new * n_sub, KV_NEW_COLS), lambda b, i: (b, 0)),
    ]
    args = [q, kv]
    if cache is not None:
        for arr in cache:
            width = arr.shape[-1]
            in_specs.append(pl.BlockSpec((None, None, PAST_LEN, width), lambda b, i: (b, layer, 0, 0)))
            args.append(arr)
    in_specs += [
        pl.BlockSpec((tq, D_MODEL), lambda b, i: (b * nq + i + x_blk_off, 0)),
        pl.BlockSpec((None, 1, N_MOD), lambda b, i: (mod_row_fn(b), 0, 0)),
        pl.BlockSpec((None, D_MODEL, D_MODEL), lambda b, i: (layer, 0, 0), pipeline_mode=pl.Buffered(1)),
        pl.BlockSpec((None, 4, HEAD_DIM), lambda b, i: (layer, 0, 0)),
        pl.BlockSpec((1, LANES), lambda b, i: (0, 0)),
        pl.BlockSpec((1, D_MODEL), lambda b, i: (0, 0)),
        pl.BlockSpec((1, D_MODEL), lambda b, i: (0, 0)),
    ]
    args += [x, mod3, w_out, diff_lambda, subln, ln_g, ln_b]
    return pl.pallas_call(
        functools.partial(_attn_kernel, n_new=n_new, n_cache=n_cache, n_sub=n_sub, lam_init=lam_init),
        grid=(n_batch, nq),
        in_specs=in_specs,
        out_specs=pl.BlockSpec((tq, D_MODEL), lambda b, i: (b * nq + i, 0)),
        out_shape=jax.ShapeDtypeStruct((n_batch * n_sub * n_new, D_MODEL), F32),
        scratch_shapes=[pltpu.VMEM((n_sub, n_new + n_cache, KV_PREP_COLS), BF16),
                        pltpu.VMEM((D_MODEL, D_MODEL), BF16),
                        pltpu.VMEM((tq, D_MODEL), BF16)],
        compiler_params=_params(2),
        name="token_mixer_latent" if cache is not None else "token_mixer_ctx",
    )(*args)


def _router_kernel(xp_ref, xs_ref, mod_ref, wr_ref, h_ref, meta_ref, meta_t_ref, cum_ref,
                   total_ref, carry, *, n_prompt_tiles):
    i = pl.program_id(0)

    @pl.when(i == 0)
    def _():
        carry[...] = jnp.zeros_like(carry)

    def run(x_ref):
        tm = x_ref.shape[0]
        shift = mod_ref[:, 3 * D_MODEL:4 * D_MODEL]
        scale = mod_ref[:, 4 * D_MODEL:5 * D_MODEL]
        h = x_ref[...] * (1.0 + scale) + shift
        h_ref[...] = h.astype(BF16)
        logits = _dot_3pass(h, wr_ref[...])
        lane = lax.broadcasted_iota(jnp.int32, logits.shape, 1).astype(F32)
        neg = jnp.float32(-jnp.inf)
        logits = jnp.where(lane < N_EXPERTS, logits, neg)
        m1 = jnp.max(logits, axis=-1, keepdims=True)
        i1 = jnp.min(jnp.where(logits == m1, lane, float(LANES)), axis=-1, keepdims=True)
        rest = jnp.where(lane == i1, neg, logits)
        m2 = jnp.max(rest, axis=-1, keepdims=True)
        i2 = jnp.min(jnp.where(rest == m2, lane, float(LANES)), axis=-1, keepdims=True)
        e2 = jnp.exp(m2 - m1)
        p1 = 1.0 / (1.0 + e2)
        p2 = e2 / (1.0 + e2)
        hit1 = lane == i1
        hit2 = lane == i2

        sel = jnp.where(hit1, 1.0, 0.0) + jnp.where(hit2, 1.0, 0.0)
        r = lax.broadcasted_iota(jnp.int32, (tm, tm), 0)
        c = lax.broadcasted_iota(jnp.int32, (tm, tm), 1)
        before = jnp.where(c < r, 1.0, 0.0).astype(BF16)
        cumx = jnp.dot(before, sel.astype(BF16), preferred_element_type=F32) + carry[0:1, :]
        rank1 = jnp.sum(jnp.where(hit1, cumx, 0.0), axis=-1, keepdims=True)
        rank2 = jnp.sum(jnp.where(hit2, cumx, 0.0), axis=-1, keepdims=True)
        meta = jnp.zeros_like(logits)
        for k, val in enumerate((i1, i2, rank1, rank2, p1, p2)):
            meta = jnp.where(lane == float(k), val, meta)
        meta_ref[...] = meta
        meta_t_ref[...] = jnp.transpose(meta)[0:8, :]
        for k in range(tm // MOE_HALF):
            cum_ref[k] = jnp.broadcast_to(cumx[k * MOE_HALF:k * MOE_HALF + 1, :], (8, LANES))
        new_carry = carry[0:1, :] + jnp.sum(sel, axis=0, keepdims=True)
        carry[...] = jnp.broadcast_to(new_carry, carry.shape)
        total_ref[...] = jnp.broadcast_to(new_carry, total_ref.shape)

    @pl.when(i < n_prompt_tiles)
    def _():
        run(xp_ref)

    @pl.when(i >= n_prompt_tiles)
    def _():
        run(xs_ref)


def _router_call(xp, xs, mod3, mod_row_fn, w_router_pad):
    tm = FFN_TM
    npt = N_PROMPT // tm
    halves = tm // MOE_HALF
    return pl.pallas_call(
        functools.partial(_router_kernel, n_prompt_tiles=npt),
        grid=(N_TOK // tm,),
        in_specs=[
            pl.BlockSpec((tm, D_MODEL), lambda i: (jnp.minimum(i, npt - 1), 0)),
            pl.BlockSpec((tm, D_MODEL), lambda i: (jnp.maximum(i - npt, 0), 0)),
            pl.BlockSpec((None, 1, N_MOD), lambda i: (mod_row_fn(i), 0, 0)),
            pl.BlockSpec((D_MODEL, LANES), lambda i: (0, 0)),
        ],
        out_specs=[
            pl.BlockSpec((tm, D_MODEL), lambda i: (i, 0)),
            pl.BlockSpec((tm, LANES), lambda i: (i, 0)),
            pl.BlockSpec((8, tm), lambda i: (0, i)),
            pl.BlockSpec((halves, 8, LANES), lambda i: (i, 0, 0)),
            pl.BlockSpec((8, LANES), lambda i: (0, 0)),
        ],
        out_shape=[
            jax.ShapeDtypeStruct((N_TOK, D_MODEL), BF16),
            jax.ShapeDtypeStruct((N_TOK, LANES), F32),
            jax.ShapeDtypeStruct((8, N_TOK), F32),
            jax.ShapeDtypeStruct((N_TOK // MOE_HALF, 8, LANES), F32),
            jax.ShapeDtypeStruct((8, LANES), F32),
        ],
        scratch_shapes=[pltpu.VMEM((8, LANES), F32)],
        compiler_params=_params(1),
        name="router",
    )(xp, xs, mod3, w_router_pad)


def _ffn_kernel(xp_ref, xs_ref, mod_ref, wg_ref, wu_ref, wd_ref, lng_ref, lnb_ref, o_ref, hscr, acc, *,
                n_prompt_tiles):
    i = pl.program_id(0)
    j = pl.program_id(1)
    first = j == 0
    last = j == pl.num_programs(1) - 1

    def modulate(x_ref):
        shift = mod_ref[:, 3 * D_MODEL:4 * D_MODEL]
        scale = mod_ref[:, 4 * D_MODEL:5 * D_MODEL]
        hscr[...] = (x_ref[...] * (1.0 + scale) + shift).astype(BF16)
        acc[...] = jnp.zeros_like(acc)

    @pl.when(first & (i < n_prompt_tiles))
    def _():
        modulate(xp_ref)

    @pl.when(first & (i >= n_prompt_tiles))
    def _():
        modulate(xs_ref)

    h = hscr[...]
    g = jnp.dot(h, wg_ref[...].astype(BF16), preferred_element_type=F32)
    u = jnp.dot(h, wu_ref[...].astype(BF16), preferred_element_type=F32)
    a = _silu(g) * u
    acc[...] += jnp.dot(a.astype(BF16), wd_ref[...].astype(BF16), preferred_element_type=F32)

    def finish(x_ref):
        gate = mod_ref[:, 5 * D_MODEL:6 * D_MODEL]
        y = DEEPNORM_ALPHA * x_ref[...] + gate * acc[...]
        o_ref[...] = _layer_norm(y, lng_ref[...], lnb_ref[...])

    @pl.when(last & (i < n_prompt_tiles))
    def _():
        finish(xp_ref)

    @pl.when(last & (i >= n_prompt_tiles))
    def _():
        finish(xs_ref)


def _ffn_call(xp, xs, mod3, mod_row_fn, wg, wu, wd, layer_idx, ln_g, ln_b):
    tm, tf = FFN_TM, FFN_TF
    npt = N_PROMPT // tm
    return pl.pallas_call(
        functools.partial(_ffn_kernel, n_prompt_tiles=npt),
        grid=(N_TOK // tm, D_FF // tf),
        in_specs=[
            pl.BlockSpec((tm, D_MODEL), lambda i, j: (jnp.minimum(i, npt - 1), 0)),
            pl.BlockSpec((tm, D_MODEL), lambda i, j: (jnp.maximum(i - npt, 0), 0)),
            pl.BlockSpec((None, 1, N_MOD), lambda i, j: (mod_row_fn(i), 0, 0)),
            pl.BlockSpec((None, D_MODEL, tf), lambda i, j: (layer_idx, 0, j)),
            pl.BlockSpec((None, D_MODEL, tf), lambda i, j: (layer_idx, 0, j)),
            pl.BlockSpec((None, tf, D_MODEL), lambda i, j: (layer_idx, j, 0)),
            pl.BlockSpec((1, D_MODEL), lambda i, j: (0, 0)),
            pl.BlockSpec((1, D_MODEL), lambda i, j: (0, 0)),
        ],
        out_specs=pl.BlockSpec((tm, D_MODEL), lambda i, j: (i, 0)),
        out_shape=jax.ShapeDtypeStruct((N_TOK, D_MODEL), F32),
        scratch_shapes=[pltpu.VMEM((tm, D_MODEL), BF16), pltpu.VMEM((tm, D_MODEL), F32)],
        compiler_params=_params(2),
        name="channel_mixer_dense",
    )(xp, xs, mod3, wg, wu, wd, ln_g, ln_b)


VISIT_FIELDS = 8
NO_ROW = -1.0e9


def _moe_ffn_kernel(vinfo, gwin, h_ref, mt_ref, wg_ref, wu_ref, wd_ref, y_ref, hs, gate_s, acc):
    v = pl.program_id(0)
    j = pl.program_id(1)
    tile = vinfo[v * VISIT_FIELDS + 0]
    expert = vinfo[v * VISIT_FIELDS + 1]
    valid = vinfo[v * VISIT_FIELDS + 2] == 1
    row_lo = vinfo[v * VISIT_FIELDS + 3]
    row_hi = vinfo[v * VISIT_FIELDS + 4]
    expert_row0 = vinfo[v * VISIT_FIELDS + 5]
    n_g = MOE_GATHER_CHUNKS

    def sub_rows(s):
        return slice(s * MOE_CHUNK, (s + 1) * MOE_CHUNK)

    def active(s):
        return (row_lo < (s + 1) * MOE_CHUNK) & (row_hi > s * MOE_CHUNK)

    def owns_start(s):
        return row_lo <= s * MOE_CHUNK

    @pl.when(valid & (j == 0))
    def _():
        row_iota = lax.broadcasted_iota(jnp.int32, (MOE_CHUNK, MOE_CHUNK), 0).astype(F32)
        expert_f = expert.astype(F32)
        for s in range(MOE_SUBS):
            rows = sub_rows(s)

            @pl.when(active(s) & owns_start(s))
            def _():
                hs[rows, :] = jnp.zeros((MOE_CHUNK, D_MODEL), BF16)
                gate_s[rows, :] = jnp.zeros((MOE_CHUNK, 1), F32)

            @pl.when(active(s))
            def _():
                acc[rows, :] = jnp.zeros((MOE_CHUNK, D_MODEL), F32)
                first_chunk = gwin[(v * MOE_SUBS + s) * 2]
                rank0 = (tile * MOE_TM + s * MOE_CHUNK - expert_row0).astype(F32)

                def body(w, carry):
                    want = first_chunk + w * n_g
                    cs = jnp.minimum(want, MOE_NCHUNK - n_g)
                    pieces = []
                    gate = jnp.zeros((MOE_CHUNK, 1), F32)
                    for k in range(n_g):
                        mt = mt_ref[cs + k]
                        first = mt[0:1] == expert_f
                        second = mt[1:2] == expert_f
                        rank = jnp.where(first, mt[2:3], jnp.where(second, mt[3:4], NO_ROW))
                        prob = jnp.where(first, mt[4:5], jnp.where(second, mt[5:6], 0.0))
                        rank = rank + jnp.where(cs + k >= want, 0.0, NO_ROW)
                        match = row_iota + rank0 == rank
                        pieces.append(jnp.where(match, 1.0, 0.0).astype(BF16))
                        gate = gate + jnp.sum(jnp.where(match, prob, 0.0), axis=-1, keepdims=True)
                    onehot = jnp.concatenate(pieces, axis=1)
                    start = pl.multiple_of(cs * MOE_CHUNK, MOE_CHUNK)
                    part = jnp.dot(onehot, h_ref[pl.ds(start, n_g * MOE_CHUNK), :], preferred_element_type=F32)
                    hs[rows, :] = hs[rows, :] + part.astype(BF16)
                    gate_s[rows, :] = gate_s[rows, :] + gate
                    return carry

                lax.fori_loop(0, gwin[(v * MOE_SUBS + s) * 2 + 1], body, 0)

    @pl.when(valid)
    def _():
        first_sub = row_lo // MOE_CHUNK
        n_active = (row_hi + MOE_CHUNK - 1) // MOE_CHUNK - first_sub
        for n in range(1, MOE_SUBS + 1):
            @pl.when(n_active == n)
            def _():
                rows = pl.ds(pl.multiple_of(first_sub * MOE_CHUNK, MOE_CHUNK), n * MOE_CHUNK)
                h = hs[rows, :]
                g = jnp.dot(h, wg_ref[...].astype(BF16), preferred_element_type=F32)
                u = jnp.dot(h, wu_ref[...].astype(BF16), preferred_element_type=F32)
                a = _silu(g) * u * gate_s[rows, :]
                acc[rows, :] += jnp.dot(a.astype(BF16), wd_ref[...].astype(BF16), preferred_element_type=F32)

    @pl.when(valid & (j == pl.num_programs(1) - 1))
    def _():
        for s in range(MOE_SUBS):
            rows = sub_rows(s)
            row = lax.broadcasted_iota(jnp.int32, (MOE_CHUNK, 1), 0) + s * MOE_CHUNK
            mine = jnp.where(row >= row_lo, 1.0, 0.0) * jnp.where(row < row_hi, 1.0, 0.0)

            @pl.when(active(s) & owns_start(s))
            def _():
                y_ref[rows, :] = (acc[rows, :] * mine).astype(BF16)

            @pl.when(active(s) & jnp.logical_not(owns_start(s)))
            def _():
                y_ref[rows, :] = jnp.where(mine > 0.0, acc[rows, :], y_ref[rows, :].astype(F32)).astype(BF16)


def _moe_ffn_call(vinfo, gwin, h, meta_chunks, wg, wu, wd):
    tm, tf = MOE_TM, FFN_TF
    n_j = D_FF // tf

    def expert_of(v, vinfo):
        return vinfo[v * VISIT_FIELDS + 1]

    def w_col(v, j, vinfo):
        return jnp.where(vinfo[v * VISIT_FIELDS + 2] == 1, j, n_j - 1)

    grid_spec = pltpu.PrefetchScalarGridSpec(
        num_scalar_prefetch=2,
        grid=(MOE_MAX_VISITS, n_j),
        in_specs=[
            pl.BlockSpec((N_TOK, D_MODEL), lambda v, j, vinfo, gwin: (0, 0), pipeline_mode=pl.Buffered(1)),
            pl.BlockSpec((MOE_NCHUNK, 8, MOE_CHUNK), lambda v, j, vinfo, gwin: (0, 0, 0),
                         pipeline_mode=pl.Buffered(1)),
            pl.BlockSpec((None, D_MODEL, tf),
                         lambda v, j, vinfo, gwin: (expert_of(v, vinfo), 0, w_col(v, j, vinfo))),
            pl.BlockSpec((None, D_MODEL, tf),
                         lambda v, j, vinfo, gwin: (expert_of(v, vinfo), 0, w_col(v, j, vinfo))),
            pl.BlockSpec((None, tf, D_MODEL),
                         lambda v, j, vinfo, gwin: (expert_of(v, vinfo), w_col(v, j, vinfo), 0)),
        ],
        out_specs=pl.BlockSpec((tm, D_MODEL), lambda v, j, vinfo, gwin: (vinfo[v * VISIT_FIELDS], 0)),
        scratch_shapes=[pltpu.VMEM((tm, D_MODEL), BF16), pltpu.VMEM((tm, 1), F32),
                        pltpu.VMEM((tm, D_MODEL), F32)],
    )
    return pl.pallas_call(
        _moe_ffn_kernel,
        grid_spec=grid_spec,
        out_shape=jax.ShapeDtypeStruct((MOE_ROWS, D_MODEL), BF16),
        compiler_params=_params(2),
        name="channel_mixer_experts",
    )(vinfo, gwin, h, meta_chunks, wg, wu, wd)


def _combine_kernel(cstart, row0, xp_ref, xs_ref, meta_ref, y_ref, mod_ref, lng_ref, lnb_ref, op_ref, os_ref, *,
                    n_prompt_tiles):
    c = pl.program_id(0)

    def run(x_ref, o_ref):
        gate = mod_ref[:, 5 * D_MODEL:6 * D_MODEL]
        col = lax.broadcasted_iota(jnp.int32, (MOE_HALF, MOE_WINDOW), 1).astype(F32)
        for half in range(MOE_CHUNK // MOE_HALF):
            rows = slice(half * MOE_HALF, (half + 1) * MOE_HALF)
            meta = meta_ref[rows, :]
            e1, e2, r1, r2 = meta[:, 0:1], meta[:, 1:2], meta[:, 2:3], meta[:, 3:4]
            total = None
            for e in range(N_EXPERTS):
                start = pl.multiple_of(cstart[(c * 2 + half) * N_EXPERTS + e], MOE_HALF)
                rank = jnp.where(e1 == float(e), r1, jnp.where(e2 == float(e), r2, NO_ROW))
                onehot = jnp.where(col == rank + (row0[e] - start).astype(F32), 1.0, 0.0).astype(BF16)
                part = jnp.dot(onehot, y_ref[pl.ds(start, MOE_WINDOW), :], preferred_element_type=F32)
                total = part if total is None else total + part
            y = DEEPNORM_ALPHA * x_ref[rows, :] + gate * total
            o_ref[rows, :] = _layer_norm(y, lng_ref[...], lnb_ref[...])

    @pl.when(c < n_prompt_tiles)
    def _():
        run(xp_ref, op_ref)

    @pl.when(c >= n_prompt_tiles)
    def _():
        run(xs_ref, os_ref)


def _combine_call(cstart, row0, xp, xs, meta, y, mod3, mod_row_fn, ln_g, ln_b):
    tm = MOE_CHUNK
    npt = N_PROMPT // tm
    grid_spec = pltpu.PrefetchScalarGridSpec(
        num_scalar_prefetch=2,
        grid=(MOE_NCHUNK,),
        in_specs=[
            pl.BlockSpec((tm, D_MODEL), lambda c, a, b: (jnp.minimum(c, npt - 1), 0)),
            pl.BlockSpec((tm, D_MODEL), lambda c, a, b: (jnp.maximum(c - npt, 0), 0)),
            pl.BlockSpec((tm, LANES), lambda c, a, b: (c, 0)),
            pl.BlockSpec((MOE_ROWS, D_MODEL), lambda c, a, b: (0, 0), pipeline_mode=pl.Buffered(1)),
            pl.BlockSpec((None, 1, N_MOD), lambda c, a, b: (mod_row_fn(c), 0, 0)),
            pl.BlockSpec((1, D_MODEL), lambda c, a, b: (0, 0)),
            pl.BlockSpec((1, D_MODEL), lambda c, a, b: (0, 0)),
        ],
        out_specs=[
            pl.BlockSpec((tm, D_MODEL), lambda c, a, b: (jnp.minimum(c, npt - 1), 0)),
            pl.BlockSpec((tm, D_MODEL), lambda c, a, b: (jnp.maximum(c - npt, 0), 0)),
        ],
    )
    return pl.pallas_call(
        functools.partial(_combine_kernel, n_prompt_tiles=npt),
        grid_spec=grid_spec,
        out_shape=[jax.ShapeDtypeStruct((N_PROMPT, D_MODEL), F32),
                   jax.ShapeDtypeStruct((N_SAMPLE, D_MODEL), F32)],
        compiler_params=_params(1),
        name="expert_combine",
    )(cstart, row0, xp, xs, meta, y, mod3, ln_g, ln_b)


def _moe_plan(cum, total):
    i32 = jnp.int32
    cnt = total[0, :N_EXPERTS].astype(i32)
    off = jnp.cumsum(cnt) - cnt
    cumh = jnp.concatenate([cum[:, 0, :N_EXPERTS], total[0:1, :N_EXPERTS]], axis=0).astype(i32)
    cumc = cumh[::MOE_CHUNK // MOE_HALF]

    n_tiles = MOE_ROWS // MOE_TM
    t0 = jnp.arange(n_tiles, dtype=i32)[:, None] * MOE_TM
    lo = jnp.maximum(t0, off[None, :]).reshape(-1)
    hi = jnp.minimum(t0 + MOE_TM, (off + cnt)[None, :]).reshape(-1)
    ok = hi > lo
    n_visits = jnp.sum(ok.astype(i32))
    order = jnp.argsort(jnp.logical_not(ok), stable=True)[:MOE_MAX_VISITS].astype(i32)
    slot = jnp.arange(MOE_MAX_VISITS, dtype=i32)
    valid = slot < n_visits
    order = order[jnp.minimum(slot, n_visits - 1)]
    vt, ve = order // N_EXPERTS, order % N_EXPERTS
    vlo = jnp.where(valid, lo[order] - vt * MOE_TM, 0)
    vhi = jnp.where(valid, hi[order] - vt * MOE_TM, 0)
    zero = jnp.zeros_like(vt)
    vinfo = jnp.stack([vt, ve, valid.astype(i32), vlo, vhi, off[ve], zero, zero], axis=1).reshape(-1).astype(i32)

    s0 = jnp.arange(MOE_SUBS, dtype=i32)[None, :] * MOE_CHUNK
    rlo = jnp.maximum(vlo[:, None], s0)
    rhi = jnp.minimum(vhi[:, None], s0 + MOE_CHUNK)
    to_rank = (vt * MOE_TM - off[ve])[:, None]
    cum_v = cumc[:, ve]
    c_lo = jnp.sum((cum_v[:, :, None] <= (rlo + to_rank)[None]).astype(i32), axis=0) - 1
    c_hi = jnp.sum((cum_v[:, :, None] < (rhi + to_rank)[None]).astype(i32), axis=0) - 1
    c_lo = jnp.clip(c_lo, 0, MOE_NCHUNK - 1)
    c_hi = jnp.clip(c_hi, 0, MOE_NCHUNK - 1)
    n_win = jnp.where(rhi <= rlo, 0, (c_hi - c_lo) // MOE_GATHER_CHUNKS + 1)
    gwin = jnp.stack([c_lo, n_win], axis=-1).reshape(-1).astype(i32)

    seg_lo = off[None, :] + cumh[:-1]
    cstart = jnp.clip((seg_lo // MOE_HALF) * MOE_HALF, 0, MOE_ROWS - MOE_WINDOW)
    return vinfo, gwin, cstart.reshape(-1).astype(i32), off.astype(i32)


def _rope_tables():
    rows = DEC_SEQ // GRID_W
    row = jnp.repeat(jnp.arange(rows, dtype=F32), GRID_W)
    col = jnp.tile(jnp.arange(GRID_W, dtype=F32), rows)
    n_freq = HEAD_DIM // 4
    inv = ROPE_THETA ** (-jnp.arange(n_freq, dtype=F32) / n_freq)
    ar = row[:, None] * inv
    ac = col[:, None] * inv
    ang = jnp.concatenate([ar, ar, ac, ac], axis=-1)
    cos = jnp.tile(jnp.cos(ang), (1, LANES // HEAD_DIM))
    sin = jnp.tile(jnp.sin(ang), (1, LANES // HEAD_DIM))
    first_half = (jnp.arange(LANES) % (2 * n_freq)) < n_freq
    sin_next = jnp.where(first_half, -sin, 0.0)
    sin_prev = jnp.where(first_half, 0.0, sin)
    return cos, sin_next, sin_prev


def kernel(x_prompt, x_sample, cache_a_k, cache_a_v, cache_b_k, cache_b_v, c, c_ctx, w_ada, b_ada, w_in,
           w_out, diff_lambda, diff_subln, qk_norm_gain, ln_gain, ln_bias, w_ffn_gate, w_ffn_up,
           w_ffn_down, w_router, w_moe_gate, w_moe_up, w_moe_down):
    cond = jnp.zeros((MOD_ROWS, D_MODEL), F32).at[0].set(c_ctx).at[1:1 + DEC_BATCH].set(c)
    mod3 = _ada_call(cond, w_ada, b_ada).reshape(DEPTH * MOD_ROWS, 1, N_MOD)
    rope_tabs = _rope_tables()
    cache = (cache_a_k.reshape(DEC_BATCH, DEPTH, PAST_LEN, DIFF_WIDTH),
             cache_a_v.reshape(DEC_BATCH, DEPTH, PAST_LEN, DIFF_WIDTH),
             cache_b_k.reshape(DEC_BATCH, DEPTH, PAST_LEN, KV_WIDTH),
             cache_b_v.reshape(DEC_BATCH, DEPTH, PAST_LEN, KV_WIDTH))

    xp = x_prompt.reshape(N_PROMPT, D_MODEL)
    xs = x_sample.reshape(N_SAMPLE, D_MODEL)
    xs_off = 0
    new_caches = []
    for l in range(DEPTH):
        base = l * MOD_ROWS
        gq = jnp.tile(qk_norm_gain[l, 0], LANES // HEAD_DIM).reshape(1, LANES)
        gk = jnp.tile(qk_norm_gain[l, 1], LANES // HEAD_DIM).reshape(1, LANES)
        subln = diff_subln[l].reshape(1, LANES)
        ln1_g, ln1_b = ln_gain[l, 0].reshape(1, D_MODEL), ln_bias[l, 0].reshape(1, D_MODEL)
        ln2_g, ln2_b = ln_gain[l, 1].reshape(1, D_MODEL), ln_bias[l, 1].reshape(1, D_MODEL)

        q_p, kv_p, ak, av, bk, bv = _inproj_call(
            xp, 0, N_PROMPT, mod3, lambda i: base, w_in, gq, gk, l, None, True)
        new_caches.append((ak, av, bk, bv))
        x1_p = _attn_call(q_p, kv_p, None, xp, 0, mod3, lambda b: base, w_out, diff_lambda, subln,
                          ln1_g, ln1_b, l, BATCH, SEQ)
        s_tiles = DEC_SEQ // PROJ_TM
        q_s, kv_s = _inproj_call(
            xs, xs_off, N_SAMPLE, mod3, lambda i: base + 1 + i // s_tiles, w_in, gq, gk, l, rope_tabs, False)
        x1_s = _attn_call(q_s, kv_s, cache, xs, xs_off, mod3, lambda b: base + 1 + b, w_out, diff_lambda,
                          subln, ln1_g, ln1_b, l, DEC_BATCH, DEC_SEQ)

        npt = N_PROMPT // FFN_TM
        s_ffn_tiles = DEC_SEQ // FFN_TM
        mod_row = lambda i: base + jnp.where(i < npt, 0, 1 + (i - npt) // s_ffn_tiles)
        k = l // 2
        if l % 2 == 0:
            x2 = _ffn_call(x1_p, x1_s, mod3, mod_row, w_ffn_gate, w_ffn_up, w_ffn_down, k, ln2_g, ln2_b)
        else:
            wr = jnp.zeros((D_MODEL, LANES), F32).at[:, :N_EXPERTS].set(w_router[k])
            h2, meta, meta_t, cum, total = _router_call(x1_p, x1_s, mod3, mod_row, wr)
            vinfo, gwin, cstart, row0 = _moe_plan(cum, total)
            meta_chunks = meta_t.reshape(8, MOE_NCHUNK, MOE_CHUNK).transpose(1, 0, 2)
            y = _moe_ffn_call(vinfo, gwin, h2, meta_chunks, w_moe_gate[k], w_moe_up[k], w_moe_down[k])
            npc = N_PROMPT // MOE_CHUNK
            s_chunks = DEC_SEQ // MOE_CHUNK
            mod_row_c = lambda c: base + jnp.where(c < npc, 0, 1 + (c - npc) // s_chunks)
            xp, xs = _combine_call(cstart, row0, x1_p, x1_s, meta, y, mod3, mod_row_c, ln2_g, ln2_b)
            xs_off = 0
            continue
        xp, xs, xs_off = x2, x2, N_PROMPT

    y_prompt = xp[:N_PROMPT].reshape(BATCH, SEQ, D_MODEL)
    y_sample = xs[xs_off:xs_off + N_SAMPLE].reshape(DEC_BATCH, DEC_SEQ, D_MODEL)
    stack = lambda idx, shape: jnp.stack([nc[idx] for nc in new_caches], axis=0).reshape(
        (DEPTH, BATCH, SEQ) + shape).transpose((1, 0, 2) + tuple(range(3, 3 + len(shape))))
    return (y_prompt, y_sample,
            stack(0, (DIFF_HEADS, 2, HEAD_DIM)), stack(1, (DIFF_HEADS, 2 * HEAD_DIM)),
            stack(2, (GQA_KV_HEADS, HEAD_DIM)), stack(3, (GQA_KV_HEADS, HEAD_DIM)))
```

```python
import functools
import math

import jax
import jax.numpy as jnp
from jax import lax
from jax.experimental import pallas as pl
from jax.experimental.pallas import tpu as pltpu

D_MODEL = 1024
BATCH = 16
SEQ = 256
DEPTH = 2
DEC_BATCH = 2
DEC_SEQ = 2048
PAST_LEN = 256
GRID_W = 64
HEAD_DIM = 64
DIFF_HEADS = 4
GQA_HEADS = 8
GQA_KV_HEADS = 2
DIFF_WIDTH = DIFF_HEADS * 2 * HEAD_DIM
GQA_WIDTH = GQA_HEADS * HEAD_DIM
KV_WIDTH = GQA_KV_HEADS * HEAD_DIM
IN_COLS = 3 * DIFF_WIDTH + GQA_WIDTH + 2 * KV_WIDTH
D_FF = 2816
N_EXPERTS = 8
ROPE_THETA = 10000.0
EPS = 1e-6
DEEPNORM_ALPHA = (2 * DEPTH) ** 0.25

N_PROMPT = BATCH * SEQ
N_SAMPLE = DEC_BATCH * DEC_SEQ
N_TOK = N_PROMPT + N_SAMPLE
N_MOD = 6 * D_MODEL
MOD_ROWS = 8

LANES = 128
VMEM_LIMIT = 56 * 1024 * 1024

ADA_TN = 1536
PROJ_TM = 512
ATT_TQ = 256
ATT_CTX_BATCHES = 2
FFN_TM = 1024
FFN_TF = 256
MOE_TM = 2048
MOE_CHUNK = 256
MOE_HALF = MOE_CHUNK // 2
MOE_WINDOW = 2 * MOE_HALF
MOE_GATHER_CHUNKS = 6
MOE_ROWS = 2 * N_TOK
MOE_NCHUNK = N_TOK // MOE_CHUNK
MOE_SUBS = MOE_TM // MOE_CHUNK
MOE_MAX_VISITS = MOE_ROWS // MOE_TM + N_EXPERTS - 1

KV_DIFF_STRIDE = 3 * LANES
KV_GQA_OFF = DIFF_HEADS * KV_DIFF_STRIDE
KV_PREP_COLS = KV_GQA_OFF + 4 * LANES
KV_NEW_COLS = 2 * DIFF_WIDTH + 4 * KV_WIDTH

F32 = jnp.float32
BF16 = jnp.bfloat16


def _params(n_axes):
    return pltpu.CompilerParams(dimension_semantics=("arbitrary",) * n_axes,
                                vmem_limit_bytes=VMEM_LIMIT)


def _layer_norm(y, g, b):
    mu = jnp.mean(y, axis=-1, keepdims=True)
    yc = y - mu
    var = jnp.mean(yc * yc, axis=-1, keepdims=True)
    return yc * lax.rsqrt(var + EPS) * g + b


def _silu(x):
    return x * jax.nn.sigmoid(x)


def _half_masks():
    lane = lax.broadcasted_iota(jnp.int32, (1, LANES), 1)
    lo = (lane < HEAD_DIM).astype(F32)
    return lo, 1.0 - lo


def _split_bf16(x):
    hi = x.astype(BF16)
    return hi, (x - hi.astype(F32)).astype(BF16)


def _dot_3pass(x, w):
    x_hi, x_lo = _split_bf16(x)
    w_hi, w_lo = _split_bf16(w)
    return (jnp.dot(x_hi, w_hi, preferred_element_type=F32) + jnp.dot(x_lo, w_hi, preferred_element_type=F32)
            + jnp.dot(x_hi, w_lo, preferred_element_type=F32))


def _ada_kernel(cond_ref, w_ref, b_ref, o_ref):
    o_ref[...] = _dot_3pass(_silu(cond_ref[...]), w_ref[...]) + b_ref[...]


def _ada_call(cond, w_ada, b_ada):
    return pl.pallas_call(
        _ada_kernel,
        grid=(DEPTH, N_MOD // ADA_TN),
        in_specs=[
            pl.BlockSpec((MOD_ROWS, D_MODEL), lambda l, n: (0, 0)),
            pl.BlockSpec((None, D_MODEL, ADA_TN), lambda l, n: (l, 0, n)),
            pl.BlockSpec((None, 1, ADA_TN), lambda l, n: (l, 0, n)),
        ],
        out_specs=pl.BlockSpec((None, MOD_ROWS, ADA_TN), lambda l, n: (l, 0, n)),
        out_shape=jax.ShapeDtypeStruct((DEPTH, MOD_ROWS, N_MOD), F32),
        compiler_params=_params(2),
        name="ada_modulation",
    )(cond, w_ada, b_ada.reshape(DEPTH, 1, N_MOD))


def _head_sumsq(x):
    r = lax.broadcasted_iota(jnp.int32, (LANES, LANES), 0) // HEAD_DIM
    c = lax.broadcasted_iota(jnp.int32, (LANES, LANES), 1) // HEAD_DIM
    ones = (r == c).astype(BF16)
    sq = x * x
    hi = sq.astype(BF16)
    lo = (sq - hi.astype(F32)).astype(BF16)
    return (jnp.dot(hi, ones, preferred_element_type=F32)
            + jnp.dot(lo, ones, preferred_element_type=F32))


def _inproj_kernel(*refs, rope, caches):
    x_ref, mod_ref, w_ref, gq_ref, gk_ref = refs[:5]
    pos = 5
    if rope:
        cos_ref, sa_ref, sb_ref = refs[pos:pos + 3]
        pos += 3
    q_out, kv_out = refs[pos:pos + 2]
    pos += 2
    if caches:
        ak_out, av_out, bk_out, bv_out = refs[pos:pos + 4]
        pos += 4
    wbf = refs[pos]

    @pl.when(pl.program_id(0) == 0)
    def _():
        wbf[...] = w_ref[...].astype(BF16)

    shift = mod_ref[:, 0:D_MODEL]
    scale = mod_ref[:, D_MODEL:2 * D_MODEL]
    h = (x_ref[...] * (1.0 + scale) + shift).astype(BF16)
    proj = jnp.dot(h, wbf[...], preferred_element_type=F32)

    def group(base, g):
        return proj[:, base + g * LANES: base + (g + 1) * LANES]

    def rot(v):
        if not rope:
            return v
        return (v * cos_ref[...] + pltpu.roll(v, LANES - HEAD_DIM // 4, axis=1) * sa_ref[...]
                + pltpu.roll(v, HEAD_DIM // 4, axis=1) * sb_ref[...])

    def normed(v, gain):
        return v * lax.rsqrt(_head_sumsq(v) * (1.0 / HEAD_DIM) + EPS) * gain

    qk_scale = HEAD_DIM ** -0.5
    off_ak, off_av, off_bq = DIFF_WIDTH, 2 * DIFF_WIDTH, 3 * DIFF_WIDTH
    off_bk, off_bv = off_bq + GQA_WIDTH, off_bq + GQA_WIDTH + KV_WIDTH

    for g in range(DIFF_WIDTH // LANES):
        q_out[:, g * LANES:(g + 1) * LANES] = (rot(group(0, g)) * qk_scale).astype(BF16)
        a_k = group(off_ak, g)
        a_v = group(off_av, g)
        if caches:
            ak_out[:, g * LANES:(g + 1) * LANES] = a_k
            av_out[pl.ds(g, a_v.shape[0], stride=DIFF_HEADS), :] = a_v
        kv_out[:, g * LANES:(g + 1) * LANES] = rot(a_k).astype(BF16)
        kv_out[:, DIFF_WIDTH + g * LANES: DIFF_WIDTH + (g + 1) * LANES] = a_v.astype(BF16)
    for g in range(GQA_WIDTH // LANES):
        b_q = rot(normed(group(off_bq, g), gq_ref[...]))
        q_out[:, DIFF_WIDTH + g * LANES: DIFF_WIDTH + (g + 1) * LANES] = (b_q * qk_scale).astype(BF16)
    b_k = normed(group(off_bk, 0), gk_ref[...])
    b_v = group(off_bv, 0)
    if caches:
        bk_out[...] = b_k
        bv_out[...] = b_v
    b_k = rot(b_k)
    base = 2 * DIFF_WIDTH
    kv_out[:, base:base + LANES] = b_k.astype(BF16)
    kv_out[:, base + LANES:base + 2 * LANES] = b_v.astype(BF16)
    kv_out[:, base + 2 * LANES:base + 3 * LANES] = pltpu.roll(b_k, HEAD_DIM, axis=1).astype(BF16)
    kv_out[:, base + 3 * LANES:base + 4 * LANES] = pltpu.roll(b_v, HEAD_DIM, axis=1).astype(BF16)


def _inproj_call(x, row_off, n_rows, mod3, mod_row_fn, w_in, gq, gk, layer, rope_tabs, caches):
    tm = PROJ_TM
    n_tiles = n_rows // tm
    blk_off = row_off // tm
    rope = rope_tabs is not None
    in_specs = [
        pl.BlockSpec((tm, D_MODEL), lambda i: (i + blk_off, 0)),
        pl.BlockSpec((None, 1, N_MOD), lambda i: (mod_row_fn(i), 0, 0)),
        pl.BlockSpec((None, D_MODEL, IN_COLS), lambda i: (layer, 0, 0)),
        pl.BlockSpec((1, LANES), lambda i: (0, 0)),
        pl.BlockSpec((1, LANES), lambda i: (0, 0)),
    ]
    args = [x, mod3, w_in, gq, gk]
    if rope:
        pos_tiles = DEC_SEQ // tm
        for t in rope_tabs:
            in_specs.append(pl.BlockSpec((tm, LANES), lambda i: (i % pos_tiles, 0)))
            args.append(t)
    out_shape = [jax.ShapeDtypeStruct((n_rows, 2 * DIFF_WIDTH), BF16),
                 jax.ShapeDtypeStruct((n_rows, KV_NEW_COLS), BF16)]
    out_specs = [pl.BlockSpec((tm, 2 * DIFF_WIDTH), lambda i: (i, 0)),
                 pl.BlockSpec((tm, KV_NEW_COLS), lambda i: (i, 0))]
    if caches:
        for rows_per_token, width in ((1, DIFF_WIDTH), (DIFF_HEADS, LANES), (1, KV_WIDTH), (1, KV_WIDTH)):
            out_shape.append(jax.ShapeDtypeStruct((n_rows * rows_per_token, width), F32))
            out_specs.append(pl.BlockSpec((tm * rows_per_token, width), lambda i: (i, 0)))
    return pl.pallas_call(
        functools.partial(_inproj_kernel, rope=rope, caches=caches),
        grid=(n_tiles,),
        in_specs=in_specs,
        out_specs=out_specs,
        out_shape=out_shape,
        scratch_shapes=[pltpu.VMEM((D_MODEL, IN_COLS), BF16)],
        compiler_params=_params(1),
        name="in_projection_rope" if rope else "in_projection_ctx",
    )(*args)


def _attn_kernel(*refs, n_new, n_cache, n_sub, lam_init):
    q_ref, kv_ref = refs[:2]
    pos = 2
    if n_cache:
        cak_ref, cav_ref, cbk_ref, cbv_ref = refs[pos:pos + 4]
        pos += 4
    x_ref, mod_ref, wout_ref, lam_ref, subln_ref, lng_ref, lnb_ref, o_ref = refs[pos:pos + 8]
    kvs, wbf, oscr = refs[pos + 8:pos + 11]

    b = pl.program_id(0)
    qi = pl.program_id(1)
    lo_f, hi_f = _half_masks()
    lo_b, hi_b = lo_f.astype(BF16), hi_f.astype(BF16)

    @pl.when((b == 0) & (qi == 0))
    def _():
        wbf[...] = wout_ref[...].astype(BF16)

    @pl.when(qi == 0)
    def _():
        for sb in range(n_sub):
            new = slice(sb * n_new, (sb + 1) * n_new)
            for h in range(DIFF_HEADS):
                k = kv_ref[new, h * LANES:(h + 1) * LANES]
                c0 = h * KV_DIFF_STRIDE
                kvs[sb, 0:n_new, c0:c0 + LANES] = k * lo_b
                kvs[sb, 0:n_new, c0 + LANES:c0 + 2 * LANES] = k * hi_b
                kvs[sb, 0:n_new, c0 + 2 * LANES:c0 + 3 * LANES] = kv_ref[new, DIFF_WIDTH + h * LANES:
                                                                         DIFF_WIDTH + (h + 1) * LANES]
                if n_cache:
                    kc = cak_ref[:, h * LANES:(h + 1) * LANES]
                    kvs[sb, n_new:n_new + n_cache, c0:c0 + LANES] = (kc * lo_f).astype(BF16)
                    kvs[sb, n_new:n_new + n_cache, c0 + LANES:c0 + 2 * LANES] = (kc * hi_f).astype(BF16)
                    kvs[sb, n_new:n_new + n_cache, c0 + 2 * LANES:c0 + 3 * LANES] = (
                        cav_ref[:, h * LANES:(h + 1) * LANES].astype(BF16))
            kvs[sb, 0:n_new, KV_GQA_OFF:KV_GQA_OFF + 4 * LANES] = kv_ref[new, 2 * DIFF_WIDTH:
                                                                          2 * DIFF_WIDTH + 4 * LANES]
            if n_cache:
                ck = cbk_ref[...]
                cv = cbv_ref[...]
                rows = slice(n_new, n_new + n_cache)
                kvs[sb, rows, KV_GQA_OFF:KV_GQA_OFF + LANES] = ck.astype(BF16)
                kvs[sb, rows, KV_GQA_OFF + LANES:KV_GQA_OFF + 2 * LANES] = cv.astype(BF16)
                kvs[sb, rows, KV_GQA_OFF + 2 * LANES:KV_GQA_OFF + 3 * LANES] = (
                    pltpu.roll(ck, HEAD_DIM, axis=1).astype(BF16))
                kvs[sb, rows, KV_GQA_OFF + 3 * LANES:KV_GQA_OFF + 4 * LANES] = (
                    pltpu.roll(cv, HEAD_DIM, axis=1).astype(BF16))

    lp = lam_ref[...]
    lam = (jnp.exp(jnp.sum(lp[0:1] * lp[1:2], axis=-1, keepdims=True))
           - jnp.exp(jnp.sum(lp[2:3] * lp[3:4], axis=-1, keepdims=True)) + lam_init)

    def scores(q, k):
        return lax.dot_general(q, k, (((1,), (1,)), ((), ())), preferred_element_type=F32)

    def softmax_parts(s):
        e = jnp.exp(s - jnp.max(s, axis=-1, keepdims=True))
        return e, 1.0 / jnp.sum(e, axis=-1, keepdims=True)

    tq = q_ref.shape[0] // n_sub
    for sb in range(n_sub):
        qrows = slice(sb * tq, (sb + 1) * tq)
        for h in range(DIFF_HEADS):
            c0 = h * KV_DIFF_STRIDE
            q = q_ref[qrows, h * LANES:(h + 1) * LANES]
            e1, r1 = softmax_parts(scores(q, kvs[sb, :, c0:c0 + LANES]))
            e2, r2 = softmax_parts(scores(q, kvs[sb, :, c0 + LANES:c0 + 2 * LANES]))
            a = (e1 - e2 * (lam * r2 / r1)).astype(BF16)
            o = jnp.dot(a, kvs[sb, :, c0 + 2 * LANES:c0 + 3 * LANES], preferred_element_type=F32) * r1
            o = o * lax.rsqrt(jnp.mean(o * o, axis=-1, keepdims=True) + EPS) * subln_ref[...]
            oscr[qrows, h * LANES:(h + 1) * LANES] = (o * (1.0 - lam_init)).astype(BF16)

        for pair in range(GQA_HEADS // 2):
            q_pair = q_ref[qrows, DIFF_WIDTH + pair * LANES: DIFF_WIDTH + (pair + 1) * LANES]
            halves = []
            for c in range(2):
                kv_head = (2 * pair + c) // (GQA_HEADS // GQA_KV_HEADS)
                koff = KV_GQA_OFF if kv_head == c else KV_GQA_OFF + 2 * LANES
                e, r = softmax_parts(scores(q_pair * (lo_b if c == 0 else hi_b), kvs[sb, :, koff:koff + LANES]))
                o = jnp.dot(e.astype(BF16), kvs[sb, :, koff + LANES:koff + 2 * LANES],
                            preferred_element_type=F32)
                halves.append(o * r)
            o_pair = halves[0] * lo_f + halves[1] * hi_f
            oscr[qrows, DIFF_WIDTH + pair * LANES: DIFF_WIDTH + (pair + 1) * LANES] = o_pair.astype(BF16)

    mix = jnp.dot(oscr[...], wbf[...], preferred_element_type=F32)
    gate = mod_ref[:, 2 * D_MODEL:3 * D_MODEL]
    y = DEEPNORM_ALPHA * x_ref[...] + gate * mix
    o_ref[...] = _layer_norm(y, lng_ref[...], lnb_ref[...])


def _attn_call(q, kv, cache, x, x_row_off, mod3, mod_row_fn, w_out, diff_lambda, subln, ln_g, ln_b,
               layer, n_batch, n_new):
    nq = n_new // ATT_TQ
    n_sub = ATT_CTX_BATCHES if (nq == 1 and cache is None) else 1
    tq = ATT_TQ * n_sub
    n_batch = n_batch // n_sub
    x_blk_off = x_row_off // tq
    n_cache = PAST_LEN if cache is not None else 0
    lam_init = 0.8 - 0.6 * math.exp(-0.3 * layer)
    in_specs = [
        pl.BlockSpec((tq, 2 * DIFF_WIDTH), lambda b, i: (b * nq + i, 0)),
        pl.BlockSpec((n_new * n_sub, KV_NEW_COLS), lambda b, i: (b, 0)),
    ]
    args = [q, kv]
    if cache is not None:
        for arr in cache:
            width = arr.shape[-1]
            in_specs.append(pl.BlockSpec((None, None, PAST_LEN, width), lambda b, i: (b, layer, 0, 0)))
            args.append(arr)
    in_specs += [
        pl.BlockSpec((tq, D_MODEL), lambda b, i: (b * nq + i + x_blk_off, 0)),
        pl.BlockSpec((None, 1, N_MOD), lambda b, i: (mod_row_fn(b), 0, 0)),
        pl.BlockSpec((None, D_MODEL, D_MODEL), lambda b, i: (layer, 0, 0), pipeline_mode=pl.Buffered(1)),
        pl.BlockSpec((None, 4, HEAD_DIM), lambda b, i: (layer, 0, 0)),
        pl.BlockSpec((1, LANES), lambda b, i: (0, 0)),
        pl.BlockSpec((1, D_MODEL), lambda b, i: (0, 0)),
        pl.BlockSpec((1, D_MODEL), lambda b, i: (0, 0)),
    ]
    args += [x, mod3, w_out, diff_lambda, subln, ln_g, ln_b]
    return pl.pallas_call(
        functools.partial(_attn_kernel, n_new=n_new, n_cache=n_cache, n_sub=n_sub, lam_init=lam_init),
        grid=(n_batch, nq),
        in_specs=in_specs,
        out_specs=pl.BlockSpec((tq, D_MODEL), lambda b, i: (b * nq + i, 0)),
        out_shape=jax.ShapeDtypeStruct((n_batch * n_sub * n_new, D_MODEL), F32),
        scratch_shapes=[pltpu.VMEM((n_sub, n_new + n_cache, KV_PREP_COLS), BF16),
                        pltpu.VMEM((D_MODEL, D_MODEL), BF16),
                        pltpu.VMEM((tq, D_MODEL), BF16)],
        compiler_params=_params(2),
        name="token_mixer_latent" if cache is not None else "token_mixer_ctx",
    )(*args)


def _router_kernel(xp_ref, xs_ref, mod_ref, wr_ref, h_ref, meta_ref, meta_t_ref, cum_ref,
                   total_ref, carry, *, n_prompt_tiles):
    i = pl.program_id(0)

    @pl.when(i == 0)
    def _():
        carry[...] = jnp.zeros_like(carry)

    def run(x_ref):
        tm = x_ref.shape[0]
        shift = mod_ref[:, 3 * D_MODEL:4 * D_MODEL]
        scale = mod_ref[:, 4 * D_MODEL:5 * D_MODEL]
        h = x_ref[...] * (1.0 + scale) + shift
        h_ref[...] = h.astype(BF16)
        logits = _dot_3pass(h, wr_ref[...])
        lane = lax.broadcasted_iota(jnp.int32, logits.shape, 1).astype(F32)
        neg = jnp.float32(-jnp.inf)
        logits = jnp.where(lane < N_EXPERTS, logits, neg)
        m1 = jnp.max(logits, axis=-1, keepdims=True)
        i1 = jnp.min(jnp.where(logits == m1, lane, float(LANES)), axis=-1, keepdims=True)
        rest = jnp.where(lane == i1, neg, logits)
        m2 = jnp.max(rest, axis=-1, keepdims=True)
        i2 = jnp.min(jnp.where(rest == m2, lane, float(LANES)), axis=-1, keepdims=True)
        e2 = jnp.exp(m2 - m1)
        p1 = 1.0 / (1.0 + e2)
        p2 = e2 / (1.0 + e2)
        hit1 = lane == i1
        hit2 = lane == i2

        sel = jnp.where(hit1, 1.0, 0.0) + jnp.where(hit2, 1.0, 0.0)
        r = lax.broadcasted_iota(jnp.int32, (tm, tm), 0)
        c = lax.broadcasted_iota(jnp.int32, (tm, tm), 1)
        before = jnp.where(c < r, 1.0, 0.0).astype(BF16)
        cumx = jnp.dot(before, sel.astype(BF16), preferred_element_type=F32) + carry[0:1, :]
        rank1 = jnp.sum(jnp.where(hit1, cumx, 0.0), axis=-1, keepdims=True)
        rank2 = jnp.sum(jnp.where(hit2, cumx, 0.0), axis=-1, keepdims=True)
        meta = jnp.zeros_like(logits)
        for k, val in enumerate((i1, i2, rank1, rank2, p1, p2)):
            meta = jnp.where(lane == float(k), val, meta)
        meta_ref[...] = meta
        meta_t_ref[...] = jnp.transpose(meta)[0:8, :]
        for k in range(tm // MOE_HALF):
            cum_ref[k] = jnp.broadcast_to(cumx[k * MOE_HALF:k * MOE_HALF + 1, :], (8, LANES))
        new_carry = carry[0:1, :] + jnp.sum(sel, axis=0, keepdims=True)
        carry[...] = jnp.broadcast_to(new_carry, carry.shape)
        total_ref[...] = jnp.broadcast_to(new_carry, total_ref.shape)

    @pl.when(i < n_prompt_tiles)
    def _():
        run(xp_ref)

    @pl.when(i >= n_prompt_tiles)
    def _():
        run(xs_ref)


def _router_call(xp, xs, mod3, mod_row_fn, w_router_pad):
    tm = FFN_TM
    npt = N_PROMPT // tm
    halves = tm // MOE_HALF
    return pl.pallas_call(
        functools.partial(_router_kernel, n_prompt_tiles=npt),
        grid=(N_TOK // tm,),
        in_specs=[
            pl.BlockSpec((tm, D_MODEL), lambda i: (jnp.minimum(i, npt - 1), 0)),
            pl.BlockSpec((tm, D_MODEL), lambda i: (jnp.maximum(i - npt, 0), 0)),
            pl.BlockSpec((None, 1, N_MOD), lambda i: (mod_row_fn(i), 0, 0)),
            pl.BlockSpec((D_MODEL, LANES), lambda i: (0, 0)),
        ],
        out_specs=[
            pl.BlockSpec((tm, D_MODEL), lambda i: (i, 0)),
            pl.BlockSpec((tm, LANES), lambda i: (i, 0)),
            pl.BlockSpec((8, tm), lambda i: (0, i)),
            pl.BlockSpec((halves, 8, LANES), lambda i: (i, 0, 0)),
            pl.BlockSpec((8, LANES), lambda i: (0, 0)),
        ],
        out_shape=[
            jax.ShapeDtypeStruct((N_TOK, D_MODEL), BF16),
            jax.ShapeDtypeStruct((N_TOK, LANES), F32),
            jax.ShapeDtypeStruct((8, N_TOK), F32),
            jax.ShapeDtypeStruct((N_TOK // MOE_HALF, 8, LANES), F32),
            jax.ShapeDtypeStruct((8, LANES), F32),
        ],
        scratch_shapes=[pltpu.VMEM((8, LANES), F32)],
        compiler_params=_params(1),
        name="router",
    )(xp, xs, mod3, w_router_pad)


def _ffn_kernel(xp_ref, xs_ref, mod_ref, wg_ref, wu_ref, wd_ref, lng_ref, lnb_ref, o_ref, hscr, acc, *,
                n_prompt_tiles):
    i = pl.program_id(0)
    j = pl.program_id(1)
    first = j == 0
    last = j == pl.num_programs(1) - 1

    def modulate(x_ref):
        shift = mod_ref[:, 3 * D_MODEL:4 * D_MODEL]
        scale = mod_ref[:, 4 * D_MODEL:5 * D_MODEL]
        hscr[...] = (x_ref[...] * (1.0 + scale) + shift).astype(BF16)
        acc[...] = jnp.zeros_like(acc)

    @pl.when(first & (i < n_prompt_tiles))
    def _():
        modulate(xp_ref)

    @pl.when(first & (i >= n_prompt_tiles))
    def _():
        modulate(xs_ref)

    h = hscr[...]
    g = jnp.dot(h, wg_ref[...].astype(BF16), preferred_element_type=F32)
    u = jnp.dot(h, wu_ref[...].astype(BF16), preferred_element_type=F32)
    a = _silu(g) * u
    acc[...] += jnp.dot(a.astype(BF16), wd_ref[...].astype(BF16), preferred_element_type=F32)

    def finish(x_ref):
        gate = mod_ref[:, 5 * D_MODEL:6 * D_MODEL]
        y = DEEPNORM_ALPHA * x_ref[...] + gate * acc[...]
        o_ref[...] = _layer_norm(y, lng_ref[...], lnb_ref[...])

    @pl.when(last & (i < n_prompt_tiles))
    def _():
        finish(xp_ref)

    @pl.when(last & (i >= n_prompt_tiles))
    def _():
        finish(xs_ref)


def _ffn_call(xp, xs, mod3, mod_row_fn, wg, wu, wd, layer_idx, ln_g, ln_b):
    tm, tf = FFN_TM, FFN_TF
    npt = N_PROMPT // tm
    return pl.pallas_call(
        functools.partial(_ffn_kernel, n_prompt_tiles=npt),
        grid=(N_TOK // tm, D_FF // tf),
        in_specs=[
            pl.BlockSpec((tm, D_MODEL), lambda i, j: (jnp.minimum(i, npt - 1), 0)),
            pl.BlockSpec((tm, D_MODEL), lambda i, j: (jnp.maximum(i - npt, 0), 0)),
            pl.BlockSpec((None, 1, N_MOD), lambda i, j: (mod_row_fn(i), 0, 0)),
            pl.BlockSpec((None, D_MODEL, tf), lambda i, j: (layer_idx, 0, j)),
            pl.BlockSpec((None, D_MODEL, tf), lambda i, j: (layer_idx, 0, j)),
            pl.BlockSpec((None, tf, D_MODEL), lambda i, j: (layer_idx, j, 0)),
            pl.BlockSpec((1, D_MODEL), lambda i, j: (0, 0)),
            pl.BlockSpec((1, D_MODEL), lambda i, j: (0, 0)),
        ],
        out_specs=pl.BlockSpec((tm, D_MODEL), lambda i, j: (i, 0)),
        out_shape=jax.ShapeDtypeStruct((N_TOK, D_MODEL), F32),
        scratch_shapes=[pltpu.VMEM((tm, D_MODEL), BF16), pltpu.VMEM((tm, D_MODEL), F32)],
        compiler_params=_params(2),
        name="channel_mixer_dense",
    )(xp, xs, mod3, wg, wu, wd, ln_g, ln_b)


VISIT_FIELDS = 8
NO_ROW = -1.0e9


def _moe_ffn_kernel(vinfo, gwin, h_ref, mt_ref, wg_ref, wu_ref, wd_ref, y_ref, hs, gate_s, acc):
    v = pl.program_id(0)
    j = pl.program_id(1)
    tile = vinfo[v * VISIT_FIELDS + 0]
    expert = vinfo[v * VISIT_FIELDS + 1]
    valid = vinfo[v * VISIT_FIELDS + 2] == 1
    row_lo = vinfo[v * VISIT_FIELDS + 3]
    row_hi = vinfo[v * VISIT_FIELDS + 4]
    expert_row0 = vinfo[v * VISIT_FIELDS + 5]
    n_g = MOE_GATHER_CHUNKS

    def sub_rows(s):
        return slice(s * MOE_CHUNK, (s + 1) * MOE_CHUNK)

    def active(s):
        return (row_lo < (s + 1) * MOE_CHUNK) & (row_hi > s * MOE_CHUNK)

    def owns_start(s):
        return row_lo <= s * MOE_CHUNK

    @pl.when(valid & (j == 0))
    def _():
        row_iota = lax.broadcasted_iota(jnp.int32, (MOE_CHUNK, MOE_CHUNK), 0).astype(F32)
        expert_f = expert.astype(F32)
        for s in range(MOE_SUBS):
            rows = sub_rows(s)

            @pl.when(active(s) & owns_start(s))
            def _():
                hs[rows, :] = jnp.zeros((MOE_CHUNK, D_MODEL), BF16)
                gate_s[rows, :] = jnp.zeros((MOE_CHUNK, 1), F32)

            @pl.when(active(s))
            def _():
                acc[rows, :] = jnp.zeros((MOE_CHUNK, D_MODEL), F32)
                first_chunk = gwin[(v * MOE_SUBS + s) * 2]
                rank0 = (tile * MOE_TM + s * MOE_CHUNK - expert_row0).astype(F32)

                def body(w, carry):
                    want = first_chunk + w * n_g
                    cs = jnp.minimum(want, MOE_NCHUNK - n_g)
                    pieces = []
                    gate = jnp.zeros((MOE_CHUNK, 1), F32)
                    for k in range(n_g):
                        mt = mt_ref[cs + k]
                        first = mt[0:1] == expert_f
                        second = mt[1:2] == expert_f
                        rank = jnp.where(first, mt[2:3], jnp.where(second, mt[3:4], NO_ROW))
                        prob = jnp.where(first, mt[4:5], jnp.where(second, mt[5:6], 0.0))
                        rank = rank + jnp.where(cs + k >= want, 0.0, NO_ROW)
                        match = row_iota + rank0 == rank
                        pieces.append(jnp.where(match, 1.0, 0.0).astype(BF16))
                        gate = gate + jnp.sum(jnp.where(match, prob, 0.0), axis=-1, keepdims=True)
                    onehot = jnp.concatenate(pieces, axis=1)
                    start = pl.multiple_of(cs * MOE_CHUNK, MOE_CHUNK)
                    part = jnp.dot(onehot, h_ref[pl.ds(start, n_g * MOE_CHUNK), :], preferred_element_type=F32)
                    hs[rows, :] = hs[rows, :] + part.astype(BF16)
                    gate_s[rows, :] = gate_s[rows, :] + gate
                    return carry

                lax.fori_loop(0, gwin[(v * MOE_SUBS + s) * 2 + 1], body, 0)

    @pl.when(valid)
    def _():
        first_sub = row_lo // MOE_CHUNK
        n_active = (row_hi + MOE_CHUNK - 1) // MOE_CHUNK - first_sub
        for n in range(1, MOE_SUBS + 1):
            @pl.when(n_active == n)
            def _():
                rows = pl.ds(pl.multiple_of(first_sub * MOE_CHUNK, MOE_CHUNK), n * MOE_CHUNK)
                h = hs[rows, :]
                g = jnp.dot(h, wg_ref[...].astype(BF16), preferred_element_type=F32)
                u = jnp.dot(h, wu_ref[...].astype(BF16), preferred_element_type=F32)
                a = _silu(g) * u * gate_s[rows, :]
                acc[rows, :] += jnp.dot(a.astype(BF16), wd_ref[...].astype(BF16), preferred_element_type=F32)

    @pl.when(valid & (j == pl.num_programs(1) - 1))
    def _():
        for s in range(MOE_SUBS):
            rows = sub_rows(s)
            row = lax.broadcasted_iota(jnp.int32, (MOE_CHUNK, 1), 0) + s * MOE_CHUNK
            mine = jnp.where(row >= row_lo, 1.0, 0.0) * jnp.where(row < row_hi, 1.0, 0.0)

            @pl.when(active(s) & owns_start(s))
            def _():
                y_ref[rows, :] = (acc[rows, :] * mine).astype(BF16)

            @pl.when(active(s) & jnp.logical_not(owns_start(s)))
            def _():
                y_ref[rows, :] = jnp.where(mine > 0.0, acc[rows, :], y_ref[rows, :].astype(F32)).astype(BF16)


def _moe_ffn_call(vinfo, gwin, h, meta_chunks, wg, wu, wd):
    tm, tf = MOE_TM, FFN_TF
    n_j = D_FF // tf

    def expert_of(v, vinfo):
        return vinfo[v * VISIT_FIELDS + 1]

    def w_col(v, j, vinfo):
        return jnp.where(vinfo[v * VISIT_FIELDS + 2] == 1, j, n_j - 1)

    grid_spec = pltpu.PrefetchScalarGridSpec(
        num_scalar_prefetch=2,
        grid=(MOE_MAX_VISITS, n_j),
        in_specs=[
            pl.BlockSpec((N_TOK, D_MODEL), lambda v, j, vinfo, gwin: (0, 0), pipeline_mode=pl.Buffered(1)),
            pl.BlockSpec((MOE_NCHUNK, 8, MOE_CHUNK), lambda v, j, vinfo, gwin: (0, 0, 0),
                         pipeline_mode=pl.Buffered(1)),
            pl.BlockSpec((None, D_MODEL, tf),
                         lambda v, j, vinfo, gwin: (expert_of(v, vinfo), 0, w_col(v, j, vinfo))),
            pl.BlockSpec((None, D_MODEL, tf),
                         lambda v, j, vinfo, gwin: (expert_of(v, vinfo), 0, w_col(v, j, vinfo))),
            pl.BlockSpec((None, tf, D_MODEL),
                         lambda v, j, vinfo, gwin: (expert_of(v, vinfo), w_col(v, j, vinfo), 0)),
        ],
        out_specs=pl.BlockSpec((tm, D_MODEL), lambda v, j, vinfo, gwin: (vinfo[v * VISIT_FIELDS], 0)),
        scratch_shapes=[pltpu.VMEM((tm, D_MODEL), BF16), pltpu.VMEM((tm, 1), F32),
                        pltpu.VMEM((tm, D_MODEL), F32)],
    )
    return pl.pallas_call(
        _moe_ffn_kernel,
        grid_spec=grid_spec,
        out_shape=jax.ShapeDtypeStruct((MOE_ROWS, D_MODEL), BF16),
        compiler_params=_params(2),
        name="channel_mixer_experts",
    )(vinfo, gwin, h, meta_chunks, wg, wu, wd)


def _combine_kernel(cstart, row0, xp_ref, xs_ref, meta_ref, y_ref, mod_ref, lng_ref, lnb_ref, op_ref, os_ref, *,
                    n_prompt_tiles):
    c = pl.program_id(0)

    def run(x_ref, o_ref):
        gate = mod_ref[:, 5 * D_MODEL:6 * D_MODEL]
        col = lax.broadcasted_iota(jnp.int32, (MOE_HALF, MOE_WINDOW), 1).astype(F32)
        for half in range(MOE_CHUNK // MOE_HALF):
            rows = slice(half * MOE_HALF, (half + 1) * MOE_HALF)
            meta = meta_ref[rows, :]
            e1, e2, r1, r2 = meta[:, 0:1], meta[:, 1:2], meta[:, 2:3], meta[:, 3:4]
            total = None
            for e in range(N_EXPERTS):
                start = pl.multiple_of(cstart[(c * 2 + half) * N_EXPERTS + e], MOE_HALF)
                rank = jnp.where(e1 == float(e), r1, jnp.where(e2 == float(e), r2, NO_ROW))
                onehot = jnp.where(col == rank + (row0[e] - start).astype(F32), 1.0, 0.0).astype(BF16)
                part = jnp.dot(onehot, y_ref[pl.ds(start, MOE_WINDOW), :], preferred_element_type=F32)
                total = part if total is None else total + part
            y = DEEPNORM_ALPHA * x_ref[rows, :] + gate * total
            o_ref[rows, :] = _layer_norm(y, lng_ref[...], lnb_ref[...])

    @pl.when(c < n_prompt_tiles)
    def _():
        run(xp_ref, op_ref)

    @pl.when(c >= n_prompt_tiles)
    def _():
        run(xs_ref, os_ref)


def _combine_call(cstart, row0, xp, xs, meta, y, mod3, mod_row_fn, ln_g, ln_b):
    tm = MOE_CHUNK
    npt = N_PROMPT // tm
    grid_spec = pltpu.PrefetchScalarGridSpec(
        num_scalar_prefetch=2,
        grid=(MOE_NCHUNK,),
        in_specs=[
            pl.BlockSpec((tm, D_MODEL), lambda c, a, b: (jnp.minimum(c, npt - 1), 0)),
            pl.BlockSpec((tm, D_MODEL), lambda c, a, b: (jnp.maximum(c - npt, 0), 0)),
            pl.BlockSpec((tm, LANES), lambda c, a, b: (c, 0)),
            pl.BlockSpec((MOE_ROWS, D_MODEL), lambda c, a, b: (0, 0), pipeline_mode=pl.Buffered(1)),
            pl.BlockSpec((None, 1, N_MOD), lambda c, a, b: (mod_row_fn(c), 0, 0)),
            pl.BlockSpec((1, D_MODEL), lambda c, a, b: (0, 0)),
            pl.BlockSpec((1, D_MODEL), lambda c, a, b: (0, 0)),
        ],
        out_specs=[
            pl.BlockSpec((tm, D_MODEL), lambda c, a, b: (jnp.minimum(c, npt - 1), 0)),
            pl.BlockSpec((tm, D_MODEL), lambda c, a, b: (jnp.maximum(c - npt, 0), 0)),
        ],
    )
    return pl.pallas_call(
        functools.partial(_combine_kernel, n_prompt_tiles=npt),
        grid_spec=grid_spec,
        out_shape=[jax.ShapeDtypeStruct((N_PROMPT, D_MODEL), F32),
                   jax.ShapeDtypeStruct((N_SAMPLE, D_MODEL), F32)],
        compiler_params=_params(1),
        name="expert_combine",
    )(cstart, row0, xp, xs, meta, y, mod3, ln_g, ln_b)


def _moe_plan(cum, total):
    i32 = jnp.int32
    cnt = total[0, :N_EXPERTS].astype(i32)
    off = jnp.cumsum(cnt) - cnt
    cumh = jnp.concatenate([cum[:, 0, :N_EXPERTS], total[0:1, :N_EXPERTS]], axis=0).astype(i32)
    cumc = cumh[::MOE_CHUNK // MOE_HALF]

    n_tiles = MOE_ROWS // MOE_TM
    t0 = jnp.arange(n_tiles, dtype=i32)[:, None] * MOE_TM
    lo = jnp.maximum(t0, off[None, :]).reshape(-1)
    hi = jnp.minimum(t0 + MOE_TM, (off + cnt)[None, :]).reshape(-1)
    ok = hi > lo
    n_visits = jnp.sum(ok.astype(i32))
    order = jnp.argsort(jnp.logical_not(ok), stable=True)[:MOE_MAX_VISITS].astype(i32)
    slot = jnp.arange(MOE_MAX_VISITS, dtype=i32)
    valid = slot < n_visits
    order = order[jnp.minimum(slot, n_visits - 1)]
    vt, ve = order // N_EXPERTS, order % N_EXPERTS
    vlo = jnp.where(valid, lo[order] - vt * MOE_TM, 0)
    vhi = jnp.where(valid, hi[order] - vt * MOE_TM, 0)
    zero = jnp.zeros_like(vt)
    vinfo = jnp.stack([vt, ve, valid.astype(i32), vlo, vhi, off[ve], zero, zero], axis=1).reshape(-1).astype(i32)

    s0 = jnp.arange(MOE_SUBS, dtype=i32)[None, :] * MOE_CHUNK
    rlo = jnp.maximum(vlo[:, None], s0)
    rhi = jnp.minimum(vhi[:, None], s0 + MOE_CHUNK)
    to_rank = (vt * MOE_TM - off[ve])[:, None]
    cum_v = cumc[:, ve]
    c_lo = jnp.sum((cum_v[:, :, None] <= (rlo + to_rank)[None]).astype(i32), axis=0) - 1
    c_hi = jnp.sum((cum_v[:, :, None] < (rhi + to_rank)[None]).astype(i32), axis=0) - 1
    c_lo = jnp.clip(c_lo, 0, MOE_NCHUNK - 1)
    c_hi = jnp.clip(c_hi, 0, MOE_NCHUNK - 1)
    n_win = jnp.where(rhi <= rlo, 0, (c_hi - c_lo) // MOE_GATHER_CHUNKS + 1)
    gwin = jnp.stack([c_lo, n_win], axis=-1).reshape(-1).astype(i32)

    seg_lo = off[None, :] + cumh[:-1]
    cstart = jnp.clip((seg_lo // MOE_HALF) * MOE_HALF, 0, MOE_ROWS - MOE_WINDOW)
    return vinfo, gwin, cstart.reshape(-1).astype(i32), off.astype(i32)


def _rope_tables():
    rows = DEC_SEQ // GRID_W
    row = jnp.repeat(jnp.arange(rows, dtype=F32), GRID_W)
    col = jnp.tile(jnp.arange(GRID_W, dtype=F32), rows)
    n_freq = HEAD_DIM // 4
    inv = ROPE_THETA ** (-jnp.arange(n_freq, dtype=F32) / n_freq)
    ar = row[:, None] * inv
    ac = col[:, None] * inv
    ang = jnp.concatenate([ar, ar, ac, ac], axis=-1)
    cos = jnp.tile(jnp.cos(ang), (1, LANES // HEAD_DIM))
    sin = jnp.tile(jnp.sin(ang), (1, LANES // HEAD_DIM))
    first_half = (jnp.arange(LANES) % (2 * n_freq)) < n_freq
    sin_next = jnp.where(first_half, -sin, 0.0)
    sin_prev = jnp.where(first_half, 0.0, sin)
    return cos, sin_next, sin_prev


def kernel(x_prompt, x_sample, cache_a_k, cache_a_v, cache_b_k, cache_b_v, c, c_ctx, w_ada, b_ada, w_in,
           w_out, diff_lambda, diff_subln, qk_norm_gain, ln_gain, ln_bias, w_ffn_gate, w_ffn_up,
           w_ffn_down, w_router, w_moe_gate, w_moe_up, w_moe_down):
    cond = jnp.zeros((MOD_ROWS, D_MODEL), F32).at[0].set(c_ctx).at[1:1 + DEC_BATCH].set(c)
    mod3 = _ada_call(cond, w_ada, b_ada).reshape(DEPTH * MOD_ROWS, 1, N_MOD)
    rope_tabs = _rope_tables()
    cache = (cache_a_k.reshape(DEC_BATCH, DEPTH, PAST_LEN, DIFF_WIDTH),
             cache_a_v.reshape(DEC_BATCH, DEPTH, PAST_LEN, DIFF_WIDTH),
             cache_b_k.reshape(DEC_BATCH, DEPTH, PAST_LEN, KV_WIDTH),
             cache_b_v.reshape(DEC_BATCH, DEPTH, PAST_LEN, KV_WIDTH))

    xp = x_prompt.reshape(N_PROMPT, D_MODEL)
    xs = x_sample.reshape(N_SAMPLE, D_MODEL)
    xs_off = 0
    new_caches = []
    for l in range(DEPTH):
        base = l * MOD_ROWS
        gq = jnp.tile(qk_norm_gain[l, 0], LANES // HEAD_DIM).reshape(1, LANES)
        gk = jnp.tile(qk_norm_gain[l, 1], LANES // HEAD_DIM).reshape(1, LANES)
        subln = diff_subln[l].reshape(1, LANES)
        ln1_g, ln1_b = ln_gain[l, 0].reshape(1, D_MODEL), ln_bias[l, 0].reshape(1, D_MODEL)
        ln2_g, ln2_b = ln_gain[l, 1].reshape(1, D_MODEL), ln_bias[l, 1].reshape(1, D_MODEL)

        q_p, kv_p, ak, av, bk, bv = _inproj_call(
            xp, 0, N_PROMPT, mod3, lambda i: base, w_in, gq, gk, l, None, True)
        new_caches.append((ak, av, bk, bv))
        x1_p = _attn_call(q_p, kv_p, None, xp, 0, mod3, lambda b: base, w_out, diff_lambda, subln,
                          ln1_g, ln1_b, l, BATCH, SEQ)
        s_tiles = DEC_SEQ // PROJ_TM
        q_s, kv_s = _inproj_call(
            xs, xs_off, N_SAMPLE, mod3, lambda i: base + 1 + i // s_tiles, w_in, gq, gk, l, rope_tabs, False)
        x1_s = _attn_call(q_s, kv_s, cache, xs, xs_off, mod3, lambda b: base + 1 + b, w_out, diff_lambda,
                          subln, ln1_g, ln1_b, l, DEC_BATCH, DEC_SEQ)

        npt = N_PROMPT // FFN_TM
        s_ffn_tiles = DEC_SEQ // FFN_TM
        mod_row = lambda i: base + jnp.where(i < npt, 0, 1 + (i - npt) // s_ffn_tiles)
        k = l // 2
        if l % 2 == 0:
            x2 = _ffn_call(x1_p, x1_s, mod3, mod_row, w_ffn_gate, w_ffn_up, w_ffn_down, k, ln2_g, ln2_b)
        else:
            wr = jnp.zeros((D_MODEL, LANES), F32).at[:, :N_EXPERTS].set(w_router[k])
            h2, meta, meta_t, cum, total = _router_call(x1_p, x1_s, mod3, mod_row, wr)
            vinfo, gwin, cstart, row0 = _moe_plan(cum, total)
            meta_chunks = meta_t.reshape(8, MOE_NCHUNK, MOE_CHUNK).transpose(1, 0, 2)
            y = _moe_ffn_call(vinfo, gwin, h2, meta_chunks, w_moe_gate[k], w_moe_up[k], w_moe_down[k])
            npc = N_PROMPT // MOE_CHUNK
            s_chunks = DEC_SEQ // MOE_CHUNK
            mod_row_c = lambda c: base + jnp.where(c < npc, 0, 1 + (c - npc) // s_chunks)
            xp, xs = _combine_call(cstart, row0, x1_p, x1_s, meta, y, mod3, mod_row_c, ln2_g, ln2_b)
            xs_off = 0
            continue
        xp, xs, xs_off = x2, x2, N_PROMPT

    y_prompt = xp[:N_PROMPT].reshape(BATCH, SEQ, D_MODEL)
    y_sample = xs[xs_off:xs_off + N_SAMPLE].reshape(DEC_BATCH, DEC_SEQ, D_MODEL)
    stack = lambda idx, shape: jnp.stack([nc[idx] for nc in new_caches], axis=0).reshape(
        (DEPTH, BATCH, SEQ) + shape).transpose((1, 0, 2) + tuple(range(3, 3 + len(shape))))
    return (y_prompt, y_sample,
            stack(0, (DIFF_HEADS, 2, HEAD_DIM)), stack(1, (DIFF_HEADS, 2 * HEAD_DIM)),
            stack(2, (GQA_KV_HEADS, HEAD_DIM)), stack(3, (GQA_KV_HEADS, HEAD_DIM)))
```

```python
import functools
import math

import jax
import jax.numpy as jnp
from jax import lax
from jax.experimental import pallas as pl
from jax.experimental.pallas import tpu as pltpu

D_MODEL = 1024
BATCH = 16
SEQ = 256
DEPTH = 2
DEC_BATCH = 2
DEC_SEQ = 2048
PAST_LEN = 256
GRID_W = 64
HEAD_DIM = 64
DIFF_HEADS = 4
GQA_HEADS = 8
GQA_KV_HEADS = 2
DIFF_WIDTH = DIFF_HEADS * 2 * HEAD_DIM
GQA_WIDTH = GQA_HEADS * HEAD_DIM
KV_WIDTH = GQA_KV_HEADS * HEAD_DIM
IN_COLS = 3 * DIFF_WIDTH + GQA_WIDTH + 2 * KV_WIDTH
D_FF = 2816
N_EXPERTS = 8
ROPE_THETA = 10000.0
EPS = 1e-6
DEEPNORM_ALPHA = (2 * DEPTH) ** 0.25

N_PROMPT = BATCH * SEQ
N_SAMPLE = DEC_BATCH * DEC_SEQ
N_TOK = N_PROMPT + N_SAMPLE
N_MOD = 6 * D_MODEL
MOD_ROWS = 8

LANES = 128
VMEM_LIMIT = 56 * 1024 * 1024

ADA_TN = 1536
PROJ_TM = 512
ATT_TQ = 256
ATT_CTX_BATCHES = 2
FFN_TM = 1024
FFN_TF = 256
MOE_TM = 2048
MOE_CHUNK = 256
MOE_HALF = MOE_CHUNK // 2
MOE_WINDOW = 2 * MOE_HALF
MOE_GATHER_CHUNKS = 6
MOE_ROWS = 2 * N_TOK
MOE_NCHUNK = N_TOK // MOE_CHUNK
MOE_SUBS = MOE_TM // MOE_CHUNK
MOE_MAX_VISITS = MOE_ROWS // MOE_TM + N_EXPERTS - 1

KV_DIFF_STRIDE = 3 * LANES
KV_GQA_OFF = DIFF_HEADS * KV_DIFF_STRIDE
KV_PREP_COLS = KV_GQA_OFF + 6 * LANES
KV_NEW_COLS = 2 * DIFF_WIDTH + 4 * KV_WIDTH

F32 = jnp.float32
BF16 = jnp.bfloat16


def _params(n_axes):
    return pltpu.CompilerParams(dimension_semantics=("arbitrary",) * n_axes,
                                vmem_limit_bytes=VMEM_LIMIT)


def _layer_norm(y, g, b):
    mu = jnp.mean(y, axis=-1, keepdims=True)
    yc = y - mu
    var = jnp.mean(yc * yc, axis=-1, keepdims=True)
    return yc * lax.rsqrt(var + EPS) * g + b


def _silu(x):
    return x * jax.nn.sigmoid(x)


def _half_masks():
    lane = lax.broadcasted_iota(jnp.int32, (1, LANES), 1)
    lo = (lane < HEAD_DIM).astype(F32)
    return lo, 1.0 - lo


def _split_bf16(x):
    hi = x.astype(BF16)
    return hi, (x - hi.astype(F32)).astype(BF16)


def _dot_3pass(x, w):
    x_hi, x_lo = _split_bf16(x)
    w_hi, w_lo = _split_bf16(w)
    return (jnp.dot(x_hi, w_hi, preferred_element_type=F32) + jnp.dot(x_lo, w_hi, preferred_element_type=F32)
            + jnp.dot(x_hi, w_lo, preferred_element_type=F32))


def _ada_kernel(cond_ref, w_ref, b_ref, o_ref):
    o_ref[...] = _dot_3pass(_silu(cond_ref[...]), w_ref[...]) + b_ref[...]


def _ada_call(cond, w_ada, b_ada):
    return pl.pallas_call(
        _ada_kernel,
        grid=(DEPTH, N_MOD // ADA_TN),
        in_specs=[
            pl.BlockSpec((MOD_ROWS, D_MODEL), lambda l, n: (0, 0)),
            pl.BlockSpec((None, D_MODEL, ADA_TN), lambda l, n: (l, 0, n)),
            pl.BlockSpec((None, 1, ADA_TN), lambda l, n: (l, 0, n)),
        ],
        out_specs=pl.BlockSpec((None, MOD_ROWS, ADA_TN), lambda l, n: (l, 0, n)),
        out_shape=jax.ShapeDtypeStruct((DEPTH, MOD_ROWS, N_MOD), F32),
        compiler_params=_params(2),
        name="ada_modulation",
    )(cond, w_ada, b_ada.reshape(DEPTH, 1, N_MOD))


def _head_sumsq(x):
    r = lax.broadcasted_iota(jnp.int32, (LANES, LANES), 0) // HEAD_DIM
    c = lax.broadcasted_iota(jnp.int32, (LANES, LANES), 1) // HEAD_DIM
    ones = (r == c).astype(BF16)
    sq = x * x
    hi = sq.astype(BF16)
    lo = (sq - hi.astype(F32)).astype(BF16)
    return (jnp.dot(hi, ones, preferred_element_type=F32)
            + jnp.dot(lo, ones, preferred_element_type=F32))


def _inproj_kernel(*refs, rope, caches):
    x_ref, mod_ref, w_ref, gq_ref, gk_ref = refs[:5]
    pos = 5
    if rope:
        cos_ref, sa_ref, sb_ref = refs[pos:pos + 3]
        pos += 3
    q_out, kv_out = refs[pos:pos + 2]
    pos += 2
    if caches:
        ak_out, av_out, bk_out, bv_out = refs[pos:pos + 4]
        pos += 4
    wbf = refs[pos]

    @pl.when(pl.program_id(0) == 0)
    def _():
        wbf[...] = w_ref[...].astype(BF16)

    shift = mod_ref[:, 0:D_MODEL]
    scale = mod_ref[:, D_MODEL:2 * D_MODEL]
    h = (x_ref[...] * (1.0 + scale) + shift).astype(BF16)
    proj = jnp.dot(h, wbf[...], preferred_element_type=F32)

    def group(base, g):
        return proj[:, base + g * LANES: base + (g + 1) * LANES]

    def rot(v):
        if not rope:
            return v
        return (v * cos_ref[...] + pltpu.roll(v, LANES - HEAD_DIM // 4, axis=1) * sa_ref[...]
                + pltpu.roll(v, HEAD_DIM // 4, axis=1) * sb_ref[...])

    def normed(v, gain):
        return v * lax.rsqrt(_head_sumsq(v) * (1.0 / HEAD_DIM) + EPS) * gain

    qk_scale = HEAD_DIM ** -0.5
    off_ak, off_av, off_bq = DIFF_WIDTH, 2 * DIFF_WIDTH, 3 * DIFF_WIDTH
    off_bk, off_bv = off_bq + GQA_WIDTH, off_bq + GQA_WIDTH + KV_WIDTH

    for g in range(DIFF_WIDTH // LANES):
        q_out[:, g * LANES:(g + 1) * LANES] = (rot(group(0, g)) * qk_scale).astype(BF16)
        a_k = group(off_ak, g)
        a_v = group(off_av, g)
        if caches:
            ak_out[:, g * LANES:(g + 1) * LANES] = a_k
            av_out[pl.ds(g, a_v.shape[0], stride=DIFF_HEADS), :] = a_v
        kv_out[:, g * LANES:(g + 1) * LANES] = rot(a_k).astype(BF16)
        kv_out[:, DIFF_WIDTH + g * LANES: DIFF_WIDTH + (g + 1) * LANES] = a_v.astype(BF16)
    for g in range(GQA_WIDTH // LANES):
        b_q = rot(normed(group(off_bq, g), gq_ref[...]))
        q_out[:, DIFF_WIDTH + g * LANES: DIFF_WIDTH + (g + 1) * LANES] = (b_q * qk_scale).astype(BF16)
    b_k = normed(group(off_bk, 0), gk_ref[...])
    b_v = group(off_bv, 0)
    if caches:
        bk_out[...] = b_k
        bv_out[...] = b_v
    b_k = rot(b_k)
    base = 2 * DIFF_WIDTH
    kv_out[:, base:base + LANES] = b_k.astype(BF16)
    kv_out[:, base + LANES:base + 2 * LANES] = b_v.astype(BF16)
    kv_out[:, base + 2 * LANES:base + 3 * LANES] = pltpu.roll(b_k, HEAD_DIM, axis=1).astype(BF16)
    kv_out[:, base + 3 * LANES:base + 4 * LANES] = pltpu.roll(b_v, HEAD_DIM, axis=1).astype(BF16)


def _inproj_call(x, row_off, n_rows, mod3, mod_row_fn, w_in, gq, gk, layer, rope_tabs, caches):
    tm = PROJ_TM
    n_tiles = n_rows // tm
    blk_off = row_off // tm
    rope = rope_tabs is not None
    in_specs = [
        pl.BlockSpec((tm, D_MODEL), lambda i: (i + blk_off, 0)),
        pl.BlockSpec((None, 1, N_MOD), lambda i: (mod_row_fn(i), 0, 0)),
        pl.BlockSpec((None, D_MODEL, IN_COLS), lambda i: (layer, 0, 0)),
        pl.BlockSpec((1, LANES), lambda i: (0, 0)),
        pl.BlockSpec((1, LANES), lambda i: (0, 0)),
    ]
    args = [x, mod3, w_in, gq, gk]
    if rope:
        pos_tiles = DEC_SEQ // tm
        for t in rope_tabs:
            in_specs.append(pl.BlockSpec((tm, LANES), lambda i: (i % pos_tiles, 0)))
            args.append(t)
    out_shape = [jax.ShapeDtypeStruct((n_rows, 2 * DIFF_WIDTH), BF16),
                 jax.ShapeDtypeStruct((n_rows, KV_NEW_COLS), BF16)]
    out_specs = [pl.BlockSpec((tm, 2 * DIFF_WIDTH), lambda i: (i, 0)),
                 pl.BlockSpec((tm, KV_NEW_COLS), lambda i: (i, 0))]
    if caches:
        for rows_per_token, width in ((1, DIFF_WIDTH), (DIFF_HEADS, LANES), (1, KV_WIDTH), (1, KV_WIDTH)):
            out_shape.append(jax.ShapeDtypeStruct((n_rows * rows_per_token, width), F32))
            out_specs.append(pl.BlockSpec((tm * rows_per_token, width), lambda i: (i, 0)))
    return pl.pallas_call(
        functools.partial(_inproj_kernel, rope=rope, caches=caches),
        grid=(n_tiles,),
        in_specs=in_specs,
        out_specs=out_specs,
        out_shape=out_shape,
        scratch_shapes=[pltpu.VMEM((D_MODEL, IN_COLS), BF16)],
        compiler_params=_params(1),
        name="in_projection_rope" if rope else "in_projection_ctx",
    )(*args)


def _attn_kernel(*refs, n_new, n_cache, n_sub, lam_init):
    q_ref, kv_ref = refs[:2]
    pos = 2
    if n_cache:
        cak_ref, cav_ref, cbk_ref, cbv_ref = refs[pos:pos + 4]
        pos += 4
    x_ref, mod_ref, wout_ref, lam_ref, subln_ref, lng_ref, lnb_ref, o_ref = refs[pos:pos + 8]
    kvs, wbf, oscr = refs[pos + 8:pos + 11]

    b = pl.program_id(0)
    qi = pl.program_id(1)
    lo_f, hi_f = _half_masks()
    lo_b, hi_b = lo_f.astype(BF16), hi_f.astype(BF16)

    @pl.when((b == 0) & (qi == 0))
    def _():
        wbf[...] = wout_ref[...].astype(BF16)

    @pl.when(qi == 0)
    def _():
        for sb in range(n_sub):
            new = slice(sb * n_new, (sb + 1) * n_new)
            for h in range(DIFF_HEADS):
                k = kv_ref[new, h * LANES:(h + 1) * LANES]
                c0 = h * KV_DIFF_STRIDE
                kvs[sb, 0:n_new, c0:c0 + LANES] = k * lo_b
                kvs[sb, 0:n_new, c0 + LANES:c0 + 2 * LANES] = k * hi_b
                kvs[sb, 0:n_new, c0 + 2 * LANES:c0 + 3 * LANES] = kv_ref[new, DIFF_WIDTH + h * LANES:
                                                                         DIFF_WIDTH + (h + 1) * LANES]
                if n_cache:
                    kc = cak_ref[:, h * LANES:(h + 1) * LANES]
                    kvs[sb, n_new:n_new + n_cache, c0:c0 + LANES] = (kc * lo_f).astype(BF16)
                    kvs[sb, n_new:n_new + n_cache, c0 + LANES:c0 + 2 * LANES] = (kc * hi_f).astype(BF16)
                    kvs[sb, n_new:n_new + n_cache, c0 + 2 * LANES:c0 + 3 * LANES] = (
                        cav_ref[:, h * LANES:(h + 1) * LANES].astype(BF16))
            def put_gqa(rows, k, k_sw, v, v_sw, lo, hi):
                blocks = (k, k_sw, v * lo + hi, v * hi + lo, v_sw * lo + hi, v_sw * hi + lo)
                for n, blk in enumerate(blocks):
                    kvs[sb, rows, KV_GQA_OFF + n * LANES:KV_GQA_OFF + (n + 1) * LANES] = blk.astype(BF16)

            g0 = 2 * DIFF_WIDTH
            put_gqa(slice(0, n_new), kv_ref[new, g0:g0 + LANES], kv_ref[new, g0 + 2 * LANES:g0 + 3 * LANES],
                    kv_ref[new, g0 + LANES:g0 + 2 * LANES], kv_ref[new, g0 + 3 * LANES:g0 + 4 * LANES], lo_b, hi_b)
            if n_cache:
                ck = cbk_ref[...]
                cv = cbv_ref[...]
                put_gqa(slice(n_new, n_new + n_cache), ck, pltpu.roll(ck, HEAD_DIM, axis=1),
                        cv, pltpu.roll(cv, HEAD_DIM, axis=1), lo_f, hi_f)

    lp = lam_ref[...]
    lam = (jnp.exp(jnp.sum(lp[0:1] * lp[1:2], axis=-1, keepdims=True))
           - jnp.exp(jnp.sum(lp[2:3] * lp[3:4], axis=-1, keepdims=True)) + lam_init)

    def scores(q, k):
        return lax.dot_general(q, k, (((1,), (1,)), ((), ())), preferred_element_type=F32)

    def softmax_parts(s):
        e = jnp.exp(s - jnp.max(s, axis=-1, keepdims=True))
        return e, 1.0 / jnp.sum(e, axis=-1, keepdims=True)

    tq = q_ref.shape[0] // n_sub
    for sb in range(n_sub):
        qrows = slice(sb * tq, (sb + 1) * tq)
        for h in range(DIFF_HEADS):
            c0 = h * KV_DIFF_STRIDE
            q = q_ref[qrows, h * LANES:(h + 1) * LANES]
            e1, r1 = softmax_parts(scores(q, kvs[sb, :, c0:c0 + LANES]))
            e2, r2 = softmax_parts(scores(q, kvs[sb, :, c0 + LANES:c0 + 2 * LANES]))
            a = (e1 - e2 * (lam * r2 / r1)).astype(BF16)
            o = jnp.dot(a, kvs[sb, :, c0 + 2 * LANES:c0 + 3 * LANES], preferred_element_type=F32) * r1
            o = o * lax.rsqrt(jnp.mean(o * o, axis=-1, keepdims=True) + EPS) * subln_ref[...]
            oscr[qrows, h * LANES:(h + 1) * LANES] = (o * (1.0 - lam_init)).astype(BF16)

        for pair in range(GQA_HEADS // 2):
            q_pair = q_ref[qrows, DIFF_WIDTH + pair * LANES: DIFF_WIDTH + (pair + 1) * LANES]
            halves = []
            for c in range(2):
                kv_head = (2 * pair + c) // (GQA_HEADS // GQA_KV_HEADS)
                swapped = 0 if kv_head == c else 1
                koff = KV_GQA_OFF + swapped * LANES
                voff = KV_GQA_OFF + (2 + 2 * swapped + c) * LANES
                s = scores(q_pair * (lo_b if c == 0 else hi_b), kvs[sb, :, koff:koff + LANES])
                e = jnp.exp(s - jnp.max(s, axis=-1, keepdims=True)).astype(BF16)
                o = jnp.dot(e, kvs[sb, :, voff:voff + LANES], preferred_element_type=F32)
                halves.append(o / pltpu.roll(o, HEAD_DIM, axis=1))
            o_pair = jnp.where(lo_f > 0.5, halves[0], halves[1])
            oscr[qrows, DIFF_WIDTH + pair * LANES: DIFF_WIDTH + (pair + 1) * LANES] = o_pair.astype(BF16)

    mix = jnp.dot(oscr[...], wbf[...], preferred_element_type=F32)
    gate = mod_ref[:, 2 * D_MODEL:3 * D_MODEL]
    y = DEEPNORM_ALPHA * x_ref[...] + gate * mix
    o_ref[...] = _layer_norm(y, lng_ref[...], lnb_ref[...])


def _attn_call(q, kv, cache, x, x_row_off, mod3, mod_row_fn, w_out, diff_lambda, subln, ln_g, ln_b,
               layer, n_batch, n_new):
    nq = n_new // ATT_TQ
    n_sub = ATT_CTX_BATCHES if (nq == 1 and cache is None) else 1
    tq = ATT_TQ * n_sub
    n_batch = n_batch // n_sub
    x_blk_off = x_row_off // tq
    n_cache = PAST_LEN if cache is not None else 0
    lam_init = 0.8 - 0.6 * math.exp(-0.3 * layer)
    in_specs = [
        pl.BlockSpec((tq, 2 * DIFF_WIDTH), lambda b, i: (b * nq + i, 0)),
        pl.BlockSpec((n_new * n_sub, KV_NEW_COLS), lambda b, i: (b, 0)),
    ]
    args = [q, kv]
    if cache is not None:
        for arr in cache:
            width = arr.shape[-1]
            in_specs.append(pl.BlockSpec((None, None, PAST_LEN, width), lambda b, i: (b, layer, 0, 0)))
            args.append(arr)
    in_specs += [
        pl.BlockSpec((tq, D_MODEL), lambda b, i: (b * nq + i + x_blk_off, 0)),
        pl.BlockSpec((None, 1, N_MOD), lambda b, i: (mod_row_fn(b), 0, 0)),
        pl.BlockSpec((None, D_MODEL, D_MODEL), lambda b, i: (layer, 0, 0), pipeline_mode=pl.Buffered(1)),
        pl.BlockSpec((None, 4, HEAD_DIM), lambda b, i: (layer, 0, 0)),
        pl.BlockSpec((1, LANES), lambda b, i: (0, 0)),
        pl.BlockSpec((1, D_MODEL), lambda b, i: (0, 0)),
        pl.BlockSpec((1, D_MODEL), lambda b, i: (0, 0)),
    ]
    args += [x, mod3, w_out, diff_lambda, subln, ln_g, ln_b]
    return pl.pallas_call(
        functools.partial(_attn_kernel, n_new=n_new, n_cache=n_cache, n_sub=n_sub, lam_init=lam_init),
        grid=(n_batch, nq),
        in_specs=in_specs,
        out_specs=pl.BlockSpec((tq, D_MODEL), lambda b, i: (b * nq + i, 0)),
        out_shape=jax.ShapeDtypeStruct((n_batch * n_sub * n_new, D_MODEL), F32),
        scratch_shapes=[pltpu.VMEM((n_sub, n_new + n_cache, KV_PREP_COLS), BF16),
                        pltpu.VMEM((D_MODEL, D_MODEL), BF16),
                        pltpu.VMEM((tq, D_MODEL), BF16)],
        compiler_params=_params(2),
        name="token_mixer_latent" if cache is not None else "token_mixer_ctx",
    )(*args)


def _router_kernel(xp_ref, xs_ref, mod_ref, wr_ref, h_ref, meta_ref, meta_t_ref, cum_ref,
                   total_ref, carry, *, n_prompt_tiles):
    i = pl.program_id(0)

    @pl.when(i == 0)
    def _():
        carry[...] = jnp.zeros_like(carry)

    def run(x_ref):
        tm = x_ref.shape[0]
        shift = mod_ref[:, 3 * D_MODEL:4 * D_MODEL]
        scale = mod_ref[:, 4 * D_MODEL:5 * D_MODEL]
        h = x_ref[...] * (1.0 + scale) + shift
        h_ref[...] = h.astype(BF16)
        logits = _dot_3pass(h, wr_ref[...])
        lane = lax.broadcasted_iota(jnp.int32, logits.shape, 1).astype(F32)
        neg = jnp.float32(-jnp.inf)
        logits = jnp.where(lane < N_EXPERTS, logits, neg)
        m1 = jnp.max(logits, axis=-1, keepdims=True)
        i1 = jnp.min(jnp.where(logits == m1, lane, float(LANES)), axis=-1, keepdims=True)
        rest = jnp.where(lane == i1, neg, logits)
        m2 = jnp.max(rest, axis=-1, keepdims=True)
        i2 = jnp.min(jnp.where(rest == m2, lane, float(LANES)), axis=-1, keepdims=True)
        e2 = jnp.exp(m2 - m1)
        p1 = 1.0 / (1.0 + e2)
        p2 = e2 / (1.0 + e2)
        hit1 = lane == i1
        hit2 = lane == i2

        sel = jnp.where(hit1, 1.0, 0.0) + jnp.where(hit2, 1.0, 0.0)
        r = lax.broadcasted_iota(jnp.int32, (tm, tm), 0)
        c = lax.broadcasted_iota(jnp.int32, (tm, tm), 1)
        before = jnp.where(c < r, 1.0, 0.0).astype(BF16)
        cumx = jnp.dot(before, sel.astype(BF16), preferred_element_type=F32) + carry[0:1, :]
        rank1 = jnp.sum(jnp.where(hit1, cumx, 0.0), axis=-1, keepdims=True)
        rank2 = jnp.sum(jnp.where(hit2, cumx, 0.0), axis=-1, keepdims=True)
        meta = jnp.zeros_like(logits)
        for k, val in enumerate((i1, i2, rank1, rank2, p1, p2)):
            meta = jnp.where(lane == float(k), val, meta)
        meta_ref[...] = meta
        meta_t_ref[...] = jnp.transpose(meta)[0:8, :]
        for k in range(tm // MOE_HALF):
            cum_ref[k] = jnp.broadcast_to(cumx[k * MOE_HALF:k * MOE_HALF + 1, :], (8, LANES))
        new_carry = carry[0:1, :] + jnp.sum(sel, axis=0, keepdims=True)
        carry[...] = jnp.broadcast_to(new_carry, carry.shape)
        total_ref[...] = jnp.broadcast_to(new_carry, total_ref.shape)

    @pl.when(i < n_prompt_tiles)
    def _():
        run(xp_ref)

    @pl.when(i >= n_prompt_tiles)
    def _():
        run(xs_ref)


def _router_call(xp, xs, mod3, mod_row_fn, w_router_pad):
    tm = FFN_TM
    npt = N_PROMPT // tm
    halves = tm // MOE_HALF
    return pl.pallas_call(
        functools.partial(_router_kernel, n_prompt_tiles=npt),
        grid=(N_TOK // tm,),
        in_specs=[
            pl.BlockSpec((tm, D_MODEL), lambda i: (jnp.minimum(i, npt - 1), 0)),
            pl.BlockSpec((tm, D_MODEL), lambda i: (jnp.maximum(i - npt, 0), 0)),
            pl.BlockSpec((None, 1, N_MOD), lambda i: (mod_row_fn(i), 0, 0)),
            pl.BlockSpec((D_MODEL, LANES), lambda i: (0, 0)),
        ],
        out_specs=[
            pl.BlockSpec((tm, D_MODEL), lambda i: (i, 0)),
            pl.BlockSpec((tm, LANES), lambda i: (i, 0)),
            pl.BlockSpec((8, tm), lambda i: (0, i)),
            pl.BlockSpec((halves, 8, LANES), lambda i: (i, 0, 0)),
            pl.BlockSpec((8, LANES), lambda i: (0, 0)),
        ],
        out_shape=[
            jax.ShapeDtypeStruct((N_TOK, D_MODEL), BF16),
            jax.ShapeDtypeStruct((N_TOK, LANES), F32),
            jax.ShapeDtypeStruct((8, N_TOK), F32),
            jax.ShapeDtypeStruct((N_TOK // MOE_HALF, 8, LANES), F32),
            jax.ShapeDtypeStruct((8, LANES), F32),
        ],
        scratch_shapes=[pltpu.VMEM((8, LANES), F32)],
        compiler_params=_params(1),
        name="router",
    )(xp, xs, mod3, w_router_pad)


def _ffn_kernel(xp_ref, xs_ref, mod_ref, wg_ref, wu_ref, wd_ref, lng_ref, lnb_ref, o_ref, hscr, acc, *,
                n_prompt_tiles):
    i = pl.program_id(0)
    j = pl.program_id(1)
    first = j == 0
    last = j == pl.num_programs(1) - 1

    def modulate(x_ref):
        shift = mod_ref[:, 3 * D_MODEL:4 * D_MODEL]
        scale = mod_ref[:, 4 * D_MODEL:5 * D_MODEL]
        hscr[...] = (x_ref[...] * (1.0 + scale) + shift).astype(BF16)
        acc[...] = jnp.zeros_like(acc)

    @pl.when(first & (i < n_prompt_tiles))
    def _():
        modulate(xp_ref)

    @pl.when(first & (i >= n_prompt_tiles))
    def _():
        modulate(xs_ref)

    h = hscr[...]
    g = jnp.dot(h, wg_ref[...].astype(BF16), preferred_element_type=F32)
    u = jnp.dot(h, wu_ref[...].astype(BF16), preferred_element_type=F32)
    a = _silu(g) * u
    acc[...] += jnp.dot(a.astype(BF16), wd_ref[...].astype(BF16), preferred_element_type=F32)

    def finish(x_ref):
        gate = mod_ref[:, 5 * D_MODEL:6 * D_MODEL]
        y = DEEPNORM_ALPHA * x_ref[...] + gate * acc[...]
        o_ref[...] = _layer_norm(y, lng_ref[...], lnb_ref[...])

    @pl.when(last & (i < n_prompt_tiles))
    def _():
        finish(xp_ref)

    @pl.when(last & (i >= n_prompt_tiles))
    def _():
        finish(xs_ref)


def _ffn_call(xp, xs, mod3, mod_row_fn, wg, wu, wd, layer_idx, ln_g, ln_b):
    tm, tf = FFN_TM, FFN_TF
    npt = N_PROMPT // tm
    return pl.pallas_call(
        functools.partial(_ffn_kernel, n_prompt_tiles=npt),
        grid=(N_TOK // tm, D_FF // tf),
        in_specs=[
            pl.BlockSpec((tm, D_MODEL), lambda i, j: (jnp.minimum(i, npt - 1), 0)),
            pl.BlockSpec((tm, D_MODEL), lambda i, j: (jnp.maximum(i - npt, 0), 0)),
            pl.BlockSpec((None, 1, N_MOD), lambda i, j: (mod_row_fn(i), 0, 0)),
            pl.BlockSpec((None, D_MODEL, tf), lambda i, j: (layer_idx, 0, j)),
            pl.BlockSpec((None, D_MODEL, tf), lambda i, j: (layer_idx, 0, j)),
            pl.BlockSpec((None, tf, D_MODEL), lambda i, j: (layer_idx, j, 0)),
            pl.BlockSpec((1, D_MODEL), lambda i, j: (0, 0)),
            pl.BlockSpec((1, D_MODEL), lambda i, j: (0, 0)),
        ],
        out_specs=pl.BlockSpec((tm, D_MODEL), lambda i, j: (i, 0)),
        out_shape=jax.ShapeDtypeStruct((N_TOK, D_MODEL), F32),
        scratch_shapes=[pltpu.VMEM((tm, D_MODEL), BF16), pltpu.VMEM((tm, D_MODEL), F32)],
        compiler_params=_params(2),
        name="channel_mixer_dense",
    )(xp, xs, mod3, wg, wu, wd, ln_g, ln_b)


VISIT_FIELDS = 8
NO_ROW = -1.0e9


def _moe_ffn_kernel(vinfo, gwin, h_ref, mt_ref, wg_ref, wu_ref, wd_ref, y_ref, hs, gate_s, acc):
    v = pl.program_id(0)
    j = pl.program_id(1)
    tile = vinfo[v * VISIT_FIELDS + 0]
    expert = vinfo[v * VISIT_FIELDS + 1]
    valid = vinfo[v * VISIT_FIELDS + 2] == 1
    row_lo = vinfo[v * VISIT_FIELDS + 3]
    row_hi = vinfo[v * VISIT_FIELDS + 4]
    expert_row0 = vinfo[v * VISIT_FIELDS + 5]
    n_g = MOE_GATHER_CHUNKS

    def sub_rows(s):
        return slice(s * MOE_CHUNK, (s + 1) * MOE_CHUNK)

    def active(s):
        return (row_lo < (s + 1) * MOE_CHUNK) & (row_hi > s * MOE_CHUNK)

    def owns_start(s):
        return row_lo <= s * MOE_CHUNK

    @pl.when(valid & (j == 0))
    def _():
        row_iota = lax.broadcasted_iota(jnp.int32, (MOE_CHUNK, MOE_CHUNK), 0).astype(F32)
        expert_f = expert.astype(F32)
        for s in range(MOE_SUBS):
            rows = sub_rows(s)

            @pl.when(active(s) & owns_start(s))
            def _():
                hs[rows, :] = jnp.zeros((MOE_CHUNK, D_MODEL), BF16)
                gate_s[rows, :] = jnp.zeros((MOE_CHUNK, 1), F32)

            @pl.when(active(s))
            def _():
                acc[rows, :] = jnp.zeros((MOE_CHUNK, D_MODEL), F32)
                first_chunk = gwin[(v * MOE_SUBS + s) * 2]
                rank0 = (tile * MOE_TM + s * MOE_CHUNK - expert_row0).astype(F32)

                def body(w, carry):
                    want = first_chunk + w * n_g
                    cs = jnp.minimum(want, MOE_NCHUNK - n_g)
                    pieces = []
                    gate = jnp.zeros((MOE_CHUNK, 1), F32)
                    for k in range(n_g):
                        mt = mt_ref[cs + k]
                        first = mt[0:1] == expert_f
                        second = mt[1:2] == expert_f
                        rank = jnp.where(first, mt[2:3], jnp.where(second, mt[3:4], NO_ROW))
                        prob = jnp.where(first, mt[4:5], jnp.where(second, mt[5:6], 0.0))
                        rank = rank + jnp.where(cs + k >= want, 0.0, NO_ROW)
                        match = row_iota + rank0 == rank
                        pieces.append(jnp.where(match, 1.0, 0.0).astype(BF16))
                        gate = gate + jnp.sum(jnp.where(match, prob, 0.0), axis=-1, keepdims=True)
                    onehot = jnp.concatenate(pieces, axis=1)
                    start = pl.multiple_of(cs * MOE_CHUNK, MOE_CHUNK)
                    part = jnp.dot(onehot, h_ref[pl.ds(start, n_g * MOE_CHUNK), :], preferred_element_type=F32)
                    hs[rows, :] = hs[rows, :] + part.astype(BF16)
                    gate_s[rows, :] = gate_s[rows, :] + gate
                    return carry

                lax.fori_loop(0, gwin[(v * MOE_SUBS + s) * 2 + 1], body, 0)

    @pl.when(valid)
    def _():
        first_sub = row_lo // MOE_CHUNK
        n_active = (row_hi + MOE_CHUNK - 1) // MOE_CHUNK - first_sub
        for n in range(1, MOE_SUBS + 1):
            @pl.when(n_active == n)
            def _():
                rows = pl.ds(pl.multiple_of(first_sub * MOE_CHUNK, MOE_CHUNK), n * MOE_CHUNK)
                h = hs[rows, :]
                g = jnp.dot(h, wg_ref[...].astype(BF16), preferred_element_type=F32)
                u = jnp.dot(h, wu_ref[...].astype(BF16), preferred_element_type=F32)
                a = _silu(g) * u * gate_s[rows, :]
                acc[rows, :] += jnp.dot(a.astype(BF16), wd_ref[...].astype(BF16), preferred_element_type=F32)

    @pl.when(valid & (j == pl.num_programs(1) - 1))
    def _():
        for s in range(MOE_SUBS):
            rows = sub_rows(s)
            row = lax.broadcasted_iota(jnp.int32, (MOE_CHUNK, 1), 0) + s * MOE_CHUNK
            mine = jnp.where(row >= row_lo, 1.0, 0.0) * jnp.where(row < row_hi, 1.0, 0.0)

            @pl.when(active(s) & owns_start(s))
            def _():
                y_ref[rows, :] = (acc[rows, :] * mine).astype(BF16)

            @pl.when(active(s) & jnp.logical_not(owns_start(s)))
            def _():
                y_ref[rows, :] = jnp.where(mine > 0.0, acc[rows, :], y_ref[rows, :].astype(F32)).astype(BF16)


def _moe_ffn_call(vinfo, gwin, h, meta_chunks, wg, wu, wd):
    tm, tf = MOE_TM, FFN_TF
    n_j = D_FF // tf

    def expert_of(v, vinfo):
        return vinfo[v * VISIT_FIELDS + 1]

    def w_col(v, j, vinfo):
        return jnp.where(vinfo[v * VISIT_FIELDS + 2] == 1, j, n_j - 1)

    grid_spec = pltpu.PrefetchScalarGridSpec(
        num_scalar_prefetch=2,
        grid=(MOE_MAX_VISITS, n_j),
        in_specs=[
            pl.BlockSpec((N_TOK, D_MODEL), lambda v, j, vinfo, gwin: (0, 0), pipeline_mode=pl.Buffered(1)),
            pl.BlockSpec((MOE_NCHUNK, 8, MOE_CHUNK), lambda v, j, vinfo, gwin: (0, 0, 0),
                         pipeline_mode=pl.Buffered(1)),
            pl.BlockSpec((None, D_MODEL, tf),
                         lambda v, j, vinfo, gwin: (expert_of(v, vinfo), 0, w_col(v, j, vinfo))),
            pl.BlockSpec((None, D_MODEL, tf),
                         lambda v, j, vinfo, gwin: (expert_of(v, vinfo), 0, w_col(v, j, vinfo))),
            pl.BlockSpec((None, tf, D_MODEL),
                         lambda v, j, vinfo, gwin: (expert_of(v, vinfo), w_col(v, j, vinfo), 0)),
        ],
        out_specs=pl.BlockSpec((tm, D_MODEL), lambda v, j, vinfo, gwin: (vinfo[v * VISIT_FIELDS], 0)),
        scratch_shapes=[pltpu.VMEM((tm, D_MODEL), BF16), pltpu.VMEM((tm, 1), F32),
                        pltpu.VMEM((tm, D_MODEL), F32)],
    )
    return pl.pallas_call(
        _moe_ffn_kernel,
        grid_spec=grid_spec,
        out_shape=jax.ShapeDtypeStruct((MOE_ROWS, D_MODEL), BF16),
        compiler_params=_params(2),
        name="channel_mixer_experts",
    )(vinfo, gwin, h, meta_chunks, wg, wu, wd)


def _combine_kernel(cstart, row0, xp_ref, xs_ref, meta_ref, y_ref, mod_ref, lng_ref, lnb_ref, op_ref, os_ref, *,
                    n_prompt_tiles):
    c = pl.program_id(0)

    def run(x_ref, o_ref):
        gate = mod_ref[:, 5 * D_MODEL:6 * D_MODEL]
        col = lax.broadcasted_iota(jnp.int32, (MOE_HALF, MOE_WINDOW), 1).astype(F32)
        for half in range(MOE_CHUNK // MOE_HALF):
            rows = slice(half * MOE_HALF, (half + 1) * MOE_HALF)
            meta = meta_ref[rows, :]
            e1, e2, r1, r2 = meta[:, 0:1], meta[:, 1:2], meta[:, 2:3], meta[:, 3:4]
            total = None
            for e in range(N_EXPERTS):
                start = pl.multiple_of(cstart[(c * 2 + half) * N_EXPERTS + e], MOE_HALF)
                rank = jnp.where(e1 == float(e), r1, jnp.where(e2 == float(e), r2, NO_ROW))
                onehot = jnp.where(col == rank + (row0[e] - start).astype(F32), 1.0, 0.0).astype(BF16)
                part = jnp.dot(onehot, y_ref[pl.ds(start, MOE_WINDOW), :], preferred_element_type=F32)
                total = part if total is None else total + part
            y = DEEPNORM_ALPHA * x_ref[rows, :] + gate * total
            o_ref[rows, :] = _layer_norm(y, lng_ref[...], lnb_ref[...])

    @pl.when(c < n_prompt_tiles)
    def _():
        run(xp_ref, op_ref)

    @pl.when(c >= n_prompt_tiles)
    def _():
        run(xs_ref, os_ref)


def _combine_call(cstart, row0, xp, xs, meta, y, mod3, mod_row_fn, ln_g, ln_b):
    tm = MOE_CHUNK
    npt = N_PROMPT // tm
    grid_spec = pltpu.PrefetchScalarGridSpec(
        num_scalar_prefetch=2,
        grid=(MOE_NCHUNK,),
        in_specs=[
            pl.BlockSpec((tm, D_MODEL), lambda c, a, b: (jnp.minimum(c, npt - 1), 0)),
            pl.BlockSpec((tm, D_MODEL), lambda c, a, b: (jnp.maximum(c - npt, 0), 0)),
            pl.BlockSpec((tm, LANES), lambda c, a, b: (c, 0)),
            pl.BlockSpec((MOE_ROWS, D_MODEL), lambda c, a, b: (0, 0), pipeline_mode=pl.Buffered(1)),
            pl.BlockSpec((None, 1, N_MOD), lambda c, a, b: (mod_row_fn(c), 0, 0)),
            pl.BlockSpec((1, D_MODEL), lambda c, a, b: (0, 0)),
            pl.BlockSpec((1, D_MODEL), lambda c, a, b: (0, 0)),
        ],
        out_specs=[
            pl.BlockSpec((tm, D_MODEL), lambda c, a, b: (jnp.minimum(c, npt - 1), 0)),
            pl.BlockSpec((tm, D_MODEL), lambda c, a, b: (jnp.maximum(c - npt, 0), 0)),
        ],
    )
    return pl.pallas_call(
        functools.partial(_combine_kernel, n_prompt_tiles=npt),
        grid_spec=grid_spec,
        out_shape=[jax.ShapeDtypeStruct((N_PROMPT, D_MODEL), F32),
                   jax.ShapeDtypeStruct((N_SAMPLE, D_MODEL), F32)],
        compiler_params=_params(1),
        name="expert_combine",
    )(cstart, row0, xp, xs, meta, y, mod3, ln_g, ln_b)


def _moe_plan(cum, total):
    i32 = jnp.int32
    cnt = total[0, :N_EXPERTS].astype(i32)
    off = jnp.cumsum(cnt) - cnt
    cumh = jnp.concatenate([cum[:, 0, :N_EXPERTS], total[0:1, :N_EXPERTS]], axis=0).astype(i32)
    cumc = cumh[::MOE_CHUNK // MOE_HALF]

    n_tiles = MOE_ROWS // MOE_TM
    t0 = jnp.arange(n_tiles, dtype=i32)[:, None] * MOE_TM
    lo = jnp.maximum(t0, off[None, :]).reshape(-1)
    hi = jnp.minimum(t0 + MOE_TM, (off + cnt)[None, :]).reshape(-1)
    ok = hi > lo
    n_visits = jnp.sum(ok.astype(i32))
    order = jnp.argsort(jnp.logical_not(ok), stable=True)[:MOE_MAX_VISITS].astype(i32)
    slot = jnp.arange(MOE_MAX_VISITS, dtype=i32)
    valid = slot < n_visits
    order = order[jnp.minimum(slot, n_visits - 1)]
    vt, ve = order // N_EXPERTS, order % N_EXPERTS
    vlo = jnp.where(valid, lo[order] - vt * MOE_TM, 0)
    vhi = jnp.where(valid, hi[order] - vt * MOE_TM, 0)
    zero = jnp.zeros_like(vt)
    vinfo = jnp.stack([vt, ve, valid.astype(i32), vlo, vhi, off[ve], zero, zero], axis=1).reshape(-1).astype(i32)

    s0 = jnp.arange(MOE_SUBS, dtype=i32)[None, :] * MOE_CHUNK
    rlo = jnp.maximum(vlo[:, None], s0)
    rhi = jnp.minimum(vhi[:, None], s0 + MOE_CHUNK)
    to_rank = (vt * MOE_TM - off[ve])[:, None]
    cum_v = cumc[:, ve]
    c_lo = jnp.sum((cum_v[:, :, None] <= (rlo + to_rank)[None]).astype(i32), axis=0) - 1
    c_hi = jnp.sum((cum_v[:, :, None] < (rhi + to_rank)[None]).astype(i32), axis=0) - 1
    c_lo = jnp.clip(c_lo, 0, MOE_NCHUNK - 1)
    c_hi = jnp.clip(c_hi, 0, MOE_NCHUNK - 1)
    n_win = jnp.where(rhi <= rlo, 0, (c_hi - c_lo) // MOE_GATHER_CHUNKS + 1)
    gwin = jnp.stack([c_lo, n_win], axis=-1).reshape(-1).astype(i32)

    seg_lo = off[None, :] + cumh[:-1]
    cstart = jnp.clip((seg_lo // MOE_HALF) * MOE_HALF, 0, MOE_ROWS - MOE_WINDOW)
    return vinfo, gwin, cstart.reshape(-1).astype(i32), off.astype(i32)


def _rope_tables():
    rows = DEC_SEQ // GRID_W
    row = jnp.repeat(jnp.arange(rows, dtype=F32), GRID_W)
    col = jnp.tile(jnp.arange(GRID_W, dtype=F32), rows)
    n_freq = HEAD_DIM // 4
    inv = ROPE_THETA ** (-jnp.arange(n_freq, dtype=F32) / n_freq)
    ar = row[:, None] * inv
    ac = col[:, None] * inv
    ang = jnp.concatenate([ar, ar, ac, ac], axis=-1)
    cos = jnp.tile(jnp.cos(ang), (1, LANES // HEAD_DIM))
    sin = jnp.tile(jnp.sin(ang), (1, LANES // HEAD_DIM))
    first_half = (jnp.arange(LANES) % (2 * n_freq)) < n_freq
    sin_next = jnp.where(first_half, -sin, 0.0)
    sin_prev = jnp.where(first_half, 0.0, sin)
    return cos, sin_next, sin_prev


def kernel(x_prompt, x_sample, cache_a_k, cache_a_v, cache_b_k, cache_b_v, c, c_ctx, w_ada, b_ada, w_in,
           w_out, diff_lambda, diff_subln, qk_norm_gain, ln_gain, ln_bias, w_ffn_gate, w_ffn_up,
           w_ffn_down, w_router, w_moe_gate, w_moe_up, w_moe_down):
    cond = jnp.zeros((MOD_ROWS, D_MODEL), F32).at[0].set(c_ctx).at[1:1 + DEC_BATCH].set(c)
    mod3 = _ada_call(cond, w_ada, b_ada).reshape(DEPTH * MOD_ROWS, 1, N_MOD)
    rope_tabs = _rope_tables()
    cache = (cache_a_k.reshape(DEC_BATCH, DEPTH, PAST_LEN, DIFF_WIDTH),
             cache_a_v.reshape(DEC_BATCH, DEPTH, PAST_LEN, DIFF_WIDTH),
             cache_b_k.reshape(DEC_BATCH, DEPTH, PAST_LEN, KV_WIDTH),
             cache_b_v.reshape(DEC_BATCH, DEPTH, PAST_LEN, KV_WIDTH))

    xp = x_prompt.reshape(N_PROMPT, D_MODEL)
    xs = x_sample.reshape(N_SAMPLE, D_MODEL)
    xs_off = 0
    new_caches = []
    for l in range(DEPTH):
        base = l * MOD_ROWS
        gq = jnp.tile(qk_norm_gain[l, 0], LANES // HEAD_DIM).reshape(1, LANES)
        gk = jnp.tile(qk_norm_gain[l, 1], LANES // HEAD_DIM).reshape(1, LANES)
        subln = diff_subln[l].reshape(1, LANES)
        ln1_g, ln1_b = ln_gain[l, 0].reshape(1, D_MODEL), ln_bias[l, 0].reshape(1, D_MODEL)
        ln2_g, ln2_b = ln_gain[l, 1].reshape(1, D_MODEL), ln_bias[l, 1].reshape(1, D_MODEL)

        q_p, kv_p, ak, av, bk, bv = _inproj_call(
            xp, 0, N_PROMPT, mod3, lambda i: base, w_in, gq, gk, l, None, True)
        new_caches.append((ak, av, bk, bv))
        x1_p = _attn_call(q_p, kv_p, None, xp, 0, mod3, lambda b: base, w_out, diff_lambda, subln,
                          ln1_g, ln1_b, l, BATCH, SEQ)
        s_tiles = DEC_SEQ // PROJ_TM
        q_s, kv_s = _inproj_call(
            xs, xs_off, N_SAMPLE, mod3, lambda i: base + 1 + i // s_tiles, w_in, gq, gk, l, rope_tabs, False)
        x1_s = _attn_call(q_s, kv_s, cache, xs, xs_off, mod3, lambda b: base + 1 + b, w_out, diff_lambda,
                          subln, ln1_g, ln1_b, l, DEC_BATCH, DEC_SEQ)

        npt = N_PROMPT // FFN_TM
        s_ffn_tiles = DEC_SEQ // FFN_TM
        mod_row = lambda i: base + jnp.where(i < npt, 0, 1 + (i - npt) // s_ffn_tiles)
        k = l // 2
        if l % 2 == 0:
            x2 = _ffn_call(x1_p, x1_s, mod3, mod_row, w_ffn_gate, w_ffn_up, w_ffn_down, k, ln2_g, ln2_b)
        else:
            wr = jnp.zeros((D_MODEL, LANES), F32).at[:, :N_EXPERTS].set(w_router[k])
            h2, meta, meta_t, cum, total = _router_call(x1_p, x1_s, mod3, mod_row, wr)
            vinfo, gwin, cstart, row0 = _moe_plan(cum, total)
            meta_chunks = meta_t.reshape(8, MOE_NCHUNK, MOE_CHUNK).transpose(1, 0, 2)
            y = _moe_ffn_call(vinfo, gwin, h2, meta_chunks, w_moe_gate[k], w_moe_up[k], w_moe_down[k])
            npc = N_PROMPT // MOE_CHUNK
            s_chunks = DEC_SEQ // MOE_CHUNK
            mod_row_c = lambda c: base + jnp.where(c < npc, 0, 1 + (c - npc) // s_chunks)
            xp, xs = _combine_call(cstart, row0, x1_p, x1_s, meta, y, mod3, mod_row_c, ln2_g, ln2_b)
            xs_off = 0
            continue
        xp, xs, xs_off = x2, x2, N_PROMPT

    y_prompt = xp[:N_PROMPT].reshape(BATCH, SEQ, D_MODEL)
    y_sample = xs[xs_off:xs_off + N_SAMPLE].reshape(DEC_BATCH, DEC_SEQ, D_MODEL)
    stack = lambda idx, shape: jnp.stack([nc[idx] for nc in new_caches], axis=0).reshape(
        (DEPTH, BATCH, SEQ) + shape).transpose((1, 0, 2) + tuple(range(3, 3 + len(shape))))
    return (y_prompt, y_sample,
            stack(0, (DIFF_HEADS, 2, HEAD_DIM)), stack(1, (DIFF_HEADS, 2 * HEAD_DIM)),
            stack(2, (GQA_KV_HEADS, HEAD_DIM)), stack(3, (GQA_KV_HEADS, HEAD_DIM)))
```

```python
import functools
import math

import jax
import jax.numpy as jnp
from jax import lax
from jax.experimental import pallas as pl
from jax.experimental.pallas import tpu as pltpu

D_MODEL = 1024
BATCH = 16
SEQ = 256
DEPTH = 2
DEC_BATCH = 2
DEC_SEQ = 2048
PAST_LEN = 256
GRID_W = 64
HEAD_DIM = 64
DIFF_HEADS = 4
GQA_HEADS = 8
GQA_KV_HEADS = 2
DIFF_WIDTH = DIFF_HEADS * 2 * HEAD_DIM
GQA_WIDTH = GQA_HEADS * HEAD_DIM
KV_WIDTH = GQA_KV_HEADS * HEAD_DIM
IN_COLS = 3 * DIFF_WIDTH + GQA_WIDTH + 2 * KV_WIDTH
D_FF = 2816
N_EXPERTS = 8
ROPE_THETA = 10000.0
EPS = 1e-6
DEEPNORM_ALPHA = (2 * DEPTH) ** 0.25

N_PROMPT = BATCH * SEQ
N_SAMPLE = DEC_BATCH * DEC_SEQ
N_TOK = N_PROMPT + N_SAMPLE
N_MOD = 6 * D_MODEL
MOD_ROWS = 8

LANES = 128
VMEM_LIMIT = 56 * 1024 * 1024

ADA_TN = 1536
PROJ_TM = 512
ATT_TQ = 256
ATT_CTX_BATCHES = 2
FFN_TM = 1024
FFN_TF = 256
MOE_TM = 2048
MOE_CHUNK = 256
MOE_HALF = MOE_CHUNK // 2
MOE_WINDOW = 2 * MOE_HALF
MOE_GATHER_CHUNKS = 6
MOE_ROWS = 2 * N_TOK
MOE_NCHUNK = N_TOK // MOE_CHUNK
MOE_SUBS = MOE_TM // MOE_CHUNK
MOE_MAX_VISITS = MOE_ROWS // MOE_TM + N_EXPERTS - 1

KV_DIFF_STRIDE = 3 * LANES
KV_GQA_OFF = DIFF_HEADS * KV_DIFF_STRIDE
KV_PREP_COLS = KV_GQA_OFF + 6 * LANES
KV_NEW_COLS = 2 * DIFF_WIDTH + 4 * KV_WIDTH

F32 = jnp.float32
BF16 = jnp.bfloat16


def _params(n_axes):
    return pltpu.CompilerParams(dimension_semantics=("arbitrary",) * n_axes,
                                vmem_limit_bytes=VMEM_LIMIT)


def _layer_norm(y, g, b):
    mu = jnp.mean(y, axis=-1, keepdims=True)
    yc = y - mu
    var = jnp.mean(yc * yc, axis=-1, keepdims=True)
    return yc * lax.rsqrt(var + EPS) * g + b


def _silu(x):
    return x * jax.nn.sigmoid(x)


def _half_masks():
    lane = lax.broadcasted_iota(jnp.int32, (1, LANES), 1)
    lo = (lane < HEAD_DIM).astype(F32)
    return lo, 1.0 - lo


def _split_bf16(x):
    hi = x.astype(BF16)
    return hi, (x - hi.astype(F32)).astype(BF16)


def _dot_3pass(x, w):
    x_hi, x_lo = _split_bf16(x)
    w_hi, w_lo = _split_bf16(w)
    return (jnp.dot(x_hi, w_hi, preferred_element_type=F32) + jnp.dot(x_lo, w_hi, preferred_element_type=F32)
            + jnp.dot(x_hi, w_lo, preferred_element_type=F32))


def _ada_kernel(cond_ref, w_ref, b_ref, o_ref):
    o_ref[...] = _dot_3pass(_silu(cond_ref[...]), w_ref[...]) + b_ref[...]


def _ada_call(cond, w_ada, b_ada):
    return pl.pallas_call(
        _ada_kernel,
        grid=(DEPTH, N_MOD // ADA_TN),
        in_specs=[
            pl.BlockSpec((MOD_ROWS, D_MODEL), lambda l, n: (0, 0)),
            pl.BlockSpec((None, D_MODEL, ADA_TN), lambda l, n: (l, 0, n)),
            pl.BlockSpec((None, 1, ADA_TN), lambda l, n: (l, 0, n)),
        ],
        out_specs=pl.BlockSpec((None, MOD_ROWS, ADA_TN), lambda l, n: (l, 0, n)),
        out_shape=jax.ShapeDtypeStruct((DEPTH, MOD_ROWS, N_MOD), F32),
        compiler_params=_params(2),
        name="ada_modulation",
    )(cond, w_ada, b_ada.reshape(DEPTH, 1, N_MOD))


def _head_sumsq(x):
    r = lax.broadcasted_iota(jnp.int32, (LANES, LANES), 0) // HEAD_DIM
    c = lax.broadcasted_iota(jnp.int32, (LANES, LANES), 1) // HEAD_DIM
    ones = (r == c).astype(BF16)
    sq = x * x
    hi = sq.astype(BF16)
    lo = (sq - hi.astype(F32)).astype(BF16)
    return (jnp.dot(hi, ones, preferred_element_type=F32)
            + jnp.dot(lo, ones, preferred_element_type=F32))


def _inproj_kernel(*refs, rope, caches):
    x_ref, mod_ref, w_ref, gq_ref, gk_ref = refs[:5]
    pos = 5
    if rope:
        cos_ref, sa_ref, sb_ref = refs[pos:pos + 3]
        pos += 3
    q_out, kv_out = refs[pos:pos + 2]
    pos += 2
    if caches:
        ak_out, av_out, bk_out, bv_out = refs[pos:pos + 4]
        pos += 4
    wbf = refs[pos]

    @pl.when(pl.program_id(0) == 0)
    def _():
        wbf[...] = w_ref[...].astype(BF16)

    shift = mod_ref[:, 0:D_MODEL]
    scale = mod_ref[:, D_MODEL:2 * D_MODEL]
    h = (x_ref[...] * (1.0 + scale) + shift).astype(BF16)
    proj = jnp.dot(h, wbf[...], preferred_element_type=F32)

    def group(base, g):
        return proj[:, base + g * LANES: base + (g + 1) * LANES]

    def rot(v):
        if not rope:
            return v
        return (v * cos_ref[...] + pltpu.roll(v, LANES - HEAD_DIM // 4, axis=1) * sa_ref[...]
                + pltpu.roll(v, HEAD_DIM // 4, axis=1) * sb_ref[...])

    def normed(v, gain):
        return v * lax.rsqrt(_head_sumsq(v) * (1.0 / HEAD_DIM) + EPS) * gain

    qk_scale = HEAD_DIM ** -0.5
    off_ak, off_av, off_bq = DIFF_WIDTH, 2 * DIFF_WIDTH, 3 * DIFF_WIDTH
    off_bk, off_bv = off_bq + GQA_WIDTH, off_bq + GQA_WIDTH + KV_WIDTH

    for g in range(DIFF_WIDTH // LANES):
        q_out[:, g * LANES:(g + 1) * LANES] = (rot(group(0, g)) * qk_scale).astype(BF16)
        a_k = group(off_ak, g)
        a_v = group(off_av, g)
        if caches:
            ak_out[:, g * LANES:(g + 1) * LANES] = a_k
            av_out[pl.ds(g, a_v.shape[0], stride=DIFF_HEADS), :] = a_v
        kv_out[:, g * LANES:(g + 1) * LANES] = rot(a_k).astype(BF16)
        kv_out[:, DIFF_WIDTH + g * LANES: DIFF_WIDTH + (g + 1) * LANES] = a_v.astype(BF16)
    for g in range(GQA_WIDTH // LANES):
        b_q = rot(normed(group(off_bq, g), gq_ref[...]))
        q_out[:, DIFF_WIDTH + g * LANES: DIFF_WIDTH + (g + 1) * LANES] = (b_q * qk_scale).astype(BF16)
    b_k = normed(group(off_bk, 0), gk_ref[...])
    b_v = group(off_bv, 0)
    if caches:
        bk_out[...] = b_k
        bv_out[...] = b_v
    b_k = rot(b_k)
    base = 2 * DIFF_WIDTH
    kv_out[:, base:base + LANES] = b_k.astype(BF16)
    kv_out[:, base + LANES:base + 2 * LANES] = b_v.astype(BF16)
    kv_out[:, base + 2 * LANES:base + 3 * LANES] = pltpu.roll(b_k, HEAD_DIM, axis=1).astype(BF16)
    kv_out[:, base + 3 * LANES:base + 4 * LANES] = pltpu.roll(b_v, HEAD_DIM, axis=1).astype(BF16)


def _inproj_call(x, row_off, n_rows, mod3, mod_row_fn, w_in, gq, gk, layer, rope_tabs, caches):
    tm = PROJ_TM
    n_tiles = n_rows // tm
    blk_off = row_off // tm
    rope = rope_tabs is not None
    in_specs = [
        pl.BlockSpec((tm, D_MODEL), lambda i: (i + blk_off, 0)),
        pl.BlockSpec((None, 1, N_MOD), lambda i: (mod_row_fn(i), 0, 0)),
        pl.BlockSpec((None, D_MODEL, IN_COLS), lambda i: (layer, 0, 0)),
        pl.BlockSpec((1, LANES), lambda i: (0, 0)),
        pl.BlockSpec((1, LANES), lambda i: (0, 0)),
    ]
    args = [x, mod3, w_in, gq, gk]
    if rope:
        pos_tiles = DEC_SEQ // tm
        for t in rope_tabs:
            in_specs.append(pl.BlockSpec((tm, LANES), lambda i: (i % pos_tiles, 0)))
            args.append(t)
    out_shape = [jax.ShapeDtypeStruct((n_rows, 2 * DIFF_WIDTH), BF16),
                 jax.ShapeDtypeStruct((n_rows, KV_NEW_COLS), BF16)]
    out_specs = [pl.BlockSpec((tm, 2 * DIFF_WIDTH), lambda i: (i, 0)),
                 pl.BlockSpec((tm, KV_NEW_COLS), lambda i: (i, 0))]
    if caches:
        for rows_per_token, width in ((1, DIFF_WIDTH), (DIFF_HEADS, LANES), (1, KV_WIDTH), (1, KV_WIDTH)):
            out_shape.append(jax.ShapeDtypeStruct((n_rows * rows_per_token, width), F32))
            out_specs.append(pl.BlockSpec((tm * rows_per_token, width), lambda i: (i, 0)))
    return pl.pallas_call(
        functools.partial(_inproj_kernel, rope=rope, caches=caches),
        grid=(n_tiles,),
        in_specs=in_specs,
        out_specs=out_specs,
        out_shape=out_shape,
        scratch_shapes=[pltpu.VMEM((D_MODEL, IN_COLS), BF16)],
        compiler_params=_params(1),
        name="in_projection_rope" if rope else "in_projection_ctx",
    )(*args)


def _attn_kernel(*refs, n_new, n_cache, n_sub, lam_init):
    q_ref, kv_ref = refs[:2]
    pos = 2
    if n_cache:
        cak_ref, cav_ref, cbk_ref, cbv_ref = refs[pos:pos + 4]
        pos += 4
    x_ref, mod_ref, wout_ref, lam_ref, subln_ref, lng_ref, lnb_ref, o_ref = refs[pos:pos + 8]
    kvs, wbf, oscr = refs[pos + 8:pos + 11]

    b = pl.program_id(0)
    qi = pl.program_id(1)
    lo_f, hi_f = _half_masks()
    lo_b, hi_b = lo_f.astype(BF16), hi_f.astype(BF16)

    @pl.when((b == 0) & (qi == 0))
    def _():
        wbf[...] = wout_ref[...].astype(BF16)

    @pl.when(qi == 0)
    def _():
        for sb in range(n_sub):
            new = slice(sb * n_new, (sb + 1) * n_new)
            for h in range(DIFF_HEADS):
                k = kv_ref[new, h * LANES:(h + 1) * LANES]
                c0 = h * KV_DIFF_STRIDE
                kvs[sb, 0:n_new, c0:c0 + LANES] = k * lo_b
                kvs[sb, 0:n_new, c0 + LANES:c0 + 2 * LANES] = k * hi_b
                kvs[sb, 0:n_new, c0 + 2 * LANES:c0 + 3 * LANES] = kv_ref[new, DIFF_WIDTH + h * LANES:
                                                                         DIFF_WIDTH + (h + 1) * LANES]
                if n_cache:
                    kc = cak_ref[:, h * LANES:(h + 1) * LANES]
                    kvs[sb, n_new:n_new + n_cache, c0:c0 + LANES] = (kc * lo_f).astype(BF16)
                    kvs[sb, n_new:n_new + n_cache, c0 + LANES:c0 + 2 * LANES] = (kc * hi_f).astype(BF16)
                    kvs[sb, n_new:n_new + n_cache, c0 + 2 * LANES:c0 + 3 * LANES] = (
                        cav_ref[:, h * LANES:(h + 1) * LANES].astype(BF16))
            def put_gqa(rows, k, k_sw, v, v_sw, lo, hi):
                blocks = (k, k_sw, v * lo + hi, v * hi + lo, v_sw * lo + hi, v_sw * hi + lo)
                for n, blk in enumerate(blocks):
                    kvs[sb, rows, KV_GQA_OFF + n * LANES:KV_GQA_OFF + (n + 1) * LANES] = blk.astype(BF16)

            g0 = 2 * DIFF_WIDTH
            put_gqa(slice(0, n_new), kv_ref[new, g0:g0 + LANES], kv_ref[new, g0 + 2 * LANES:g0 + 3 * LANES],
                    kv_ref[new, g0 + LANES:g0 + 2 * LANES], kv_ref[new, g0 + 3 * LANES:g0 + 4 * LANES], lo_b, hi_b)
            if n_cache:
                ck = cbk_ref[...]
                cv = cbv_ref[...]
                put_gqa(slice(n_new, n_new + n_cache), ck, pltpu.roll(ck, HEAD_DIM, axis=1),
                        cv, pltpu.roll(cv, HEAD_DIM, axis=1), lo_f, hi_f)

    lp = lam_ref[...]
    lam = (jnp.exp(jnp.sum(lp[0:1] * lp[1:2], axis=-1, keepdims=True))
           - jnp.exp(jnp.sum(lp[2:3] * lp[3:4], axis=-1, keepdims=True)) + lam_init)

    def scores(q, k):
        return lax.dot_general(q, k, (((1,), (1,)), ((), ())), preferred_element_type=F32)

    def softmax_parts(s):
        e = jnp.exp(s - jnp.max(s, axis=-1, keepdims=True))
        return e, 1.0 / jnp.sum(e, axis=-1, keepdims=True)

    tq = q_ref.shape[0] // n_sub
    for sb in range(n_sub):
        qrows = slice(sb * tq, (sb + 1) * tq)
        for h in range(DIFF_HEADS):
            c0 = h * KV_DIFF_STRIDE
            q = q_ref[qrows, h * LANES:(h + 1) * LANES]
            e1, r1 = softmax_parts(scores(q, kvs[sb, :, c0:c0 + LANES]))
            e2, r2 = softmax_parts(scores(q, kvs[sb, :, c0 + LANES:c0 + 2 * LANES]))
            a = (e1 - e2 * (lam * r2 / r1)).astype(BF16)
            o = jnp.dot(a, kvs[sb, :, c0 + 2 * LANES:c0 + 3 * LANES], preferred_element_type=F32) * r1
            o = o * lax.rsqrt(jnp.mean(o * o, axis=-1, keepdims=True) + EPS) * subln_ref[...]
            oscr[qrows, h * LANES:(h + 1) * LANES] = (o * (1.0 - lam_init)).astype(BF16)

        for pair in range(GQA_HEADS // 2):
            q_pair = q_ref[qrows, DIFF_WIDTH + pair * LANES: DIFF_WIDTH + (pair + 1) * LANES]
            halves = []
            for c in range(2):
                kv_head = (2 * pair + c) // (GQA_HEADS // GQA_KV_HEADS)
                swapped = 0 if kv_head == c else 1
                koff = KV_GQA_OFF + swapped * LANES
                voff = KV_GQA_OFF + (2 + 2 * swapped + c) * LANES
                s = scores(q_pair * (lo_b if c == 0 else hi_b), kvs[sb, :, koff:koff + LANES])
                e = jnp.exp(s - jnp.max(s, axis=-1, keepdims=True))
                o = jnp.dot(e.astype(BF16), kvs[sb, :, voff:voff + LANES], preferred_element_type=F32)
                if n_cache:
                    halves.append(o / pltpu.roll(o, HEAD_DIM, axis=1))
                else:
                    halves.append(o / jnp.sum(e, axis=-1, keepdims=True))
            o_pair = jnp.where(lo_f > 0.5, halves[0], halves[1])
            oscr[qrows, DIFF_WIDTH + pair * LANES: DIFF_WIDTH + (pair + 1) * LANES] = o_pair.astype(BF16)

    mix = jnp.dot(oscr[...], wbf[...], preferred_element_type=F32)
    gate = mod_ref[:, 2 * D_MODEL:3 * D_MODEL]
    y = DEEPNORM_ALPHA * x_ref[...] + gate * mix
    o_ref[...] = _layer_norm(y, lng_ref[...], lnb_ref[...])


def _attn_call(q, kv, cache, x, x_row_off, mod3, mod_row_fn, w_out, diff_lambda, subln, ln_g, ln_b,
               layer, n_batch, n_new):
    nq = n_new // ATT_TQ
    n_sub = ATT_CTX_BATCHES if (nq == 1 and cache is None) else 1
    tq = ATT_TQ * n_sub
    n_batch = n_batch // n_sub
    x_blk_off = x_row_off // tq
    n_cache = PAST_LEN if cache is not None else 0
    lam_init = 0.8 - 0.6 * math.exp(-0.3 * layer)
    in_specs = [
        pl.BlockSpec((tq, 2 * DIFF_WIDTH), lambda b, i: (b * nq + i, 0)),
        pl.BlockSpec((n_new * n_sub, KV_NEW_COLS), lambda b, i: (b, 0)),
    ]
    args = [q, kv]
    if cache is not None:
        for arr in cache:
            width = arr.shape[-1]
            in_specs.append(pl.BlockSpec((None, None, PAST_LEN, width), lambda b, i: (b, layer, 0, 0)))
            args.append(arr)
    in_specs += [
        pl.BlockSpec((tq, D_MODEL), lambda b, i: (b * nq + i + x_blk_off, 0)),
        pl.BlockSpec((None, 1, N_MOD), lambda b, i: (mod_row_fn(b), 0, 0)),
        pl.BlockSpec((None, D_MODEL, D_MODEL), lambda b, i: (layer, 0, 0), pipeline_mode=pl.Buffered(1)),
        pl.BlockSpec((None, 4, HEAD_DIM), lambda b, i: (layer, 0, 0)),
        pl.BlockSpec((1, LANES), lambda b, i: (0, 0)),
        pl.BlockSpec((1, D_MODEL), lambda b, i: (0, 0)),
        pl.BlockSpec((1, D_MODEL), lambda b, i: (0, 0)),
    ]
    args += [x, mod3, w_out, diff_lambda, subln, ln_g, ln_b]
    return pl.pallas_call(
        functools.partial(_attn_kernel, n_new=n_new, n_cache=n_cache, n_sub=n_sub, lam_init=lam_init),
        grid=(n_batch, nq),
        in_specs=in_specs,
        out_specs=pl.BlockSpec((tq, D_MODEL), lambda b, i: (b * nq + i, 0)),
        out_shape=jax.ShapeDtypeStruct((n_batch * n_sub * n_new, D_MODEL), F32),
        scratch_shapes=[pltpu.VMEM((n_sub, n_new + n_cache, KV_PREP_COLS), BF16),
                        pltpu.VMEM((D_MODEL, D_MODEL), BF16),
                        pltpu.VMEM((tq, D_MODEL), BF16)],
        compiler_params=_params(2),
        name="token_mixer_latent" if cache is not None else "token_mixer_ctx",
    )(*args)


def _router_kernel(xp_ref, xs_ref, mod_ref, wr_ref, h_ref, meta_ref, meta_t_ref, cum_ref,
                   total_ref, carry, *, n_prompt_tiles):
    i = pl.program_id(0)

    @pl.when(i == 0)
    def _():
        carry[...] = jnp.zeros_like(carry)

    def run(x_ref):
        tm = x_ref.shape[0]
        shift = mod_ref[:, 3 * D_MODEL:4 * D_MODEL]
        scale = mod_ref[:, 4 * D_MODEL:5 * D_MODEL]
        h = x_ref[...] * (1.0 + scale) + shift
        h_ref[...] = h.astype(BF16)
        logits = _dot_3pass(h, wr_ref[...])
        lane = lax.broadcasted_iota(jnp.int32, logits.shape, 1).astype(F32)
        neg = jnp.float32(-jnp.inf)
        logits = jnp.where(lane < N_EXPERTS, logits, neg)
        m1 = jnp.max(logits, axis=-1, keepdims=True)
        i1 = jnp.min(jnp.where(logits == m1, lane, float(LANES)), axis=-1, keepdims=True)
        rest = jnp.where(lane == i1, neg, logits)
        m2 = jnp.max(rest, axis=-1, keepdims=True)
        i2 = jnp.min(jnp.where(rest == m2, lane, float(LANES)), axis=-1, keepdims=True)
        e2 = jnp.exp(m2 - m1)
        p1 = 1.0 / (1.0 + e2)
        p2 = e2 / (1.0 + e2)
        hit1 = lane == i1
        hit2 = lane == i2

        sel = jnp.where(hit1, 1.0, 0.0) + jnp.where(hit2, 1.0, 0.0)
        r = lax.broadcasted_iota(jnp.int32, (tm, tm), 0)
        c = lax.broadcasted_iota(jnp.int32, (tm, tm), 1)
        before = jnp.where(c < r, 1.0, 0.0).astype(BF16)
        cumx = jnp.dot(before, sel.astype(BF16), preferred_element_type=F32) + carry[0:1, :]
        rank1 = jnp.sum(jnp.where(hit1, cumx, 0.0), axis=-1, keepdims=True)
        rank2 = jnp.sum(jnp.where(hit2, cumx, 0.0), axis=-1, keepdims=True)
        meta = jnp.zeros_like(logits)
        for k, val in enumerate((i1, i2, rank1, rank2, p1, p2)):
            meta = jnp.where(lane == float(k), val, meta)
        meta_ref[...] = meta
        meta_t_ref[...] = jnp.transpose(meta)[0:8, :]
        for k in range(tm // MOE_HALF):
            cum_ref[k] = jnp.broadcast_to(cumx[k * MOE_HALF:k * MOE_HALF + 1, :], (8, LANES))
        new_carry = carry[0:1, :] + jnp.sum(sel, axis=0, keepdims=True)
        carry[...] = jnp.broadcast_to(new_carry, carry.shape)
        total_ref[...] = jnp.broadcast_to(new_carry, total_ref.shape)

    @pl.when(i < n_prompt_tiles)
    def _():
        run(xp_ref)

    @pl.when(i >= n_prompt_tiles)
    def _():
        run(xs_ref)


def _router_call(xp, xs, mod3, mod_row_fn, w_router_pad):
    tm = FFN_TM
    npt = N_PROMPT // tm
    halves = tm // MOE_HALF
    return pl.pallas_call(
        functools.partial(_router_kernel, n_prompt_tiles=npt),
        grid=(N_TOK // tm,),
        in_specs=[
            pl.BlockSpec((tm, D_MODEL), lambda i: (jnp.minimum(i, npt - 1), 0)),
            pl.BlockSpec((tm, D_MODEL), lambda i: (jnp.maximum(i - npt, 0), 0)),
            pl.BlockSpec((None, 1, N_MOD), lambda i: (mod_row_fn(i), 0, 0)),
            pl.BlockSpec((D_MODEL, LANES), lambda i: (0, 0)),
        ],
        out_specs=[
            pl.BlockSpec((tm, D_MODEL), lambda i: (i, 0)),
            pl.BlockSpec((tm, LANES), lambda i: (i, 0)),
            pl.BlockSpec((8, tm), lambda i: (0, i)),
            pl.BlockSpec((halves, 8, LANES), lambda i: (i, 0, 0)),
            pl.BlockSpec((8, LANES), lambda i: (0, 0)),
        ],
        out_shape=[
            jax.ShapeDtypeStruct((N_TOK, D_MODEL), BF16),
            jax.ShapeDtypeStruct((N_TOK, LANES), F32),
            jax.ShapeDtypeStruct((8, N_TOK), F32),
            jax.ShapeDtypeStruct((N_TOK // MOE_HALF, 8, LANES), F32),
            jax.ShapeDtypeStruct((8, LANES), F32),
        ],
        scratch_shapes=[pltpu.VMEM((8, LANES), F32)],
        compiler_params=_params(1),
        name="router",
    )(xp, xs, mod3, w_router_pad)


def _ffn_kernel(xp_ref, xs_ref, mod_ref, wg_ref, wu_ref, wd_ref, lng_ref, lnb_ref, o_ref, hscr, acc, *,
                n_prompt_tiles):
    i = pl.program_id(0)
    j = pl.program_id(1)
    first = j == 0
    last = j == pl.num_programs(1) - 1

    def modulate(x_ref):
        shift = mod_ref[:, 3 * D_MODEL:4 * D_MODEL]
        scale = mod_ref[:, 4 * D_MODEL:5 * D_MODEL]
        hscr[...] = (x_ref[...] * (1.0 + scale) + shift).astype(BF16)
        acc[...] = jnp.zeros_like(acc)

    @pl.when(first & (i < n_prompt_tiles))
    def _():
        modulate(xp_ref)

    @pl.when(first & (i >= n_prompt_tiles))
    def _():
        modulate(xs_ref)

    h = hscr[...]
    g = jnp.dot(h, wg_ref[...].astype(BF16), preferred_element_type=F32)
    u = jnp.dot(h, wu_ref[...].astype(BF16), preferred_element_type=F32)
    a = _silu(g) * u
    acc[...] += jnp.dot(a.astype(BF16), wd_ref[...].astype(BF16), preferred_element_type=F32)

    def finish(x_ref):
        gate = mod_ref[:, 5 * D_MODEL:6 * D_MODEL]
        y = DEEPNORM_ALPHA * x_ref[...] + gate * acc[...]
        o_ref[...] = _layer_norm(y, lng_ref[...], lnb_ref[...])

    @pl.when(last & (i < n_prompt_tiles))
    def _():
        finish(xp_ref)

    @pl.when(last & (i >= n_prompt_tiles))
    def _():
        finish(xs_ref)


def _ffn_call(xp, xs, mod3, mod_row_fn, wg, wu, wd, layer_idx, ln_g, ln_b):
    tm, tf = FFN_TM, FFN_TF
    npt = N_PROMPT // tm
    return pl.pallas_call(
        functools.partial(_ffn_kernel, n_prompt_tiles=npt),
        grid=(N_TOK // tm, D_FF // tf),
        in_specs=[
            pl.BlockSpec((tm, D_MODEL), lambda i, j: (jnp.minimum(i, npt - 1), 0)),
            pl.BlockSpec((tm, D_MODEL), lambda i, j: (jnp.maximum(i - npt, 0), 0)),
            pl.BlockSpec((None, 1, N_MOD), lambda i, j: (mod_row_fn(i), 0, 0)),
            pl.BlockSpec((None, D_MODEL, tf), lambda i, j: (layer_idx, 0, j)),
            pl.BlockSpec((None, D_MODEL, tf), lambda i, j: (layer_idx, 0, j)),
            pl.BlockSpec((None, tf, D_MODEL), lambda i, j: (layer_idx, j, 0)),
            pl.BlockSpec((1, D_MODEL), lambda i, j: (0, 0)),
            pl.BlockSpec((1, D_MODEL), lambda i, j: (0, 0)),
        ],
        out_specs=pl.BlockSpec((tm, D_MODEL), lambda i, j: (i, 0)),
        out_shape=jax.ShapeDtypeStruct((N_TOK, D_MODEL), F32),
        scratch_shapes=[pltpu.VMEM((tm, D_MODEL), BF16), pltpu.VMEM((tm, D_MODEL), F32)],
        compiler_params=_params(2),
        name="channel_mixer_dense",
    )(xp, xs, mod3, wg, wu, wd, ln_g, ln_b)


VISIT_FIELDS = 8
NO_ROW = -1.0e9


def _moe_ffn_kernel(vinfo, gwin, h_ref, mt_ref, wg_ref, wu_ref, wd_ref, y_ref, hs, gate_s, acc):
    v = pl.program_id(0)
    j = pl.program_id(1)
    tile = vinfo[v * VISIT_FIELDS + 0]
    expert = vinfo[v * VISIT_FIELDS + 1]
    valid = vinfo[v * VISIT_FIELDS + 2] == 1
    row_lo = vinfo[v * VISIT_FIELDS + 3]
    row_hi = vinfo[v * VISIT_FIELDS + 4]
    expert_row0 = vinfo[v * VISIT_FIELDS + 5]
    n_g = MOE_GATHER_CHUNKS

    def sub_rows(s):
        return slice(s * MOE_CHUNK, (s + 1) * MOE_CHUNK)

    def active(s):
        return (row_lo < (s + 1) * MOE_CHUNK) & (row_hi > s * MOE_CHUNK)

    def owns_start(s):
        return row_lo <= s * MOE_CHUNK

    @pl.when(valid & (j == 0))
    def _():
        row_iota = lax.broadcasted_iota(jnp.int32, (MOE_CHUNK, MOE_CHUNK), 0).astype(F32)
        expert_f = expert.astype(F32)
        for s in range(MOE_SUBS):
            rows = sub_rows(s)

            @pl.when(active(s) & owns_start(s))
            def _():
                hs[rows, :] = jnp.zeros((MOE_CHUNK, D_MODEL), BF16)
                gate_s[rows, :] = jnp.zeros((MOE_CHUNK, 1), F32)

            @pl.when(active(s))
            def _():
                acc[rows, :] = jnp.zeros((MOE_CHUNK, D_MODEL), F32)
                first_chunk = gwin[(v * MOE_SUBS + s) * 2]
                rank0 = (tile * MOE_TM + s * MOE_CHUNK - expert_row0).astype(F32)

                def body(w, carry):
                    want = first_chunk + w * n_g
                    cs = jnp.minimum(want, MOE_NCHUNK - n_g)
                    pieces = []
                    gate = jnp.zeros((MOE_CHUNK, 1), F32)
                    for k in range(n_g):
                        mt = mt_ref[cs + k]
                        first = mt[0:1] == expert_f
                        second = mt[1:2] == expert_f
                        rank = jnp.where(first, mt[2:3], jnp.where(second, mt[3:4], NO_ROW))
                        prob = jnp.where(first, mt[4:5], jnp.where(second, mt[5:6], 0.0))
                        rank = rank + jnp.where(cs + k >= want, 0.0, NO_ROW)
                        match = row_iota + rank0 == rank
                        pieces.append(jnp.where(match, 1.0, 0.0).astype(BF16))
                        gate = gate + jnp.sum(jnp.where(match, prob, 0.0), axis=-1, keepdims=True)
                    onehot = jnp.concatenate(pieces, axis=1)
                    start = pl.multiple_of(cs * MOE_CHUNK, MOE_CHUNK)
                    part = jnp.dot(onehot, h_ref[pl.ds(start, n_g * MOE_CHUNK), :], preferred_element_type=F32)
                    hs[rows, :] = hs[rows, :] + part.astype(BF16)
                    gate_s[rows, :] = gate_s[rows, :] + gate
                    return carry

                lax.fori_loop(0, gwin[(v * MOE_SUBS + s) * 2 + 1], body, 0)

    @pl.when(valid)
    def _():
        first_sub = row_lo // MOE_CHUNK
        n_active = (row_hi + MOE_CHUNK - 1) // MOE_CHUNK - first_sub
        for n in range(1, MOE_SUBS + 1):
            @pl.when(n_active == n)
            def _():
                rows = pl.ds(pl.multiple_of(first_sub * MOE_CHUNK, MOE_CHUNK), n * MOE_CHUNK)
                h = hs[rows, :]
                g = jnp.dot(h, wg_ref[...].astype(BF16), preferred_element_type=F32)
                u = jnp.dot(h, wu_ref[...].astype(BF16), preferred_element_type=F32)
                a = _silu(g) * u * gate_s[rows, :]
                acc[rows, :] += jnp.dot(a.astype(BF16), wd_ref[...].astype(BF16), preferred_element_type=F32)

    @pl.when(valid & (j == pl.num_programs(1) - 1))
    def _():
        for s in range(MOE_SUBS):
            rows = sub_rows(s)
            row = lax.broadcasted_iota(jnp.int32, (MOE_CHUNK, 1), 0) + s * MOE_CHUNK
            mine = jnp.where(row >= row_lo, 1.0, 0.0) * jnp.where(row < row_hi, 1.0, 0.0)

            @pl.when(active(s) & owns_start(s))
            def _():
                y_ref[rows, :] = (acc[rows, :] * mine).astype(BF16)

            @pl.when(active(s) & jnp.logical_not(owns_start(s)))
            def _():
                y_ref[rows, :] = jnp.where(mine > 0.0, acc[rows, :], y_ref[rows, :].astype(F32)).astype(BF16)


def _moe_ffn_call(vinfo, gwin, h, meta_chunks, wg, wu, wd):
    tm, tf = MOE_TM, FFN_TF
    n_j = D_FF // tf

    def expert_of(v, vinfo):
        return vinfo[v * VISIT_FIELDS + 1]

    def w_col(v, j, vinfo):
        return jnp.where(vinfo[v * VISIT_FIELDS + 2] == 1, j, n_j - 1)

    grid_spec = pltpu.PrefetchScalarGridSpec(
        num_scalar_prefetch=2,
        grid=(MOE_MAX_VISITS, n_j),
        in_specs=[
            pl.BlockSpec((N_TOK, D_MODEL), lambda v, j, vinfo, gwin: (0, 0), pipeline_mode=pl.Buffered(1)),
            pl.BlockSpec((MOE_NCHUNK, 8, MOE_CHUNK), lambda v, j, vinfo, gwin: (0, 0, 0),
                         pipeline_mode=pl.Buffered(1)),
            pl.BlockSpec((None, D_MODEL, tf),
                         lambda v, j, vinfo, gwin: (expert_of(v, vinfo), 0, w_col(v, j, vinfo))),
            pl.BlockSpec((None, D_MODEL, tf),
                         lambda v, j, vinfo, gwin: (expert_of(v, vinfo), 0, w_col(v, j, vinfo))),
            pl.BlockSpec((None, tf, D_MODEL),
                         lambda v, j, vinfo, gwin: (expert_of(v, vinfo), w_col(v, j, vinfo), 0)),
        ],
        out_specs=pl.BlockSpec((tm, D_MODEL), lambda v, j, vinfo, gwin: (vinfo[v * VISIT_FIELDS], 0)),
        scratch_shapes=[pltpu.VMEM((tm, D_MODEL), BF16), pltpu.VMEM((tm, 1), F32),
                        pltpu.VMEM((tm, D_MODEL), F32)],
    )
    return pl.pallas_call(
        _moe_ffn_kernel,
        grid_spec=grid_spec,
        out_shape=jax.ShapeDtypeStruct((MOE_ROWS, D_MODEL), BF16),
        compiler_params=_params(2),
        name="channel_mixer_experts",
    )(vinfo, gwin, h, meta_chunks, wg, wu, wd)


def _combine_kernel(cstart, row0, xp_ref, xs_ref, meta_ref, y_ref, mod_ref, lng_ref, lnb_ref, op_ref, os_ref, *,
                    n_prompt_tiles):
    c = pl.program_id(0)

    def run(x_ref, o_ref):
        gate = mod_ref[:, 5 * D_MODEL:6 * D_MODEL]
        col = lax.broadcasted_iota(jnp.int32, (MOE_HALF, MOE_WINDOW), 1).astype(F32)
        for half in range(MOE_CHUNK // MOE_HALF):
            rows = slice(half * MOE_HALF, (half + 1) * MOE_HALF)
            meta = meta_ref[rows, :]
            e1, e2, r1, r2 = meta[:, 0:1], meta[:, 1:2], meta[:, 2:3], meta[:, 3:4]
            total = None
            for e in range(N_EXPERTS):
                start = pl.multiple_of(cstart[(c * 2 + half) * N_EXPERTS + e], MOE_HALF)
                rank = jnp.where(e1 == float(e), r1, jnp.where(e2 == float(e), r2, NO_ROW))
                onehot = jnp.where(col == rank + (row0[e] - start).astype(F32), 1.0, 0.0).astype(BF16)
                part = jnp.dot(onehot, y_ref[pl.ds(start, MOE_WINDOW), :], preferred_element_type=F32)
                total = part if total is None else total + part
            y = DEEPNORM_ALPHA * x_ref[rows, :] + gate * total
            o_ref[rows, :] = _layer_norm(y, lng_ref[...], lnb_ref[...])

    @pl.when(c < n_prompt_tiles)
    def _():
        run(xp_ref, op_ref)

    @pl.when(c >= n_prompt_tiles)
    def _():
        run(xs_ref, os_ref)


def _combine_call(cstart, row0, xp, xs, meta, y, mod3, mod_row_fn, ln_g, ln_b):
    tm = MOE_CHUNK
    npt = N_PROMPT // tm
    grid_spec = pltpu.PrefetchScalarGridSpec(
        num_scalar_prefetch=2,
        grid=(MOE_NCHUNK,),
        in_specs=[
            pl.BlockSpec((tm, D_MODEL), lambda c, a, b: (jnp.minimum(c, npt - 1), 0)),
            pl.BlockSpec((tm, D_MODEL), lambda c, a, b: (jnp.maximum(c - npt, 0), 0)),
            pl.BlockSpec((tm, LANES), lambda c, a, b: (c, 0)),
            pl.BlockSpec((MOE_ROWS, D_MODEL), lambda c, a, b: (0, 0), pipeline_mode=pl.Buffered(1)),
            pl.BlockSpec((None, 1, N_MOD), lambda c, a, b: (mod_row_fn(c), 0, 0)),
            pl.BlockSpec((1, D_MODEL), lambda c, a, b: (0, 0)),
            pl.BlockSpec((1, D_MODEL), lambda c, a, b: (0, 0)),
        ],
        out_specs=[
            pl.BlockSpec((tm, D_MODEL), lambda c, a, b: (jnp.minimum(c, npt - 1), 0)),
            pl.BlockSpec((tm, D_MODEL), lambda c, a, b: (jnp.maximum(c - npt, 0), 0)),
        ],
    )
    return pl.pallas_call(
        functools.partial(_combine_kernel, n_prompt_tiles=npt),
        grid_spec=grid_spec,
        out_shape=[jax.ShapeDtypeStruct((N_PROMPT, D_MODEL), F32),
                   jax.ShapeDtypeStruct((N_SAMPLE, D_MODEL), F32)],
        compiler_params=_params(1),
        name="expert_combine",
    )(cstart, row0, xp, xs, meta, y, mod3, ln_g, ln_b)


def _moe_plan(cum, total):
    i32 = jnp.int32
    cnt = total[0, :N_EXPERTS].astype(i32)
    off = jnp.cumsum(cnt) - cnt
    cumh = jnp.concatenate([cum[:, 0, :N_EXPERTS], total[0:1, :N_EXPERTS]], axis=0).astype(i32)
    cumc = cumh[::MOE_CHUNK // MOE_HALF]

    n_tiles = MOE_ROWS // MOE_TM
    t0 = jnp.arange(n_tiles, dtype=i32)[:, None] * MOE_TM
    lo = jnp.maximum(t0, off[None, :]).reshape(-1)
    hi = jnp.minimum(t0 + MOE_TM, (off + cnt)[None, :]).reshape(-1)
    ok = hi > lo
    n_visits = jnp.sum(ok.astype(i32))
    order = jnp.argsort(jnp.logical_not(ok), stable=True)[:MOE_MAX_VISITS].astype(i32)
    slot = jnp.arange(MOE_MAX_VISITS, dtype=i32)
    valid = slot < n_visits
    order = order[jnp.minimum(slot, n_visits - 1)]
    vt, ve = order // N_EXPERTS, order % N_EXPERTS
    vlo = jnp.where(valid, lo[order] - vt * MOE_TM, 0)
    vhi = jnp.where(valid, hi[order] - vt * MOE_TM, 0)
    zero = jnp.zeros_like(vt)
    vinfo = jnp.stack([vt, ve, valid.astype(i32), vlo, vhi, off[ve], zero, zero], axis=1).reshape(-1).astype(i32)

    s0 = jnp.arange(MOE_SUBS, dtype=i32)[None, :] * MOE_CHUNK
    rlo = jnp.maximum(vlo[:, None], s0)
    rhi = jnp.minimum(vhi[:, None], s0 + MOE_CHUNK)
    to_rank = (vt * MOE_TM - off[ve])[:, None]
    cum_v = cumc[:, ve]
    c_lo = jnp.sum((cum_v[:, :, None] <= (rlo + to_rank)[None]).astype(i32), axis=0) - 1
    c_hi = jnp.sum((cum_v[:, :, None] < (rhi + to_rank)[None]).astype(i32), axis=0) - 1
    c_lo = jnp.clip(c_lo, 0, MOE_NCHUNK - 1)
    c_hi = jnp.clip(c_hi, 0, MOE_NCHUNK - 1)
    n_win = jnp.where(rhi <= rlo, 0, (c_hi - c_lo) // MOE_GATHER_CHUNKS + 1)
    gwin = jnp.stack([c_lo, n_win], axis=-1).reshape(-1).astype(i32)

    seg_lo = off[None, :] + cumh[:-1]
    cstart = jnp.clip((seg_lo // MOE_HALF) * MOE_HALF, 0, MOE_ROWS - MOE_WINDOW)
    return vinfo, gwin, cstart.reshape(-1).astype(i32), off.astype(i32)


def _rope_tables():
    rows = DEC_SEQ // GRID_W
    row = jnp.repeat(jnp.arange(rows, dtype=F32), GRID_W)
    col = jnp.tile(jnp.arange(GRID_W, dtype=F32), rows)
    n_freq = HEAD_DIM // 4
    inv = ROPE_THETA ** (-jnp.arange(n_freq, dtype=F32) / n_freq)
    ar = row[:, None] * inv
    ac = col[:, None] * inv
    ang = jnp.concatenate([ar, ar, ac, ac], axis=-1)
    cos = jnp.tile(jnp.cos(ang), (1, LANES // HEAD_DIM))
    sin = jnp.tile(jnp.sin(ang), (1, LANES // HEAD_DIM))
    first_half = (jnp.arange(LANES) % (2 * n_freq)) < n_freq
    sin_next = jnp.where(first_half, -sin, 0.0)
    sin_prev = jnp.where(first_half, 0.0, sin)
    return cos, sin_next, sin_prev


def kernel(x_prompt, x_sample, cache_a_k, cache_a_v, cache_b_k, cache_b_v, c, c_ctx, w_ada, b_ada, w_in,
           w_out, diff_lambda, diff_subln, qk_norm_gain, ln_gain, ln_bias, w_ffn_gate, w_ffn_up,
           w_ffn_down, w_router, w_moe_gate, w_moe_up, w_moe_down):
    cond = jnp.zeros((MOD_ROWS, D_MODEL), F32).at[0].set(c_ctx).at[1:1 + DEC_BATCH].set(c)
    mod3 = _ada_call(cond, w_ada, b_ada).reshape(DEPTH * MOD_ROWS, 1, N_MOD)
    rope_tabs = _rope_tables()
    cache = (cache_a_k.reshape(DEC_BATCH, DEPTH, PAST_LEN, DIFF_WIDTH),
             cache_a_v.reshape(DEC_BATCH, DEPTH, PAST_LEN, DIFF_WIDTH),
             cache_b_k.reshape(DEC_BATCH, DEPTH, PAST_LEN, KV_WIDTH),
             cache_b_v.reshape(DEC_BATCH, DEPTH, PAST_LEN, KV_WIDTH))

    xp = x_prompt.reshape(N_PROMPT, D_MODEL)
    xs = x_sample.reshape(N_SAMPLE, D_MODEL)
    xs_off = 0
    new_caches = []
    for l in range(DEPTH):
        base = l * MOD_ROWS
        gq = jnp.tile(qk_norm_gain[l, 0], LANES // HEAD_DIM).reshape(1, LANES)
        gk = jnp.tile(qk_norm_gain[l, 1], LANES // HEAD_DIM).reshape(1, LANES)
        subln = diff_subln[l].reshape(1, LANES)
        ln1_g, ln1_b = ln_gain[l, 0].reshape(1, D_MODEL), ln_bias[l, 0].reshape(1, D_MODEL)
        ln2_g, ln2_b = ln_gain[l, 1].reshape(1, D_MODEL), ln_bias[l, 1].reshape(1, D_MODEL)

        q_p, kv_p, ak, av, bk, bv = _inproj_call(
            xp, 0, N_PROMPT, mod3, lambda i: base, w_in, gq, gk, l, None, True)
        new_caches.append((ak, av, bk, bv))
        x1_p = _attn_call(q_p, kv_p, None, xp, 0, mod3, lambda b: base, w_out, diff_lambda, subln,
                          ln1_g, ln1_b, l, BATCH, SEQ)
        s_tiles = DEC_SEQ // PROJ_TM
        q_s, kv_s = _inproj_call(
            xs, xs_off, N_SAMPLE, mod3, lambda i: base + 1 + i // s_tiles, w_in, gq, gk, l, rope_tabs, False)
        x1_s = _attn_call(q_s, kv_s, cache, xs, xs_off, mod3, lambda b: base + 1 + b, w_out, diff_lambda,
                          subln, ln1_g, ln1_b, l, DEC_BATCH, DEC_SEQ)

        npt = N_PROMPT // FFN_TM
        s_ffn_tiles = DEC_SEQ // FFN_TM
        mod_row = lambda i: base + jnp.where(i < npt, 0, 1 + (i - npt) // s_ffn_tiles)
        k = l // 2
        if l % 2 == 0:
            x2 = _ffn_call(x1_p, x1_s, mod3, mod_row, w_ffn_gate, w_ffn_up, w_ffn_down, k, ln2_g, ln2_b)
        else:
            wr = jnp.zeros((D_MODEL, LANES), F32).at[:, :N_EXPERTS].set(w_router[k])
            h2, meta, meta_t, cum, total = _router_call(x1_p, x1_s, mod3, mod_row, wr)
            vinfo, gwin, cstart, row0 = _moe_plan(cum, total)
            meta_chunks = meta_t.reshape(8, MOE_NCHUNK, MOE_CHUNK).transpose(1, 0, 2)
            y = _moe_ffn_call(vinfo, gwin, h2, meta_chunks, w_moe_gate[k], w_moe_up[k], w_moe_down[k])
            npc = N_PROMPT // MOE_CHUNK
            s_chunks = DEC_SEQ // MOE_CHUNK
            mod_row_c = lambda c: base + jnp.where(c < npc, 0, 1 + (c - npc) // s_chunks)
            xp, xs = _combine_call(cstart, row0, x1_p, x1_s, meta, y, mod3, mod_row_c, ln2_g, ln2_b)
            xs_off = 0
            continue
        xp, xs, xs_off = x2, x2, N_PROMPT

    y_prompt = xp[:N_PROMPT].reshape(BATCH, SEQ, D_MODEL)
    y_sample = xs[xs_off:xs_off + N_SAMPLE].reshape(DEC_BATCH, DEC_SEQ, D_MODEL)
    stack = lambda idx, shape: jnp.stack([nc[idx] for nc in new_caches], axis=0).reshape(
        (DEPTH, BATCH, SEQ) + shape).transpose((1, 0, 2) + tuple(range(3, 3 + len(shape))))
    return (y_prompt, y_sample,
            stack(0, (DIFF_HEADS, 2, HEAD_DIM)), stack(1, (DIFF_HEADS, 2 * HEAD_DIM)),
            stack(2, (GQA_KV_HEADS, HEAD_DIM)), stack(3, (GQA_KV_HEADS, HEAD_DIM)))
```
